```python
import functools
import jax, jax.numpy as jnp
from jax import lax
import numpy as np

D_MODEL = 1024
BATCH = 16
SEQ = 4096
DEPTH = 4
DEC_BATCH = 32
DEC_SEQ = 16
PAST_LEN = 2048

CHUNK = 64
N_EVEN = (DEPTH + 1) // 2
N_ODD = DEPTH // 2

A_HEADS = 16
A_HEAD_DIM = 64
A_WIDTH = A_HEADS * A_HEAD_DIM
A_LORA_W = 64
A_LORA_A = 64
A_LORA_G = 128
A_COLS = 3 * A_WIDTH + A_LORA_W + A_LORA_A + A_LORA_G
A_SPLITS = (A_WIDTH, 2 * A_WIDTH, 3 * A_WIDTH, 3 * A_WIDTH + A_LORA_W, 3 * A_WIDTH + A_LORA_W + A_LORA_A)

B_HEADS = 8
B_QK_DIM = 64
B_V_DIM = 128
B_QK_WIDTH = B_HEADS * B_QK_DIM
B_V_WIDTH = B_HEADS * B_V_DIM
B_CONV_W = 4
B_COLS = 2 * B_QK_WIDTH + 2 * B_V_WIDTH + 2 * B_HEADS
B_SPLITS = (2 * B_QK_WIDTH, 2 * B_QK_WIDTH + B_V_WIDTH, 2 * B_QK_WIDTH + 2 * B_V_WIDTH,
            2 * B_QK_WIDTH + 2 * B_V_WIDTH + B_HEADS)
EVEN_COLS = A_COLS + B_COLS
EVEN_MIX_WIDTH = A_WIDTH + B_V_WIDTH

C_HEADS = 8
C_KV_HEADS = 2
C_HEAD_DIM = 128
C_GROUP = C_HEADS // C_KV_HEADS
C_WIDTH = C_HEADS * C_HEAD_DIM
IDX_HEADS = 8
IDX_DIM = 64
TOPK_MAX = 256
C_SPLITS = (C_WIDTH, C_WIDTH + C_KV_HEADS * C_HEAD_DIM, C_WIDTH + 2 * C_KV_HEADS * C_HEAD_DIM,
            C_WIDTH + 2 * C_KV_HEADS * C_HEAD_DIM + IDX_HEADS * IDX_DIM,
            C_WIDTH + 2 * C_KV_HEADS * C_HEAD_DIM + IDX_HEADS * IDX_DIM + IDX_DIM)
C_COLS = C_SPLITS[-1] + IDX_HEADS

D_FF = 2816
N_EXPERTS = 8
TOP_K_EXPERTS = 2
D_FF_EXPERT = 1408

ALPHA = (2 * DEPTH) ** 0.25
BETA = (8 * DEPTH) ** -0.25
LN_EPS = 1e-5

kernel_name = 'hybrid_rwkv7_mlstm_dsa_stream_step'


def layer_norm(x, g, b):
    xf = x.astype(jnp.float32)
    mu = xf.mean(-1, keepdims=True)
    var = jnp.mean(jnp.square(xf - mu), -1, keepdims=True)
    return ((xf - mu) * lax.rsqrt(var + LN_EPS) * g + b).astype(x.dtype)


def head_norm(xh, g, b):
    B, T = xh.shape[:2]
    xf = xh.astype(jnp.float32)
    mu = xf.mean(-1, keepdims=True)
    var = jnp.mean(jnp.square(xf - mu), -1, keepdims=True)
    y = ((xf - mu) * lax.rsqrt(var + LN_EPS)).reshape(B, T, -1)
    return (y * g + b).astype(g.dtype)


def causal_window(u, buf):
    padded = jnp.concatenate([buf.astype(u.dtype), u], axis=1)
    return padded, padded[:, padded.shape[1] - buf.shape[1]:]


def swiglu(x, wg, wu, wd):
    return (jax.nn.silu(x @ wg) * (x @ wu)) @ wd


def wkv7_scan(r, decay, k, v, a, b, s0):
    def step(s, inp):
        r_t, w_t, k_t, v_t, a_t, b_t = inp
        sa = jnp.einsum('bhvk,bhk->bhv', s, a_t)
        s = s * w_t[:, :, None, :] + sa[..., None] * b_t[:, :, None, :] + v_t[..., None] * k_t[:, :, None, :]
        return s, jnp.einsum('bhvk,bhk->bhv', s, r_t)
    xs = tuple(jnp.moveaxis(t.astype(jnp.float32), 1, 0) for t in (r, decay, k, v, a, b))
    s_fin, ys = lax.scan(step, s0.astype(jnp.float32), xs)
    return jnp.moveaxis(ys, 0, 1), s_fin


def rwkv7_mixer(ua, shift0, wkv0, w):
    B, T, _ = ua.shape
    padded, new_shift = causal_window(ua, shift0)
    xs = ua + (padded[:, :T] - ua) * w['mu']
    r, k, v, dw, da, dg = jnp.split(xs, A_SPLITS, axis=-1)
    logw = -jax.nn.softplus(-(w['w0'] + jnp.tanh(dw) @ w['w2'])) - 0.5
    decay = jnp.exp(-jnp.exp(logw.astype(jnp.float32)))
    a = jax.nn.sigmoid(w['a0'] + da @ w['a2'])
    g = jax.nn.sigmoid(dg) @ w['g2']
    heads = lambda t: t.reshape(B, T, A_HEADS, A_HEAD_DIM)
    kk = heads(k * w['k_k']).astype(jnp.float32)
    kk = kk * lax.rsqrt(jnp.maximum(jnp.sum(kk * kk, -1, keepdims=True), 1e-24))
    k = k * (1 + (a - 1) * w['k_a'])
    rh, kh, vh, ah = heads(r), heads(k), heads(v), heads(a)
    y, wkv = wkv7_scan(rh, heads(decay), kh, vh, -kk, kk * ah, wkv0)
    y = head_norm(y, w['a_ln_g'], w['a_ln_b'])
    bonus = jnp.sum(rh * kh * w['r_k'], -1, keepdims=True) * vh
    out = (y + bonus.reshape(B, T, A_WIDTH)) * g
    return out, new_shift, wkv.astype(ua.dtype)


def mlstm_chunkwise(q, k, v, logi, logf, c0, n0, m0):
    B, T, H, Dk = q.shape
    Dv = v.shape[-1]
    L = min(CHUNK, T)
    nc = T // L
    tril = jnp.tril(jnp.ones((L, L), dtype=bool))

    def blocks(t):
        return jnp.moveaxis(t.astype(jnp.float32).reshape(B, nc, L, *t.shape[2:]), 1, 0)

    def step(carry, inp):
        c, n, m = carry
        qc, kc, vc, ic, fc = inp
        bcum = jnp.cumsum(fc, axis=1)
        gtot = bcum[:, -1]
        dmat = bcum[:, :, None, :] - bcum[:, None, :, :] + ic[:, None, :, :]
        dmat = jnp.where(tril[None, :, :, None], dmat, -jnp.inf)
        inter = bcum + m[:, None, :]
        m_t = jnp.maximum(inter, dmat.max(axis=2))
        w_intra = jnp.exp(dmat - m_t[:, :, None, :])
        w_inter = jnp.exp(inter - m_t)
        s = jnp.einsum('bthd,bshd->btsh', qc, kc) * w_intra
        num = jnp.einsum('btsh,bshv->bthv', s, vc) + w_inter[..., None] * jnp.einsum('bhvd,bthd->bthv', c, qc)
        den = s.sum(axis=2) + w_inter * jnp.einsum('bhd,bthd->bth', n, qc)
        h = num / jnp.maximum(jnp.abs(den), jnp.exp(-m_t))[..., None]
        lw = gtot[:, None, :] - bcum + ic
        m_new = jnp.maximum(gtot + m, lw.max(axis=1))
        w_s = jnp.exp(lw - m_new[:, None, :])
        dec = jnp.exp(gtot + m - m_new)
        c_new = dec[..., None, None] * c + jnp.einsum('bsh,bshv,bshd->bhvd', w_s, vc, kc)
        n_new = dec[..., None] * n + jnp.einsum('bsh,bshd->bhd', w_s, kc)
        return (c_new, n_new, m_new), h

    init = (c0.astype(jnp.float32), n0.astype(jnp.float32), m0.astype(jnp.float32))
    (c, n, m), h = lax.scan(step, init, tuple(blocks(t) for t in (q, k, v, logi, logf)))
    return jnp.moveaxis(h, 0, 1).reshape(B, T, H, Dv), c, n, m


def mlstm_mixer(ub, conv0, c0, n0, m0, w):
    B, T, _ = ub.shape
    uqk, uv, uo, ui, uf = jnp.split(ub, B_SPLITS, axis=-1)
    padded, new_conv = causal_window(uqk, conv0)
    qk = jax.nn.silu(sum(padded[:, j:j + T] * w['conv_w'][j] for j in range(B_CONV_W)) + w['conv_b'])
    q = qk[..., :B_QK_WIDTH].reshape(B, T, B_HEADS, B_QK_DIM)
    k = qk[..., B_QK_WIDTH:].reshape(B, T, B_HEADS, B_QK_DIM) * (B_QK_DIM ** -0.5)
    v = uv.reshape(B, T, B_HEADS, B_V_DIM)
    logi = (ui + w['b_i']).astype(jnp.float32)
    logf = jax.nn.log_sigmoid((uf + w['b_f']).astype(jnp.float32))
    h, c, n, m = mlstm_chunkwise(q, k, v, logi, logf, c0, n0, m0)
    out = jax.nn.sigmoid(uo) * head_norm(h, w['b_ln_g'], w['b_ln_b'])
    return out, new_conv, c.astype(ub.dtype), n.astype(ub.dtype), m.astype(ub.dtype)


def even_layer(x, shift0, wkv0, conv0, c0, n0, m0, w):
    u = x @ w['w_in']
    ya, shift, wkv = rwkv7_mixer(u[..., :A_COLS], shift0, wkv0, w)
    yb, conv, c, n, m = mlstm_mixer(u[..., A_COLS:], conv0, c0, n0, m0, w)
    y = jnp.concatenate([ya, yb], axis=-1) @ w['w_out']
    x = layer_norm(ALPHA * x + y, w['ln1_g'], w['ln1_b'])
    x = layer_norm(ALPHA * x + swiglu(x, w['ffn_gate'], w['ffn_up'], w['ffn_down']), w['ln2_g'], w['ln2_b'])
    return x, (shift, wkv, conv, c, n, m)


def dsa_select_attend(q, qi, wi, keys_k, keys_v, keys_i, limit, top_k):
    B, Tq = q.shape[:2]
    S = keys_k.shape[1]
    rel = jax.nn.relu(jnp.einsum('bqhd,bsd->bqhs', qi, keys_i))
    idx_score = jnp.einsum('bqh,bqhs->bqs', wi, rel).astype(jnp.float32)
    visible = jnp.arange(S) < limit
    idx_score = jnp.where(visible[None, None, :], idx_score, -jnp.inf)
    _, sel = lax.top_k(idx_score, top_k)
    valid = sel < limit
    gather = jax.vmap(lambda rows, ids: rows[ids])
    k_sel = gather(keys_k, sel)
    v_sel = gather(keys_v, sel)
    qg = q.reshape(B, Tq, C_KV_HEADS, C_GROUP, C_HEAD_DIM)
    logits = jnp.einsum('bqhgd,bqkhd->bqhgk', qg, k_sel).astype(jnp.float32) * (C_HEAD_DIM ** -0.5)
    logits = jnp.where(valid[:, :, None, None, :], logits, -jnp.inf)
    p = jax.nn.softmax(logits, axis=-1).astype(v_sel.dtype)
    out = jnp.einsum('bqhgk,bqkhd->bqhgd', p, v_sel)
    return out.reshape(B, Tq, C_WIDTH)


def dsa_prompt(q, qi, wi, k, v, ki):
    B, T = q.shape[:2]
    n_blk = T // CHUNK
    top_k = min(TOPK_MAX, T // 4)
    to_blocks = lambda t: jnp.moveaxis(t.reshape(B, n_blk, CHUNK, *t.shape[2:]), 1, 0)

    def blk(args):
        c, qb, qib, wib = args
        return dsa_select_attend(qb, qib, wib, k, v, ki, (c + 1) * CHUNK, top_k)

    out = lax.map(blk, (jnp.arange(n_blk), to_blocks(q), to_blocks(qi), to_blocks(wi)))
    return jnp.moveaxis(out, 0, 1).reshape(B, T, C_WIDTH)


def moe_swiglu(x, w_router, b_router, w_gate, w_up, w_down):
    logits = (x @ w_router).astype(jnp.float32) + b_router
    top_val, top_idx = lax.top_k(logits, TOP_K_EXPERTS)
    gates = jax.nn.softmax(top_val, axis=-1)
    combine = jnp.einsum('btk,btke->bte', gates, jax.nn.one_hot(top_idx, N_EXPERTS, dtype=jnp.float32))
    y = jnp.zeros_like(x)
    for e in range(N_EXPERTS):
        y = y + combine[..., e:e + 1].astype(x.dtype) * swiglu(x, w_gate[e], w_up[e], w_down[e])
    return y


def odd_layer(x, past_k, past_v, past_ki, w):
    B, T, _ = x.shape
    u = x @ w['w_in']
    q, k, v, qi, ki, wi = jnp.split(u, C_SPLITS, axis=-1)
    q = q.reshape(B, T, C_HEADS, C_HEAD_DIM)
    k = k.reshape(B, T, C_KV_HEADS, C_HEAD_DIM)
    v = v.reshape(B, T, C_KV_HEADS, C_HEAD_DIM)
    qi = qi.reshape(B, T, IDX_HEADS, IDX_DIM)
    wi = wi * (IDX_HEADS ** -0.5)
    if past_k is None:
        att = dsa_prompt(q, qi, wi, k, v, ki)
    else:
        keys_k = jnp.concatenate([past_k.astype(k.dtype), k], axis=1)
        keys_v = jnp.concatenate([past_v.astype(v.dtype), v], axis=1)
        keys_i = jnp.concatenate([past_ki.astype(ki.dtype), ki], axis=1)
        L = keys_k.shape[1]
        att = dsa_select_attend(q, qi, wi, keys_k, keys_v, keys_i, L, min(TOPK_MAX, L // 4))
    x = layer_norm(ALPHA * x + att @ w['w_out'], w['ln1_g'], w['ln1_b'])
    ffn = moe_swiglu(x, w['router'], w['router_b'], w['e_gate'], w['e_up'], w['e_down'])
    x = layer_norm(ALPHA * x + ffn, w['ln2_g'], w['ln2_b'])
    return x, (k, v, ki)


def run_trunk(x, shift0, wkv0, conv0, c0, n0, m0, past_k, past_v, past_ki, ew, ow):
    even_names = ('shift', 'wkv', 'conv', 'c', 'n', 'm')
    odd_names = ('k', 'v', 'ki')
    outs = {name: [] for name in even_names + odd_names}
    for layer in range(DEPTH):
        p = layer // 2
        if layer % 2 == 0:
            w = {name: arr[p] for name, arr in ew.items()}
            x, st = even_layer(x, shift0[p], wkv0[p], conv0[p], c0[p], n0[p], m0[p], w)
            for name, val in zip(even_names, st):
                outs[name].append(val)
        else:
            w = {name: arr[p] for name, arr in ow.items()}
            if past_k is None:
                x, st = odd_layer(x, None, None, None, w)
            else:
                x, st = odd_layer(x, past_k[p], past_v[p], past_ki[p], w)
            for name, val in zip(odd_names, st):
                outs[name].append(val)
    return x, {name: jnp.stack(vals) for name, vals in outs.items()}


def setup_inputs(seed: int = 0) -> dict:
    key = jax.random.key(seed)
    ks = iter(jax.random.split(key, 64))
    f32 = jnp.float32

    def nrm(shape, scale=1.0):
        return scale * jax.random.normal(next(ks), shape, f32)

    def uni(shape, lo, hi):
        return jax.random.uniform(next(ks), shape, f32, lo, hi)

    E, O, D = N_EVEN, N_ODD, D_MODEL
    return {
        'x_prompt': nrm((BATCH, SEQ, D)),
        'x_sample': nrm((DEC_BATCH, DEC_SEQ, D)),
        'state_shift': nrm((E, DEC_BATCH, 1, A_COLS)),
        'state_wkv': nrm((E, DEC_BATCH, A_HEADS, A_HEAD_DIM, A_HEAD_DIM), 0.5),
        'state_conv': nrm((E, DEC_BATCH, B_CONV_W - 1, 2 * B_QK_WIDTH)),
        'state_c': nrm((E, DEC_BATCH, B_HEADS, B_V_DIM, B_QK_DIM), 0.5),
        'state_n': nrm((E, DEC_BATCH, B_HEADS, B_QK_DIM), 0.5),
        'state_m': nrm((E, DEC_BATCH, B_HEADS)),
        'cache_k': nrm((O, DEC_BATCH, PAST_LEN, C_KV_HEADS, C_HEAD_DIM)),
        'cache_v': nrm((O, DEC_BATCH, PAST_LEN, C_KV_HEADS, C_HEAD_DIM)),
        'cache_idx_k': nrm((O, DEC_BATCH, PAST_LEN, IDX_DIM)),
        'ev_w_in': nrm((E, D, EVEN_COLS), D ** -0.5),
        'a_mu': uni((E, A_COLS), 0.0, 1.0),
        'a_w0': uni((E, A_WIDTH), -6.0, -1.0),
        'a_w2': nrm((E, A_LORA_W, A_WIDTH), 0.1 * A_LORA_W ** -0.5),
        'a_a0': nrm((E, A_WIDTH), 0.1),
        'a_a2': nrm((E, A_LORA_A, A_WIDTH), A_LORA_A ** -0.5),
        'a_g2': nrm((E, A_LORA_G, A_WIDTH), A_LORA_G ** -0.5),
        'a_k_k': 0.85 + nrm((E, A_WIDTH), 0.05),
        'a_k_a': 1.0 + nrm((E, A_WIDTH), 0.05),
        'a_r_k': nrm((E, A_HEADS, A_HEAD_DIM), 0.1),
        'a_ln_g': 1.0 + nrm((E, A_WIDTH), 0.05),
        'a_ln_b': nrm((E, A_WIDTH), 0.01),
        'b_conv_w': nrm((E, B_CONV_W, 2 * B_QK_WIDTH), B_CONV_W ** -0.5),
        'b_conv_b': nrm((E, 2 * B_QK_WIDTH), 0.01),
        'b_i_bias': nrm((E, B_HEADS), 0.1) - 3.0,
        'b_f_bias': uni((E, B_HEADS), 3.0, 6.0),
        'b_ln_g': 1.0 + nrm((E, B_V_WIDTH), 0.05),
        'b_ln_b': nrm((E, B_V_WIDTH), 0.01),
        'ev_w_out': nrm((E, EVEN_MIX_WIDTH, D), BETA * EVEN_MIX_WIDTH ** -0.5),
        'ev_ln1_g': 1.0 + nrm((E, D), 0.05),
        'ev_ln1_b': nrm((E, D), 0.01),
        'ffn_w_gate': nrm((E, D, D_FF), D ** -0.5),
        'ffn_w_up': nrm((E, D, D_FF), D ** -0.5),
        'ffn_w_down': nrm((E, D_FF, D), BETA * D_FF ** -0.5),
        'ev_ln2_g': 1.0 + nrm((E, D), 0.05),
        'ev_ln2_b': nrm((E, D), 0.01),
        'od_w_in': nrm((O, D, C_COLS), D ** -0.5),
        'od_w_out': nrm((O, C_WIDTH, D), BETA * C_WIDTH ** -0.5),
        'od_ln1_g': 1.0 + nrm((O, D), 0.05),
        'od_ln1_b': nrm((O, D), 0.01),
        'moe_w_router': nrm((O, D, N_EXPERTS), D ** -0.5),
        'moe_b_router': nrm((O, N_EXPERTS), 0.01),
        'moe_w_gate': nrm((O, N_EXPERTS, D, D_FF_EXPERT), D ** -0.5),
        'moe_w_up': nrm((O, N_EXPERTS, D, D_FF_EXPERT), D ** -0.5),
        'moe_w_down': nrm((O, N_EXPERTS, D_FF_EXPERT, D), BETA * D_FF_EXPERT ** -0.5),
        'od_ln2_g': 1.0 + nrm((O, D), 0.05),
        'od_ln2_b': nrm((O, D), 0.01),
    }


def reference(x_prompt, x_sample, state_shift, state_wkv, state_conv, state_c, state_n, state_m,
              cache_k, cache_v, cache_idx_k,
              ev_w_in, a_mu, a_w0, a_w2, a_a0, a_a2, a_g2, a_k_k, a_k_a, a_r_k, a_ln_g, a_ln_b,
              b_conv_w, b_conv_b, b_i_bias, b_f_bias, b_ln_g, b_ln_b, ev_w_out, ev_ln1_g, ev_ln1_b,
              ffn_w_gate, ffn_w_up, ffn_w_down, ev_ln2_g, ev_ln2_b,
              od_w_in, od_w_out, od_ln1_g, od_ln1_b, moe_w_router, moe_b_router,
              moe_w_gate, moe_w_up, moe_w_down, od_ln2_g, od_ln2_b):
    ew = {'w_in': ev_w_in, 'mu': a_mu, 'w0': a_w0, 'w2': a_w2, 'a0': a_a0, 'a2': a_a2, 'g2': a_g2,
          'k_k': a_k_k, 'k_a': a_k_a, 'r_k': a_r_k, 'a_ln_g': a_ln_g, 'a_ln_b': a_ln_b,
          'conv_w': b_conv_w, 'conv_b': b_conv_b, 'b_i': b_i_bias, 'b_f': b_f_bias,
          'b_ln_g': b_ln_g, 'b_ln_b': b_ln_b, 'w_out': ev_w_out, 'ln1_g': ev_ln1_g, 'ln1_b': ev_ln1_b,
          'ffn_gate': ffn_w_gate, 'ffn_up': ffn_w_up, 'ffn_down': ffn_w_down,
          'ln2_g': ev_ln2_g, 'ln2_b': ev_ln2_b}
    ow = {'w_in': od_w_in, 'w_out': od_w_out, 'ln1_g': od_ln1_g, 'ln1_b': od_ln1_b,
          'router': moe_w_router, 'router_b': moe_b_router, 'e_gate': moe_w_gate, 'e_up': moe_w_up,
          'e_down': moe_w_down, 'ln2_g': od_ln2_g, 'ln2_b': od_ln2_b}
    bp = x_prompt.shape[0]
    z = functools.partial(jnp.zeros, dtype=x_prompt.dtype)
    y_prompt, sp = run_trunk(x_prompt,
                             z((N_EVEN, bp, 1, A_COLS)),
                             z((N_EVEN, bp, A_HEADS, A_HEAD_DIM, A_HEAD_DIM)),
                             z((N_EVEN, bp, B_CONV_W - 1, 2 * B_QK_WIDTH)),
                             z((N_EVEN, bp, B_HEADS, B_V_DIM, B_QK_DIM)),
                             z((N_EVEN, bp, B_HEADS, B_QK_DIM)),
                             z((N_EVEN, bp, B_HEADS)),
                             None, None, None, ew, ow)
    y_sample, ss = run_trunk(x_sample, state_shift, state_wkv, state_conv, state_c, state_n, state_m,
                             cache_k, cache_v, cache_idx_k, ew, ow)
    return (y_prompt, y_sample,
            sp['shift'], ss['shift'], sp['wkv'], ss['wkv'], sp['conv'], ss['conv'],
            sp['c'], ss['c'], sp['n'], ss['n'], sp['m'], ss['m'],
            sp['k'], ss['k'], sp['v'], ss['v'], sp['ki'], ss['ki'])
```

```python
import functools
import math

import jax
import jax.numpy as jnp
from jax import lax
from jax.experimental import pallas as pl
from jax.experimental.pallas import tpu as pltpu

F32 = jnp.float32
BF16 = jnp.bfloat16
I32 = jnp.int32

D_MODEL = 1024
DEPTH = 4
CHUNK = 64
A_HEADS, A_HEAD_DIM, A_WIDTH = 16, 64, 1024
A_LORA = 256
A_COLS = 3 * A_WIDTH + A_LORA
B_HEADS, B_QK_DIM, B_V_DIM = 8, 64, 128
B_QK_WIDTH, B_V_WIDTH = 512, 1024
C_HEADS, C_KV_HEADS, C_HEAD_DIM, C_GROUP = 8, 2, 128, 4
C_WIDTH = 1024
IDX_HEADS, IDX_DIM = 8, 64
TOPK_MAX = 256
N_EXPERTS = 8
D_FF_EXPERT = 1408
ALPHA = (2 * DEPTH) ** 0.25
LN_EPS = 1e-5

LANES = 128
SUBLANES = 8
VMEM_LIMIT = 56 * 1024 * 1024

EV_R, EV_K, EV_V, EV_QK, EV_VB, EV_O, EV_LORA, EV_GATE = 0, 1024, 2048, 3072, 4096, 5120, 6144, 6400
EV_COLS_PAD = 6528
OD_Q, OD_K, OD_V, OD_QI, OD_KI = 0, 1024, 1280, 1536, 2048
OD_COLS_PAD = 2176

INT_MIN = -2 ** 31


def _cparams(*sem):
    return pltpu.CompilerParams(dimension_semantics=sem, vmem_limit_bytes=VMEM_LIMIT)


def _layer_norm(z, g, b):
    mu = jnp.mean(z, axis=-1, keepdims=True)
    d = z - mu
    var = jnp.mean(d * d, axis=-1, keepdims=True)
    return d * lax.rsqrt(var + LN_EPS) * g + b


def _segsum(x, e):
    hi = x.astype(BF16)
    lo = (x - hi.astype(F32)).astype(BF16)
    return jnp.dot(hi, e, preferred_element_type=F32) + jnp.dot(lo, e, preferred_element_type=F32)


def _mm_kernel(x_ref, w_ref, o_ref):
    o_ref[...] = jnp.dot(x_ref[...].astype(BF16), w_ref[...], preferred_element_type=F32)


def _matmul(x, w, tm, tn):
    m, k = x.shape
    n = w.shape[1]
    return pl.pallas_call(
        _mm_kernel,
        grid=(m // tm, n // tn),
        in_specs=[pl.BlockSpec((tm, k), lambda i, j: (i, 0)),
                  pl.BlockSpec((k, tn), lambda i, j: (0, j))],
        out_specs=pl.BlockSpec((tm, tn), lambda i, j: (i, j)),
        out_shape=jax.ShapeDtypeStruct((m, n), F32),
        compiler_params=_cparams("parallel", "arbitrary"),
        name="proj_in",
    )(x, w)


def _rwkv_pre_kernel(ur, uk, uv, ul, s0m, s0l, mum, mul_, w0, a0, kk_, ka_, rk_, w2, a2, g2, e_ref,
                     o_r, o_w, o_k, o_v, o_a, o_b, o_g, o_bonus, prev_m, prev_l):
    @pl.when(pl.program_id(1) == 0)
    def _():
        prev_m[...] = s0m[0]
        prev_l[...] = s0l[0]

    tm = ur.shape[1]
    row = lax.broadcasted_iota(I32, (tm, 1), 0)

    def lerp(x, prev, mu):
        shifted = jnp.where(row == 0, prev, pltpu.roll(x, 1, 0))
        return x + (shifted - x) * mu

    r_raw, k_raw, v_raw, l_raw = ur[0], uk[0], uv[0], ul[0]
    r = lerp(r_raw, prev_m[:, 0:1024], mum[:, 0:1024])
    k = lerp(k_raw, prev_m[:, 1024:2048], mum[:, 1024:2048])
    v = lerp(v_raw, prev_m[:, 2048:3072], mum[:, 2048:3072])
    lo = lerp(l_raw, prev_l[...], mul_[...])
    prev_m[:, 0:1024] = r_raw[tm - 1:tm, :]
    prev_m[:, 1024:2048] = k_raw[tm - 1:tm, :]
    prev_m[:, 2048:3072] = v_raw[tm - 1:tm, :]
    prev_l[...] = l_raw[tm - 1:tm, :]

    zw = w0[...] + jnp.dot(jnp.tanh(lo).astype(BF16), w2[...], preferred_element_type=F32)
    decay = jnp.exp(-(math.exp(-0.5) * jax.nn.sigmoid(zw)))
    a = jax.nn.sigmoid(a0[...] + jnp.dot(lo.astype(BF16), a2[...], preferred_element_type=F32))
    g = jnp.dot(jax.nn.sigmoid(lo).astype(BF16), g2[...], preferred_element_type=F32)

    e = e_ref[...]
    kk = k * kk_[...]
    kk = kk * lax.rsqrt(jnp.maximum(_segsum(kk * kk, e), 1e-24))
    k2 = k * (1.0 + (a - 1.0) * ka_[...])
    bonus = _segsum(r * k2 * rk_[...], e) * v

    o_r[0] = r
    o_w[0] = decay
    o_k[0] = k2
    o_v[0] = v
    o_a[0] = -kk
    o_b[0] = kk * a
    o_g[0] = g
    o_bonus[0] = bonus


def _rwkv_pre(u, s0_main, s0_lora, p, tm):
    b, t, _ = u.shape
    blk = lambda w, c: pl.BlockSpec((1, tm, w), lambda i, j, c=c: (i, j, c))
    full = lambda arr: pl.BlockSpec(arr.shape, lambda i, j: (0,) * arr.ndim)
    params = [p['mu_main'], p['mu_lora'], p['w0'], p['a0'], p['k_k'], p['k_a'], p['r_k'],
              p['w2p'], p['a2p'], p['g2p'], p['seg64']]
    out = jax.ShapeDtypeStruct((b, t, A_WIDTH), F32)
    return pl.pallas_call(
        _rwkv_pre_kernel,
        grid=(b, t // tm),
        in_specs=[blk(1024, EV_R // 1024), blk(1024, EV_K // 1024), blk(1024, EV_V // 1024),
                  blk(A_LORA, EV_LORA // A_LORA),
                  pl.BlockSpec((1, 1, 3072), lambda i, j: (i, 0, 0)),
                  pl.BlockSpec((1, 1, A_LORA), lambda i, j: (i, 0, 0))] + [full(a) for a in params],
        out_specs=[pl.BlockSpec((1, tm, A_WIDTH), lambda i, j: (i, j, 0))] * 8,
        out_shape=[out] * 8,
        scratch_shapes=[pltpu.VMEM((1, 3072), F32), pltpu.VMEM((1, A_LORA), F32)],
        compiler_params=_cparams("parallel", "arbitrary"),
        name="rwkv_pre",
    )(u, u, u, u, s0_main, s0_lora, *params)


WKV_VPAIR = 2 * SUBLANES


def _wkv_kernel(r_ref, w_ref, k_ref, v_ref, a_ref, b_ref, s0_ref, y_ref, s_ref):
    @pl.when(pl.program_id(1) == 0)
    def _():
        s_ref[...] = s0_ref[...]

    n_t = r_ref.shape[0]
    dk = s_ref.shape[0]
    dv = s_ref.shape[1]

    def step(t, carry):
        sa = []
        for vp in range(dv // WKV_VPAIR):
            lo, mid, hi = vp * WKV_VPAIR, vp * WKV_VPAIR + SUBLANES, (vp + 1) * WKV_VPAIR
            acc0 = jnp.zeros((SUBLANES, LANES), F32)
            acc1 = jnp.zeros((SUBLANES, LANES), F32)
            for kx in range(dk):
                ab = a_ref[t, kx:kx + 1, :]
                acc0 = acc0 + s_ref[kx, lo:mid, :] * ab
                acc1 = acc1 + s_ref[kx, mid:hi, :] * ab
            sa.append((acc0, acc1))
        for vp in range(dv // WKV_VPAIR):
            lo, mid, hi = vp * WKV_VPAIR, vp * WKV_VPAIR + SUBLANES, (vp + 1) * WKV_VPAIR
            sa0, sa1 = sa[vp]
            v0 = v_ref[t, lo:mid, :]
            v1 = v_ref[t, mid:hi, :]
            y0 = jnp.zeros((SUBLANES, LANES), F32)
            y1 = jnp.zeros((SUBLANES, LANES), F32)
            for kx in range(dk):
                wb = w_ref[t, kx:kx + 1, :]
                bb = b_ref[t, kx:kx + 1, :]
                kb = k_ref[t, kx:kx + 1, :]
                rb = r_ref[t, kx:kx + 1, :]
                n0 = s_ref[kx, lo:mid, :] * wb + sa0 * bb + v0 * kb
                n1 = s_ref[kx, mid:hi, :] * wb + sa1 * bb + v1 * kb
                s_ref[kx, lo:mid, :] = n0
                s_ref[kx, mid:hi, :] = n1
                y0 = y0 + n0 * rb
                y1 = y1 + n1 * rb
            y_ref[t, lo:mid, :] = y0
            y_ref[t, mid:hi, :] = y1
        return carry

    lax.fori_loop(0, n_t, step, 0)


def _wkv_scan(r, w, k, v, a, b, s0, tt):
    t, d, c = r.shape
    seq = pl.BlockSpec((tt, d, LANES), lambda g, i: (i, 0, g))
    st = pl.BlockSpec((d, d, LANES), lambda g, i: (0, 0, g))
    return pl.pallas_call(
        _wkv_kernel,
        grid=(c // LANES, t // tt),
        in_specs=[seq] * 6 + [st],
        out_specs=[seq, st],
        out_shape=[jax.ShapeDtypeStruct((t, d, c), F32), jax.ShapeDtypeStruct((d, d, c), F32)],
        compiler_params=_cparams("parallel", "arbitrary"),
        name="wkv_scan",
    )(r, w, k, v, a, b, s0)


def _mlstm_kernel(uqk, uv, uo, ug, conv0, ct0, m0, cw, cb, gb, lng, lnb,
                  o_y, o_ct, o_m, ct, m_scr, carry, *, t_valid):
    tb = pl.program_id(1)

    @pl.when(tb == 0)
    def _():
        ct[...] = ct0[0]
        m_scr[...] = m0[0]
        carry[...] = conv0[0]

    L = uqk.shape[1]
    row = lax.broadcasted_iota(I32, (L, 1), 0)
    x = uqk[0]

    def shifted(j):
        out = pltpu.roll(x, j, 0)
        for i in range(j):
            out = jnp.where(row == i, carry[3 + i - j:4 + i - j, :], out)
        return out

    conv = x * cw[3:4, :] + shifted(1) * cw[2:3, :] + shifted(2) * cw[1:2, :] + shifted(3) * cw[0:1, :] + cb[...]
    carry[...] = x[L - 3:L, :]
    qk = conv * jax.nn.sigmoid(conv)
    q_all = qk[:, 0:B_QK_WIDTH]
    k_t = jnp.transpose(qk[:, B_QK_WIDTH:2 * B_QK_WIDTH] * (B_QK_DIM ** -0.5))

    lane = lax.broadcasted_iota(I32, (L, LANES), 1)
    gz = ug[0] + gb[...]
    lg = jnp.where(lane < B_HEADS, gz, jnp.minimum(gz, 0.0) - jnp.log(1.0 + jnp.exp(-jnp.abs(gz))))
    valid = (row + tb * L) < t_valid
    lg = jnp.where(valid, lg, jnp.where(lane < B_HEADS, -jnp.inf, 0.0))
    lg_t = jnp.transpose(lg)
    ti = lax.broadcasted_iota(I32, (L, L), 0)
    si = lax.broadcasted_iota(I32, (L, L), 1)
    tril = ti >= si
    tril_f = tril.astype(F32)
    triu_f = (ti <= si).astype(F32)
    lg_fin = jnp.where(lane < B_HEADS, 0.0, lg)
    bcum = jnp.dot(tril_f, lg_fin, preferred_element_type=F32, precision=lax.Precision.HIGHEST)
    bcum_t = jnp.dot(jnp.transpose(lg_fin), triu_f, preferred_element_type=F32,
                     precision=lax.Precision.HIGHEST)

    head_q = lax.shift_right_logical(lax.broadcasted_iota(I32, (L, B_QK_WIDTH), 1), 6)
    lane1 = lax.broadcasted_iota(I32, (1, LANES), 1)
    ones_col = (lax.broadcasted_iota(I32, (L, LANES), 1) == 0).astype(F32)
    m_vec = m_scr[...]
    m_new_vec = m_vec
    vv = uv[0]
    oo = uo[0]
    for h in range(B_HEADS):
        hs = slice(h * B_QK_DIM, (h + 1) * B_QK_DIM)
        vs = slice(h * B_V_DIM, (h + 1) * B_V_DIM)
        bc_col = bcum[:, B_HEADS + h:B_HEADS + h + 1]
        bc_row = bcum_t[B_HEADS + h:B_HEADS + h + 1, :]
        ic_row = lg_t[h:h + 1, :]
        gtot = bc_row[:, L - 1:L]
        m_prev = m_vec[:, h:h + 1]
        dmat = jnp.where(tril, bc_col - bc_row + ic_row, -jnp.inf)
        inter = bc_col + m_prev
        m_t = jnp.maximum(inter, jnp.max(dmat, axis=1, keepdims=True))
        w_intra = jnp.exp(dmat - m_t)
        w_inter = jnp.exp(inter - m_t)
        q_m = jnp.where(head_q == h, q_all, 0.0).astype(BF16)
        s = jnp.dot(q_m, k_t.astype(BF16), preferred_element_type=F32) * w_intra
        v_ext = jnp.concatenate([vv[:, vs], ones_col], axis=1).astype(BF16)
        intra = jnp.dot(s.astype(BF16), v_ext, preferred_element_type=F32)
        cross = jnp.dot(q_m, ct[...].astype(BF16), preferred_element_type=F32)
        num = intra[:, 0:B_V_DIM] + w_inter * cross[:, 0:B_V_DIM]
        den = jnp.sum(s, axis=1, keepdims=True) + w_inter * cross[:, B_V_DIM:B_V_DIM + 1]
        hh = num / jnp.maximum(jnp.abs(den), jnp.exp(-m_t))
        mu = jnp.mean(hh, axis=-1, keepdims=True)
        dd = hh - mu
        var = jnp.mean(dd * dd, axis=-1, keepdims=True)
        yn = dd * lax.rsqrt(var + LN_EPS) * lng[:, vs] + lnb[:, vs]
        o_y[0, :, vs] = jax.nn.sigmoid(oo[:, vs]) * yn
        lw = gtot - bc_row + ic_row
        m_new = jnp.maximum(gtot + m_prev, jnp.max(lw, axis=1, keepdims=True))
        w_s = jnp.exp(lw - m_new)
        dec = jnp.exp(gtot + m_prev - m_new)
        upd = jnp.dot((k_t[hs, :] * w_s).astype(BF16), v_ext, preferred_element_type=F32)
        ct[hs, :] = dec * ct[hs, :] + upd
        m_new_vec = jnp.where(lane1 == h, m_new, m_new_vec)
    m_scr[...] = m_new_vec

    @pl.when(tb == pl.num_programs(1) - 1)
    def _():
        o_ct[0] = ct[...]
        o_m[0] = m_scr[...]


def _mlstm(u, conv0, ct0, m0, p, L, t_valid):
    b, t, _ = u.shape
    blk = lambda w, c: pl.BlockSpec((1, L, w), lambda i, j, c=c: (i, j, c))
    full = lambda arr: pl.BlockSpec(arr.shape, lambda i, j: (0,) * arr.ndim)
    per_b = lambda arr: pl.BlockSpec((1,) + arr.shape[1:], lambda i, j: (i, 0, 0))
    params = [p['conv_w'], p['conv_b'], p['gate_b'], p['b_ln_g'], p['b_ln_b']]
    return pl.pallas_call(
        functools.partial(_mlstm_kernel, t_valid=t_valid),
        grid=(b, t // L),
        in_specs=[blk(1024, EV_QK // 1024), blk(1024, EV_VB // 1024), blk(1024, EV_O // 1024),
                  blk(LANES, EV_GATE // LANES), per_b(conv0), per_b(ct0), per_b(m0)] + [full(a) for a in params],
        out_specs=[pl.BlockSpec((1, L, B_V_WIDTH), lambda i, j: (i, j, 0)),
                   pl.BlockSpec((1, B_QK_WIDTH, 2 * LANES), lambda i, j: (i, 0, 0)),
                   pl.BlockSpec((1, 1, LANES), lambda i, j: (i, 0, 0))],
        out_shape=[jax.ShapeDtypeStruct((b, t, B_V_WIDTH), F32),
                   jax.ShapeDtypeStruct((b, B_QK_WIDTH, 2 * LANES), F32),
                   jax.ShapeDtypeStruct((b, 1, LANES), F32)],
        scratch_shapes=[pltpu.VMEM((B_QK_WIDTH, 2 * LANES), F32), pltpu.VMEM((1, LANES), F32),
                        pltpu.VMEM((3, 2 * B_QK_WIDTH), F32)],
        compiler_params=_cparams("parallel", "arbitrary"),
        name="mlstm",
    )(u, u, u, u, conv0, ct0, m0, *params)


def _even_out_kernel(y_ref, bonus_ref, g_ref, yb_ref, x_ref, lng, lnb, e_ref, woa, wob, g1, b1, o_ref):
    e = e_ref[...]
    y = y_ref[...]
    mu = _segsum(y, e) * (1.0 / A_HEAD_DIM)
    d = y - mu
    var = _segsum(d * d, e) * (1.0 / A_HEAD_DIM)
    yn = d * lax.rsqrt(var + LN_EPS) * lng[...] + lnb[...]
    ya = (yn + bonus_ref[...]) * g_ref[...]
    mix = (jnp.dot(ya.astype(BF16), woa[...], preferred_element_type=F32)
           + jnp.dot(yb_ref[...].astype(BF16), wob[...], preferred_element_type=F32))
    o_ref[...] = _layer_norm(ALPHA * x_ref[...] + mix, g1[...], b1[...])


def _even_out(y, bonus, g, yb, x, p, tm):
    m = x.shape[0]
    row = pl.BlockSpec((tm, D_MODEL), lambda i: (i, 0))
    full = lambda arr: pl.BlockSpec(arr.shape, lambda i: (0,) * arr.ndim)
    params = [p['a_ln_g'], p['a_ln_b'], p['seg64'], p['w_out_a'], p['w_out_b'], p['ln1_g'], p['ln1_b']]
    return pl.pallas_call(
        _even_out_kernel,
        grid=(m // tm,),
        in_specs=[row] * 5 + [full(a) for a in params],
        out_specs=row,
        out_shape=jax.ShapeDtypeStruct((m, D_MODEL), F32),
        compiler_params=_cparams("parallel"),
        name="even_out",
    )(y, bonus, g, yb, x, *params)


def _proj_ln_kernel(a_ref, x_ref, w_ref, g1, b1, o_ref):
    mix = jnp.dot(a_ref[...].astype(BF16), w_ref[...], preferred_element_type=F32)
    o_ref[...] = _layer_norm(ALPHA * x_ref[...] + mix, g1[...], b1[...])


def _proj_ln(a, x, w, g1, b1, tm):
    m = x.shape[0]
    row = pl.BlockSpec((tm, D_MODEL), lambda i: (i, 0))
    full = lambda arr: pl.BlockSpec(arr.shape, lambda i: (0,) * arr.ndim)
    return pl.pallas_call(
        _proj_ln_kernel,
        grid=(m // tm,),
        in_specs=[row, row, full(w), full(g1), full(b1)],
        out_specs=row,
        out_shape=jax.ShapeDtypeStruct((m, D_MODEL), F32),
        compiler_params=_cparams("parallel"),
        name="proj_ln",
    )(a, x, w, g1, b1)


def _experts_kernel(x_ref, wr, br, wg, wu, wd, g2, b2, o_ref, xb, comb, acc, *, routed):
    e = pl.program_id(1)

    @pl.when(e == 0)
    def _():
        x = x_ref[...]
        xb[...] = x.astype(BF16)
        acc[...] = jnp.zeros_like(acc)
        if routed:
            logits = jnp.dot(x, wr[...], preferred_element_type=F32, precision=lax.Precision.HIGHEST) + br[...]
            lane = lax.broadcasted_iota(I32, logits.shape, 1).astype(F32)
            v1 = jnp.max(logits, axis=-1, keepdims=True)
            i1 = jnp.min(jnp.where(logits == v1, lane, float(LANES)), axis=-1, keepdims=True)
            rest = jnp.where(lane == i1, -jnp.inf, logits)
            v2 = jnp.max(rest, axis=-1, keepdims=True)
            i2 = jnp.min(jnp.where(rest == v2, lane, float(LANES)), axis=-1, keepdims=True)
            e2 = jnp.exp(v2 - v1)
            den = 1.0 + e2
            comb[...] = jnp.where(lane == i1, 1.0 / den, jnp.where(lane == i2, e2 / den, 0.0))

    xbv = xb[...]
    hg = jnp.dot(xbv, wg[0], preferred_element_type=F32)
    hu = jnp.dot(xbv, wu[0], preferred_element_type=F32)
    hid = (hg * jax.nn.sigmoid(hg)) * hu
    y = jnp.dot(hid.astype(BF16), wd[0], preferred_element_type=F32)
    if routed:
        lane = lax.broadcasted_iota(I32, comb.shape, 1)
        ce = jnp.sum(jnp.where(lane == e, comb[...], 0.0), axis=-1, keepdims=True)
        y = ce * y
    acc[...] += y

    @pl.when(e == pl.num_programs(1) - 1)
    def _():
        o_ref[...] = _layer_norm(ALPHA * x_ref[...] + acc[...], g2[...], b2[...])


def _experts(x, wr, br, wg, wu, wd, g2, b2, tm, routed):
    m = x.shape[0]
    ne, _, dff = wg.shape
    row = pl.BlockSpec((tm, D_MODEL), lambda i, e: (i, 0))
    full = lambda arr: pl.BlockSpec(arr.shape, lambda i, e: (0,) * arr.ndim)
    return pl.pallas_call(
        functools.partial(_experts_kernel, routed=routed),
        grid=(m // tm, ne),
        in_specs=[row, full(wr), full(br),
                  pl.BlockSpec((1, D_MODEL, dff), lambda i, e: (e, 0, 0)),
                  pl.BlockSpec((1, D_MODEL, dff), lambda i, e: (e, 0, 0)),
                  pl.BlockSpec((1, dff, D_MODEL), lambda i, e: (e, 0, 0)),
                  full(g2), full(b2)],
        out_specs=row,
        out_shape=jax.ShapeDtypeStruct((m, D_MODEL), F32),
        scratch_shapes=[pltpu.VMEM((tm, D_MODEL), BF16), pltpu.VMEM((tm, LANES), F32),
                        pltpu.VMEM((tm, D_MODEL), F32)],
        compiler_params=_cparams("parallel", "arbitrary"),
        name="experts_routed" if routed else "experts_dense",
    )(x, wr, br, wg, wu, wd, g2, b2)


def _dsa_kernel(q_ref, qi_ref, wi_ref, kt_ref, v_ref, kit_ref, o_ref, key_scr,
                *, top_k, causal, limit_const):
    tq = q_ref.shape[1]
    s_len = kt_ref.shape[2]
    c = pl.program_id(1)
    limit = (c + 1) * tq if causal else limit_const
    kth = jnp.minimum(top_k, limit).astype(F32)
    idx = lax.broadcasted_iota(I32, (tq, s_len), 1)

    kit = kit_ref[0]
    wi = wi_ref[0] * (IDX_HEADS ** -0.5)
    score = jnp.zeros((tq, s_len), F32)
    for h in range(IDX_HEADS):
        rel = jnp.dot(qi_ref[0, h], kit, preferred_element_type=F32)
        score = score + wi[:, IDX_DIM + h:IDX_DIM + h + 1] * jnp.maximum(rel, 0.0)
    score = jnp.where(score == 0.0, 0.0, score)
    bits = pltpu.bitcast(score, I32)
    key = jnp.where(bits < 0, bits ^ 0x7FFFFFFF, bits)
    key_scr[...] = jnp.where(idx < limit, key, INT_MIN)

    def bit_step(i, thr):
        cand = thr + lax.shift_left(jnp.int32(1), 31 - i)
        cnt = jnp.sum(jnp.where(key_scr[...] >= cand, 1.0, 0.0), axis=1, keepdims=True)
        return jnp.where(cnt >= kth, cand, thr)

    thr = lax.fori_loop(0, 32, bit_step, jnp.full((tq, 1), INT_MIN, I32))
    keyv = key_scr[...]
    need = kth - jnp.sum(jnp.where(keyv > thr, 1.0, 0.0), axis=1, keepdims=True)
    n_bits = max(1, (s_len - 1).bit_length())

    def idx_step(i, j):
        cand = j + lax.shift_left(jnp.int32(1), n_bits - 1 - i)
        kv = key_scr[...]
        cnt = jnp.sum(jnp.where(kv == thr, jnp.where(idx < cand, 1.0, 0.0), 0.0), axis=1, keepdims=True)
        return jnp.where(cnt < need, cand, j)

    jcut = lax.fori_loop(0, n_bits, idx_step, jnp.zeros((tq, 1), I32))
    bias = jnp.where(keyv > thr, 0.0,
                     jnp.where(keyv == thr, jnp.where(idx <= jcut, 0.0, -jnp.inf), -jnp.inf))

    q = q_ref[0]
    for grp in range(C_KV_HEADS):
        ks = slice(grp * C_HEAD_DIM, (grp + 1) * C_HEAD_DIM)
        kt = kt_ref[0, ks, :]
        vg = v_ref[0, :, ks]
        hsl = [slice((grp * C_GROUP + hg) * C_HEAD_DIM, (grp * C_GROUP + hg + 1) * C_HEAD_DIM)
               for hg in range(C_GROUP)]
        qg = jnp.concatenate([q[:, sl] for sl in hsl], axis=0).astype(BF16)
        lg = jnp.dot(qg, kt, preferred_element_type=F32) * (C_HEAD_DIM ** -0.5)
        ps, dens = [], []
        for hg in range(C_GROUP):
            logits = lg[hg * tq:(hg + 1) * tq, :] + bias
            p = jnp.exp(logits - jnp.max(logits, axis=1, keepdims=True))
            dens.append(jnp.sum(p, axis=1, keepdims=True))
            ps.append(p.astype(BF16))
        out = jnp.dot(jnp.concatenate(ps, axis=0), vg, preferred_element_type=F32)
        for hg in range(C_GROUP):
            o_ref[0, :, hsl[hg]] = out[hg * tq:(hg + 1) * tq, :] / dens[hg]


def _dsa(u, qi_heads, kt, v, kit, tq, top_k, causal, limit_const):
    b, t, _ = u.shape
    s_len = kt.shape[2]
    return pl.pallas_call(
        functools.partial(_dsa_kernel, top_k=top_k, causal=causal, limit_const=limit_const),
        grid=(b, t // tq),
        in_specs=[pl.BlockSpec((1, tq, C_WIDTH), lambda i, j: (i, j, OD_Q // C_WIDTH)),
                  pl.BlockSpec((1, IDX_HEADS, tq, LANES), lambda i, j: (i, 0, j, 0)),
                  pl.BlockSpec((1, tq, LANES), lambda i, j: (i, j, OD_KI // LANES)),
                  pl.BlockSpec((1, 2 * C_HEAD_DIM, s_len), lambda i, j: (i, 0, 0)),
                  pl.BlockSpec((1, s_len, 2 * C_HEAD_DIM), lambda i, j: (i, 0, 0)),
                  pl.BlockSpec((1, LANES, s_len), lambda i, j: (i, 0, 0))],
        out_specs=pl.BlockSpec((1, tq, C_WIDTH), lambda i, j: (i, j, 0)),
        out_shape=jax.ShapeDtypeStruct((b, t, C_WIDTH), F32),
        scratch_shapes=[pltpu.VMEM((tq, s_len), I32)],
        compiler_params=_cparams("parallel", "arbitrary"),
        name="dsa",
    )(u, qi_heads, u, kt, v, kit)


def _prep_even(w):
    win = w['w_in']
    a, bq = win[:, :A_COLS], win[:, A_COLS:]
    cols = [a[:, 0:3072], bq[:, 0:3072], a[:, 3072:A_COLS], bq[:, 3072:3088],
            jnp.zeros((D_MODEL, EV_COLS_PAD - EV_GATE - 2 * B_HEADS), F32)]
    p = {'w_in': jnp.concatenate(cols, axis=1).astype(BF16)}
    mu = w['mu']
    p['mu_main'] = mu[None, 0:3072]
    p['mu_lora'] = mu[None, 3072:A_COLS]
    row = lambda v: v.reshape(1, -1)
    p['w0'], p['a0'], p['k_k'], p['k_a'] = row(w['w0']), row(w['a0']), row(w['k_k']), row(w['k_a'])
    p['r_k'] = row(w['r_k'])
    z = lambda n: jnp.zeros((n, A_WIDTH), F32)
    p['w2p'] = jnp.concatenate([w['w2'], z(192)], axis=0).astype(BF16)
    p['a2p'] = jnp.concatenate([z(64), w['a2'], z(128)], axis=0).astype(BF16)
    p['g2p'] = jnp.concatenate([z(128), w['g2']], axis=0).astype(BF16)
    seg = jnp.arange(A_WIDTH) // A_HEAD_DIM
    p['seg64'] = (seg[:, None] == seg[None, :]).astype(BF16)
    p['a_ln_g'], p['a_ln_b'] = row(w['a_ln_g']), row(w['a_ln_b'])
    p['conv_w'], p['conv_b'] = w['conv_w'], row(w['conv_b'])
    p['gate_b'] = jnp.concatenate([w['b_i'], w['b_f'], jnp.zeros((LANES - 2 * B_HEADS,), F32)])[None]
    p['b_ln_g'], p['b_ln_b'] = row(w['b_ln_g']), row(w['b_ln_b'])
    p['w_out_a'] = w['w_out'][:A_WIDTH].astype(BF16)
    p['w_out_b'] = w['w_out'][A_WIDTH:].astype(BF16)
    p['ln1_g'], p['ln1_b'], p['ln2_g'], p['ln2_b'] = row(w['ln1_g']), row(w['ln1_b']), row(w['ln2_g']), row(w['ln2_b'])
    half = w['ffn_gate'].shape[1] // 2
    p['ffn_g'] = jnp.stack([w['ffn_gate'][:, :half], w['ffn_gate'][:, half:]]).astype(BF16)
    p['ffn_u'] = jnp.stack([w['ffn_up'][:, :half], w['ffn_up'][:, half:]]).astype(BF16)
    p['ffn_d'] = jnp.stack([w['ffn_down'][:half], w['ffn_down'][half:]]).astype(BF16)
    p['no_router_w'] = jnp.zeros((D_MODEL, LANES), F32)
    p['no_router_b'] = jnp.zeros((1, LANES), F32)
    return p


def _prep_odd(w):
    row = lambda v: v.reshape(1, -1)
    p = {'w_in': jnp.pad(w['w_in'], ((0, 0), (0, OD_COLS_PAD - w['w_in'].shape[1]))).astype(BF16)}
    p['w_out'] = w['w_out'].astype(BF16)
    p['ln1_g'], p['ln1_b'], p['ln2_g'], p['ln2_b'] = row(w['ln1_g']), row(w['ln1_b']), row(w['ln2_g']), row(w['ln2_b'])
    p['router_w'] = jnp.pad(w['router'], ((0, 0), (0, LANES - N_EXPERTS)))
    p['router_b'] = jnp.concatenate([w['router_b'], jnp.full((LANES - N_EXPERTS,), -1e30, F32)])[None]
    p['e_gate'], p['e_up'], p['e_down'] = w['e_gate'].astype(BF16), w['e_up'].astype(BF16), w['e_down'].astype(BF16)
    return p


def _pick(n, pref):
    for c in pref:
        if n % c == 0:
            return c
    return n


def _even_layer(x, shift0, wkv0, conv0, c0, n0, m0, p):
    b, t, _ = x.shape
    m = b * t
    tm = _pick(m, (512, 256, 128, 64, 32, 16, 8))
    u = _matmul(x.reshape(m, D_MODEL), p['w_in'], tm, EV_COLS_PAD // 3).reshape(b, t, EV_COLS_PAD)

    new_shift = jnp.concatenate([u[:, t - 1:, 0:3072], u[:, t - 1:, EV_LORA:EV_LORA + A_LORA]], axis=-1)
    new_conv = jnp.concatenate([conv0, u[:, :, EV_QK:EV_QK + 1024]], axis=1)[:, t:]

    s0_main, s0_lora = shift0[:, :, 0:3072], shift0[:, :, 3072:A_COLS]
    tp = _pick(t, (256, 128, 64, 32, 16, 8))
    r, w, k, v, a, bb, g, bonus = _rwkv_pre(u, s0_main, s0_lora, p, tp)
    chains = b * A_HEADS
    cpad = -chains % LANES

    def to_scan(z):
        z = z.reshape(b, t, A_HEADS, A_HEAD_DIM).transpose(1, 3, 0, 2).reshape(t, A_HEAD_DIM, chains)
        return jnp.pad(z, ((0, 0), (0, 0), (0, cpad))) if cpad else z

    s0 = wkv0.transpose(3, 2, 0, 1).reshape(A_HEAD_DIM, A_HEAD_DIM, chains)
    if cpad:
        s0 = jnp.pad(s0, ((0, 0), (0, 0), (0, cpad)))
    tt = _pick(t, (16, 8))
    y, s_fin = _wkv_scan(*(to_scan(z) for z in (r, w, k, v, a, bb)), s0, tt)
    y = y[:, :, :chains].reshape(t, A_HEAD_DIM, b, A_HEADS).transpose(2, 0, 3, 1).reshape(m, A_WIDTH)
    new_wkv = s_fin[:, :, :chains].reshape(A_HEAD_DIM, A_HEAD_DIM, b, A_HEADS).transpose(2, 3, 1, 0)

    ct0 = jnp.concatenate([c0.transpose(0, 1, 3, 2).reshape(b, B_QK_WIDTH, B_V_DIM),
                           n0.reshape(b, B_QK_WIDTH, 1),
                           jnp.zeros((b, B_QK_WIDTH, LANES - 1), F32)], axis=-1)
    m0p = jnp.pad(m0, ((0, 0), (0, LANES - B_HEADS)))[:, None, :]
    lm = _pick(t, (256, 128))
    if t % LANES:
        t_pad = -t % LANES
        u_b = jnp.pad(u, ((0, 0), (0, t_pad), (0, 0)))
        lm = LANES
    else:
        u_b = u
    yb, ct, m_out = _mlstm(u_b, conv0, ct0, m0p, p, lm, t)
    yb = yb[:, :t].reshape(m, B_V_WIDTH)
    new_c = ct[:, :, 0:B_V_DIM].reshape(b, B_HEADS, B_QK_DIM, B_V_DIM).transpose(0, 1, 3, 2)
    new_n = ct[:, :, B_V_DIM].reshape(b, B_HEADS, B_QK_DIM)
    new_m = m_out[:, 0, 0:B_HEADS]

    x2 = x.reshape(m, D_MODEL)
    x2 = _even_out(y, bonus.reshape(m, A_WIDTH), g.reshape(m, A_WIDTH), yb, x2, p, _pick(m, (256, 128, 64, 32, 16, 8)))
    x2 = _experts(x2, p['no_router_w'], p['no_router_b'], p['ffn_g'], p['ffn_u'], p['ffn_d'],
                  p['ln2_g'], p['ln2_b'], tm, routed=False)
    return x2.reshape(b, t, D_MODEL), (new_shift, new_wkv, new_conv, new_c, new_n, new_m)


def _odd_layer(x, past_k, past_v, past_ki, p):
    b, t, _ = x.shape
    m = b * t
    tm = _pick(m, (512, 256, 128, 64, 32, 16, 8))
    u = _matmul(x.reshape(m, D_MODEL), p['w_in'], tm, OD_COLS_PAD).reshape(b, t, OD_COLS_PAD)
    k_new = u[:, :, OD_K:OD_K + 256]
    v_new = u[:, :, OD_V:OD_V + 256]
    ki_new = u[:, :, OD_KI:OD_KI + IDX_DIM]
    if past_k is None:
        keys_k, keys_v, keys_i = k_new, v_new, ki_new
        causal, limit, tq = True, 0, CHUNK
        top_k = min(TOPK_MAX, t // 4)
    else:
        keys_k = jnp.concatenate([past_k.reshape(b, -1, 256), k_new], axis=1)
        keys_v = jnp.concatenate([past_v.reshape(b, -1, 256), v_new], axis=1)
        keys_i = jnp.concatenate([past_ki, ki_new], axis=1)
        limit = keys_k.shape[1]
        causal, tq = False, t
        top_k = min(TOPK_MAX, limit // 4)
    s_pad = -keys_k.shape[1] % LANES
    pad_s = lambda z: jnp.pad(z, ((0, 0), (0, s_pad), (0, 0))) if s_pad else z
    kt = pad_s(keys_k).transpose(0, 2, 1).astype(BF16)
    vv = pad_s(keys_v).astype(BF16)
    kit = jnp.pad(pad_s(keys_i), ((0, 0), (0, 0), (0, LANES - IDX_DIM))).transpose(0, 2, 1).astype(BF16)
    qi = u[:, :, OD_QI:OD_QI + IDX_HEADS * IDX_DIM].reshape(b, t, IDX_HEADS, IDX_DIM).transpose(0, 2, 1, 3)
    qi = jnp.pad(qi, ((0, 0), (0, 0), (0, 0), (0, LANES - IDX_DIM))).astype(BF16)
    att = _dsa(u, qi, kt, vv, kit, tq, top_k, causal, limit)
    x2 = _proj_ln(att.reshape(m, C_WIDTH), x.reshape(m, D_MODEL), p['w_out'], p['ln1_g'], p['ln1_b'],
                  _pick(m, (512, 256, 128, 64, 32, 16, 8)))
    x2 = _experts(x2, p['router_w'], p['router_b'], p['e_gate'], p['e_up'], p['e_down'],
                  p['ln2_g'], p['ln2_b'], tm, routed=True)
    st = (k_new.reshape(b, t, C_KV_HEADS, C_HEAD_DIM), v_new.reshape(b, t, C_KV_HEADS, C_HEAD_DIM), ki_new)
    return x2.reshape(b, t, D_MODEL), st


def _trunk(x, shift0, wkv0, conv0, c0, n0, m0, past_k, past_v, past_ki, ev, od):
    even_out = [[] for _ in range(6)]
    odd_out = [[] for _ in range(3)]
    for layer in range(DEPTH):
        i = layer // 2
        if layer % 2 == 0:
            x, st = _even_layer(x, shift0[i], wkv0[i], conv0[i], c0[i], n0[i], m0[i], ev[i])
            for acc, val in zip(even_out, st):
                acc.append(val)
        else:
            if past_k is None:
                x, st = _odd_layer(x, None, None, None, od[i])
            else:
                x, st = _odd_layer(x, past_k[i], past_v[i], past_ki[i], od[i])
            for acc, val in zip(odd_out, st):
                acc.append(val)
    return x, [jnp.stack(v) for v in even_out], [jnp.stack(v) for v in odd_out]


def kernel(x_prompt, x_sample, state_shift, state_wkv, state_conv, state_c, state_n, state_m,
           cache_k, cache_v, cache_idx_k,
           ev_w_in, a_mu, a_w0, a_w2, a_a0, a_a2, a_g2, a_k_k, a_k_a, a_r_k, a_ln_g, a_ln_b,
           b_conv_w, b_conv_b, b_i_bias, b_f_bias, b_ln_g, b_ln_b, ev_w_out, ev_ln1_g, ev_ln1_b,
           ffn_w_gate, ffn_w_up, ffn_w_down, ev_ln2_g, ev_ln2_b,
           od_w_in, od_w_out, od_ln1_g, od_ln1_b, moe_w_router, moe_b_router,
           moe_w_gate, moe_w_up, moe_w_down, od_ln2_g, od_ln2_b):
    ew = {'w_in': ev_w_in, 'mu': a_mu, 'w0': a_w0, 'w2': a_w2, 'a0': a_a0, 'a2': a_a2, 'g2': a_g2,
          'k_k': a_k_k, 'k_a': a_k_a, 'r_k': a_r_k, 'a_ln_g': a_ln_g, 'a_ln_b': a_ln_b,
          'conv_w': b_conv_w, 'conv_b': b_conv_b, 'b_i': b_i_bias, 'b_f': b_f_bias,
          'b_ln_g': b_ln_g, 'b_ln_b': b_ln_b, 'w_out': ev_w_out, 'ln1_g': ev_ln1_g, 'ln1_b': ev_ln1_b,
          'ffn_gate': ffn_w_gate, 'ffn_up': ffn_w_up, 'ffn_down': ffn_w_down,
          'ln2_g': ev_ln2_g, 'ln2_b': ev_ln2_b}
    ow = {'w_in': od_w_in, 'w_out': od_w_out, 'ln1_g': od_ln1_g, 'ln1_b': od_ln1_b,
          'router': moe_w_router, 'router_b': moe_b_router, 'e_gate': moe_w_gate, 'e_up': moe_w_up,
          'e_down': moe_w_down, 'ln2_g': od_ln2_g, 'ln2_b': od_ln2_b}
    n_even, n_odd = ev_w_in.shape[0], od_w_in.shape[0]
    ev = [_prep_even({k: v[i] for k, v in ew.items()}) for i in range(n_even)]
    od = [_prep_odd({k: v[i] for k, v in ow.items()}) for i in range(n_odd)]

    bp = x_prompt.shape[0]
    z = functools.partial(jnp.zeros, dtype=F32)
    y_p, ep, op = _trunk(x_prompt,
                         z((n_even, bp, 1, A_COLS)), z((n_even, bp, A_HEADS, A_HEAD_DIM, A_HEAD_DIM)),
                         z((n_even, bp, 3, 2 * B_QK_WIDTH)), z((n_even, bp, B_HEADS, B_V_DIM, B_QK_DIM)),
                         z((n_even, bp, B_HEADS, B_QK_DIM)), z((n_even, bp, B_HEADS)),
                         None, None, None, ev, od)
    y_s, es, os_ = _trunk(x_sample, state_shift, state_wkv, state_conv, state_c, state_n, state_m,
                          cache_k, cache_v, cache_idx_k, ev, od)
    return (y_p, y_s,
            ep[0], es[0], ep[1], es[1], ep[2], es[2], ep[3], es[3], ep[4], es[4], ep[5], es[5],
            op[0], os_[0], op[1], os_[1], op[2], os_[2])
```

```python
import functools
import math

import jax
import jax.numpy as jnp
from jax import lax
from jax.experimental import pallas as pl
from jax.experimental.pallas import tpu as pltpu

F32 = jnp.float32
BF16 = jnp.bfloat16
I32 = jnp.int32

D_MODEL = 1024
DEPTH = 4
CHUNK = 64
A_HEADS, A_HEAD_DIM, A_WIDTH = 16, 64, 1024
A_LORA = 256
A_COLS = 3 * A_WIDTH + A_LORA
B_HEADS, B_QK_DIM, B_V_DIM = 8, 64, 128
B_QK_WIDTH, B_V_WIDTH = 512, 1024
C_HEADS, C_KV_HEADS, C_HEAD_DIM, C_GROUP = 8, 2, 128, 4
C_WIDTH = 1024
IDX_HEADS, IDX_DIM = 8, 64
TOPK_MAX = 256
N_EXPERTS = 8
D_FF_EXPERT = 1408
ALPHA = (2 * DEPTH) ** 0.25
LN_EPS = 1e-5

LANES = 128
SUBLANES = 8
VMEM_LIMIT = 56 * 1024 * 1024

EV_R, EV_K, EV_V, EV_QK, EV_VB, EV_O, EV_LORA, EV_GATE = 0, 1024, 2048, 3072, 4096, 5120, 6144, 6400
EV_COLS_PAD = 6528
OD_Q, OD_K, OD_V, OD_QI, OD_KI = 0, 1024, 1280, 1536, 2048
OD_COLS_PAD = 2176

INT_MIN = -2 ** 31


def _cparams(*sem):
    return pltpu.CompilerParams(dimension_semantics=sem, vmem_limit_bytes=VMEM_LIMIT)


def _layer_norm(z, g, b):
    mu = jnp.mean(z, axis=-1, keepdims=True)
    d = z - mu
    var = jnp.mean(d * d, axis=-1, keepdims=True)
    return d * lax.rsqrt(var + LN_EPS) * g + b


def _segsum(x, e):
    hi = x.astype(BF16)
    lo = (x - hi.astype(F32)).astype(BF16)
    return jnp.dot(hi, e, preferred_element_type=F32) + jnp.dot(lo, e, preferred_element_type=F32)


def _mm_kernel(x_ref, w_ref, o_ref):
    o_ref[...] = jnp.dot(x_ref[...].astype(BF16), w_ref[...], preferred_element_type=F32)


def _matmul(x, w, tm, tn):
    m, k = x.shape
    n = w.shape[1]
    return pl.pallas_call(
        _mm_kernel,
        grid=(m // tm, n // tn),
        in_specs=[pl.BlockSpec((tm, k), lambda i, j: (i, 0)),
                  pl.BlockSpec((k, tn), lambda i, j: (0, j))],
        out_specs=pl.BlockSpec((tm, tn), lambda i, j: (i, j)),
        out_shape=jax.ShapeDtypeStruct((m, n), F32),
        compiler_params=_cparams("parallel", "arbitrary"),
        name="proj_in",
    )(x, w)


def _rwkv_pre_kernel(ur, uk, uv, ul, s0m, s0l, mum, mul_, w0, a0, kk_, ka_, rk_, w2, a2, g2, e_ref,
                     o_r, o_w, o_k, o_v, o_a, o_b, o_g, o_bonus, prev_m, prev_l):
    @pl.when(pl.program_id(1) == 0)
    def _():
        prev_m[...] = s0m[0]
        prev_l[...] = s0l[0]

    tm = ur.shape[1]
    row = lax.broadcasted_iota(I32, (tm, 1), 0)

    def lerp(x, prev, mu):
        shifted = jnp.where(row == 0, prev, pltpu.roll(x, 1, 0))
        return x + (shifted - x) * mu

    r_raw, k_raw, v_raw, l_raw = ur[0], uk[0], uv[0], ul[0]
    r = lerp(r_raw, prev_m[:, 0:1024], mum[:, 0:1024])
    k = lerp(k_raw, prev_m[:, 1024:2048], mum[:, 1024:2048])
    v = lerp(v_raw, prev_m[:, 2048:3072], mum[:, 2048:3072])
    lo = lerp(l_raw, prev_l[...], mul_[...])
    prev_m[:, 0:1024] = r_raw[tm - 1:tm, :]
    prev_m[:, 1024:2048] = k_raw[tm - 1:tm, :]
    prev_m[:, 2048:3072] = v_raw[tm - 1:tm, :]
    prev_l[...] = l_raw[tm - 1:tm, :]

    zw = w0[...] + jnp.dot(jnp.tanh(lo).astype(BF16), w2[...], preferred_element_type=F32)
    decay = jnp.exp(-(math.exp(-0.5) * jax.nn.sigmoid(zw)))
    a = jax.nn.sigmoid(a0[...] + jnp.dot(lo.astype(BF16), a2[...], preferred_element_type=F32))
    g = jnp.dot(jax.nn.sigmoid(lo).astype(BF16), g2[...], preferred_element_type=F32)

    e = e_ref[...]
    kk = k * kk_[...]
    kk = kk * lax.rsqrt(jnp.maximum(_segsum(kk * kk, e), 1e-24))
    k2 = k * (1.0 + (a - 1.0) * ka_[...])
    bonus = _segsum(r * k2 * rk_[...], e) * v

    o_r[...] = r
    o_w[...] = decay
    o_k[...] = k2
    o_v[...] = v
    o_a[...] = -kk
    o_b[...] = kk * a
    o_g[...] = g
    o_bonus[...] = bonus


def _rwkv_pre(u, s0_main, s0_lora, p, tm):
    b, t, _ = u.shape
    blk = lambda w, c: pl.BlockSpec((1, tm, w), lambda i, j, c=c: (i, j, c))
    full = lambda arr: pl.BlockSpec(arr.shape, lambda i, j: (0,) * arr.ndim)
    params = [p['mu_main'], p['mu_lora'], p['w0'], p['a0'], p['k_k'], p['k_a'], p['r_k'],
              p['w2p'], p['a2p'], p['g2p'], p['seg64']]
    out = jax.ShapeDtypeStruct((t, b * A_WIDTH), F32)
    return pl.pallas_call(
        _rwkv_pre_kernel,
        grid=(b, t // tm),
        in_specs=[blk(1024, EV_R // 1024), blk(1024, EV_K // 1024), blk(1024, EV_V // 1024),
                  blk(A_LORA, EV_LORA // A_LORA),
                  pl.BlockSpec((1, 1, 3072), lambda i, j: (i, 0, 0)),
                  pl.BlockSpec((1, 1, A_LORA), lambda i, j: (i, 0, 0))] + [full(a) for a in params],
        out_specs=[pl.BlockSpec((tm, A_WIDTH), lambda i, j: (j, i))] * 8,
        out_shape=[out] * 8,
        scratch_shapes=[pltpu.VMEM((1, 3072), F32), pltpu.VMEM((1, A_LORA), F32)],
        compiler_params=_cparams("parallel", "arbitrary"),
        name="rwkv_pre",
    )(u, u, u, u, s0_main, s0_lora, *params)


WKV_SRC_R, WKV_SRC_W, WKV_SRC_K, WKV_SRC_V, WKV_SRC_A, WKV_SRC_B = range(6)
WKV_VGROUP = 4
WKV_KCHUNK = 16


def _wkv_kernel(r_ref, w_ref, k_ref, v_ref, a_ref, b_ref, s0_ref, y_ref, s_ref, tr, sa_buf, ybuf):
    @pl.when(pl.program_id(1) == 0)
    def _():
        s_ref[...] = s0_ref[...]

    n_t = r_ref.shape[0]
    d = A_HEAD_DIM
    srcs = (r_ref, w_ref, k_ref, v_ref, a_ref, b_ref)
    vspan = WKV_VGROUP * SUBLANES
    groups = [(p, g) for p in range(2) for g in range(d // vspan)]

    def vrows(g, j):
        return slice(g * vspan + j * SUBLANES, g * vspan + (j + 1) * SUBLANES)

    def transpose_in(t, carry):
        for i, src in enumerate(srcs):
            tr[i, t] = jnp.transpose(src[t])
        return carry

    lax.fori_loop(0, n_t, transpose_in, 0)

    for p, g in groups:
        acc = [jnp.zeros((SUBLANES, LANES), F32) for _ in range(WKV_VGROUP)]
        for kx in range(d):
            ab = tr[WKV_SRC_A, 0, p * d + kx:p * d + kx + 1, :]
            for j in range(WKV_VGROUP):
                acc[j] = acc[j] + s_ref[p, kx, vrows(g, j), :] * ab
        for j in range(WKV_VGROUP):
            sa_buf[p, vrows(g, j), :] = acc[j]

    def step(t, carry):
        tn = jnp.minimum(t + 1, n_t - 1)
        for p, g in groups:
            vv = [tr[WKV_SRC_V, t, p * d + g * vspan + j * SUBLANES:p * d + g * vspan + (j + 1) * SUBLANES, :]
                  for j in range(WKV_VGROUP)]
            sa = [sa_buf[p, vrows(g, j), :] for j in range(WKV_VGROUP)]
            zero = tuple(jnp.zeros((SUBLANES, LANES), F32) for _ in range(2 * WKV_VGROUP))

            def key_chunk(kc, accs, p=p, g=g, vv=vv, sa=sa):
                yacc, san = list(accs[:WKV_VGROUP]), list(accs[WKV_VGROUP:])
                for i in range(WKV_KCHUNK):
                    kx = kc * WKV_KCHUNK + i
                    row = pl.ds(p * d + kx, 1)
                    wb = tr[WKV_SRC_W, t, row, :]
                    bb = tr[WKV_SRC_B, t, row, :]
                    kb = tr[WKV_SRC_K, t, row, :]
                    rb = tr[WKV_SRC_R, t, row, :]
                    an = tr[WKV_SRC_A, tn, row, :]
                    for j in range(WKV_VGROUP):
                        n = s_ref[p, kx, vrows(g, j), :] * wb + sa[j] * bb + vv[j] * kb
                        s_ref[p, kx, vrows(g, j), :] = n
                        yacc[j] = yacc[j] + n * rb
                        san[j] = san[j] + n * an
                return tuple(yacc) + tuple(san)

            accs = lax.fori_loop(0, d // WKV_KCHUNK, key_chunk, zero)
            yacc, san = accs[:WKV_VGROUP], accs[WKV_VGROUP:]
            for j in range(WKV_VGROUP):
                ybuf[t, p * d + g * vspan + j * SUBLANES:p * d + g * vspan + (j + 1) * SUBLANES, :] = yacc[j]
                sa_buf[p, vrows(g, j), :] = san[j]
        return carry

    lax.fori_loop(0, n_t, step, 0)

    def transpose_out(t, carry):
        y_ref[t] = jnp.transpose(ybuf[t])
        return carry

    lax.fori_loop(0, n_t, transpose_out, 0, unroll=4)


def _wkv_scan(r, w, k, v, a, b, s0, tt):
    t, rows, _ = r.shape
    d = A_HEAD_DIM
    seq = pl.BlockSpec((tt, LANES, LANES), lambda g, i: (i, g, 0))
    st = pl.BlockSpec((2, d, d, LANES), lambda g, i: (0, 0, 0, g))
    return pl.pallas_call(
        _wkv_kernel,
        grid=(rows // LANES, t // tt),
        in_specs=[seq] * 6 + [st],
        out_specs=[seq, st],
        out_shape=[jax.ShapeDtypeStruct((t, rows, LANES), F32), jax.ShapeDtypeStruct((2, d, d, rows), F32)],
        scratch_shapes=[pltpu.VMEM((6, tt, LANES, LANES), F32), pltpu.VMEM((2, d, LANES), F32),
                        pltpu.VMEM((tt, LANES, LANES), F32)],
        compiler_params=_cparams("parallel", "arbitrary"),
        name="wkv_scan",
    )(r, w, k, v, a, b, s0)


def _mlstm_kernel(uqk, uv, uo, ug, conv0, ct0, m0, cw, cb, gb, lng, lnb,
                  o_y, o_ct, o_m, ct, m_scr, carry, *, t_valid):
    tb = pl.program_id(1)

    @pl.when(tb == 0)
    def _():
        ct[...] = ct0[0]
        m_scr[...] = m0[0]
        carry[...] = conv0[0]

    L = uqk.shape[1]
    row = lax.broadcasted_iota(I32, (L, 1), 0)
    x = uqk[0]

    def shifted(j):
        out = pltpu.roll(x, j, 0)
        for i in range(j):
            out = jnp.where(row == i, carry[3 + i - j:4 + i - j, :], out)
        return out

    conv = x * cw[3:4, :] + shifted(1) * cw[2:3, :] + shifted(2) * cw[1:2, :] + shifted(3) * cw[0:1, :] + cb[...]
    carry[...] = x[L - 3:L, :]
    qk = conv * jax.nn.sigmoid(conv)
    q_all = qk[:, 0:B_QK_WIDTH]
    k_t = jnp.transpose(qk[:, B_QK_WIDTH:2 * B_QK_WIDTH] * (B_QK_DIM ** -0.5))

    lane = lax.broadcasted_iota(I32, (L, LANES), 1)
    gz = ug[0] + gb[...]
    lg = jnp.where(lane < B_HEADS, gz, jnp.minimum(gz, 0.0) - jnp.log(1.0 + jnp.exp(-jnp.abs(gz))))
    valid = (row + tb * L) < t_valid
    lg = jnp.where(valid, lg, jnp.where(lane < B_HEADS, -jnp.inf, 0.0))
    lg_t = jnp.transpose(lg)
    ti = lax.broadcasted_iota(I32, (L, L), 0)
    si = lax.broadcasted_iota(I32, (L, L), 1)
    tril = ti >= si
    tril_f = tril.astype(F32)
    triu_f = (ti <= si).astype(F32)
    lg_fin = jnp.where(lane < B_HEADS, 0.0, lg)
    bcum = jnp.dot(tril_f, lg_fin, preferred_element_type=F32, precision=lax.Precision.HIGHEST)
    bcum_t = jnp.dot(jnp.transpose(lg_fin), triu_f, preferred_element_type=F32,
                     precision=lax.Precision.HIGHEST)

    head_q = lax.shift_right_logical(lax.broadcasted_iota(I32, (L, B_QK_WIDTH), 1), 6)
    lane1 = lax.broadcasted_iota(I32, (1, LANES), 1)
    ones_col = (lax.broadcasted_iota(I32, (L, LANES), 1) == 0).astype(F32)
    m_vec = m_scr[...]
    m_new_vec = m_vec
    vv = uv[0]
    oo = uo[0]
    for h in range(B_HEADS):
        hs = slice(h * B_QK_DIM, (h + 1) * B_QK_DIM)
        vs = slice(h * B_V_DIM, (h + 1) * B_V_DIM)
        bc_col = bcum[:, B_HEADS + h:B_HEADS + h + 1]
        bc_row = bcum_t[B_HEADS + h:B_HEADS + h + 1, :]
        ic_row = lg_t[h:h + 1, :]
        gtot = bc_row[:, L - 1:L]
        m_prev = m_vec[:, h:h + 1]
        dmat = jnp.where(tril, bc_col - bc_row + ic_row, -jnp.inf)
        inter = bc_col + m_prev
        m_t = jnp.maximum(inter, jnp.max(dmat, axis=1, keepdims=True))
        w_intra = jnp.exp(dmat - m_t)
        w_inter = jnp.exp(inter - m_t)
        q_m = jnp.where(head_q == h, q_all, 0.0).astype(BF16)
        s = jnp.dot(q_m, k_t.astype(BF16), preferred_element_type=F32) * w_intra
        v_ext = jnp.concatenate([vv[:, vs], ones_col], axis=1).astype(BF16)
        intra = jnp.dot(s.astype(BF16), v_ext, preferred_element_type=F32)
        cross = jnp.dot(q_m, ct[...].astype(BF16), preferred_element_type=F32)
        num = intra[:, 0:B_V_DIM] + w_inter * cross[:, 0:B_V_DIM]
        den = jnp.sum(s, axis=1, keepdims=True) + w_inter * cross[:, B_V_DIM:B_V_DIM + 1]
        hh = num / jnp.maximum(jnp.abs(den), jnp.exp(-m_t))
        mu = jnp.mean(hh, axis=-1, keepdims=True)
        dd = hh - mu
        var = jnp.mean(dd * dd, axis=-1, keepdims=True)
        yn = dd * lax.rsqrt(var + LN_EPS) * lng[:, vs] + lnb[:, vs]
        o_y[0, :, vs] = jax.nn.sigmoid(oo[:, vs]) * yn
        lw = gtot - bc_row + ic_row
        m_new = jnp.maximum(gtot + m_prev, jnp.max(lw, axis=1, keepdims=True))
        w_s = jnp.exp(lw - m_new)
        dec = jnp.exp(gtot + m_prev - m_new)
        upd = jnp.dot((k_t[hs, :] * w_s).astype(BF16), v_ext, preferred_element_type=F32)
        ct[hs, :] = dec * ct[hs, :] + upd
        m_new_vec = jnp.where(lane1 == h, m_new, m_new_vec)
    m_scr[...] = m_new_vec

    @pl.when(tb == pl.num_programs(1) - 1)
    def _():
        o_ct[0] = ct[...]
        o_m[0] = m_scr[...]


def _mlstm(u, conv0, ct0, m0, p, L, t_valid):
    b, t, _ = u.shape
    blk = lambda w, c: pl.BlockSpec((1, L, w), lambda i, j, c=c: (i, j, c))
    full = lambda arr: pl.BlockSpec(arr.shape, lambda i, j: (0,) * arr.ndim)
    per_b = lambda arr: pl.BlockSpec((1,) + arr.shape[1:], lambda i, j: (i, 0, 0))
    params = [p['conv_w'], p['conv_b'], p['gate_b'], p['b_ln_g'], p['b_ln_b']]
    return pl.pallas_call(
        functools.partial(_mlstm_kernel, t_valid=t_valid),
        grid=(b, t // L),
        in_specs=[blk(1024, EV_QK // 1024), blk(1024, EV_VB // 1024), blk(1024, EV_O // 1024),
                  blk(LANES, EV_GATE // LANES), per_b(conv0), per_b(ct0), per_b(m0)] + [full(a) for a in params],
        out_specs=[pl.BlockSpec((1, L, B_V_WIDTH), lambda i, j: (i, j, 0)),
                   pl.BlockSpec((1, B_QK_WIDTH, 2 * LANES), lambda i, j: (i, 0, 0)),
                   pl.BlockSpec((1, 1, LANES), lambda i, j: (i, 0, 0))],
        out_shape=[jax.ShapeDtypeStruct((b, t, B_V_WIDTH), F32),
                   jax.ShapeDtypeStruct((b, B_QK_WIDTH, 2 * LANES), F32),
                   jax.ShapeDtypeStruct((b, 1, LANES), F32)],
        scratch_shapes=[pltpu.VMEM((B_QK_WIDTH, 2 * LANES), F32), pltpu.VMEM((1, LANES), F32),
                        pltpu.VMEM((3, 2 * B_QK_WIDTH), F32)],
        compiler_params=_cparams("parallel", "arbitrary"),
        name="mlstm",
    )(u, u, u, u, conv0, ct0, m0, *params)


def _even_out_kernel(y_ref, bonus_ref, g_ref, yb_ref, x_ref, lng, lnb, e_ref, woa, wob, g1, b1, o_ref):
    e = e_ref[...]
    y = y_ref[...]
    mu = _segsum(y, e) * (1.0 / A_HEAD_DIM)
    d = y - mu
    var = _segsum(d * d, e) * (1.0 / A_HEAD_DIM)
    yn = d * lax.rsqrt(var + LN_EPS) * lng[...] + lnb[...]
    ya = (yn + bonus_ref[...]) * g_ref[...]
    mix = (jnp.dot(ya.astype(BF16), woa[...], preferred_element_type=F32)
           + jnp.dot(yb_ref[...].astype(BF16), wob[...], preferred_element_type=F32))
    o_ref[...] = _layer_norm(ALPHA * x_ref[...] + mix, g1[...], b1[...])


def _even_out(y, bonus, g, yb, x, p, b, tm):
    m = x.shape[0]
    nt = m // b // tm
    tmaj = pl.BlockSpec((tm, D_MODEL), lambda i, j: (j, i))
    row = pl.BlockSpec((tm, D_MODEL), lambda i, j: (i * nt + j, 0))
    full = lambda arr: pl.BlockSpec(arr.shape, lambda i, j: (0,) * arr.ndim)
    params = [p['a_ln_g'], p['a_ln_b'], p['seg64'], p['w_out_a'], p['w_out_b'], p['ln1_g'], p['ln1_b']]
    return pl.pallas_call(
        _even_out_kernel,
        grid=(b, nt),
        in_specs=[tmaj] * 3 + [row] * 2 + [full(a) for a in params],
        out_specs=row,
        out_shape=jax.ShapeDtypeStruct((m, D_MODEL), F32),
        compiler_params=_cparams("parallel", "parallel"),
        name="even_out",
    )(y, bonus, g, yb, x, *params)


def _proj_ln_kernel(a_ref, x_ref, w_ref, g1, b1, o_ref):
    mix = jnp.dot(a_ref[...].astype(BF16), w_ref[...], preferred_element_type=F32)
    o_ref[...] = _layer_norm(ALPHA * x_ref[...] + mix, g1[...], b1[...])


def _proj_ln(a, x, w, g1, b1, tm):
    m = x.shape[0]
    row = pl.BlockSpec((tm, D_MODEL), lambda i: (i, 0))
    full = lambda arr: pl.BlockSpec(arr.shape, lambda i: (0,) * arr.ndim)
    return pl.pallas_call(
        _proj_ln_kernel,
        grid=(m // tm,),
        in_specs=[row, row, full(w), full(g1), full(b1)],
        out_specs=row,
        out_shape=jax.ShapeDtypeStruct((m, D_MODEL), F32),
        compiler_params=_cparams("parallel"),
        name="proj_ln",
    )(a, x, w, g1, b1)


def _experts_kernel(x_ref, wr, br, wg, wu, wd, g2, b2, o_ref, xb, comb, acc, *, routed):
    e = pl.program_id(1)

    @pl.when(e == 0)
    def _():
        x = x_ref[...]
        xb[...] = x.astype(BF16)
        acc[...] = jnp.zeros_like(acc)
        if routed:
            logits = jnp.dot(x, wr[...], preferred_element_type=F32, precision=lax.Precision.HIGHEST) + br[...]
            lane = lax.broadcasted_iota(I32, logits.shape, 1).astype(F32)
            v1 = jnp.max(logits, axis=-1, keepdims=True)
            i1 = jnp.min(jnp.where(logits == v1, lane, float(LANES)), axis=-1, keepdims=True)
            rest = jnp.where(lane == i1, -jnp.inf, logits)
            v2 = jnp.max(rest, axis=-1, keepdims=True)
            i2 = jnp.min(jnp.where(rest == v2, lane, float(LANES)), axis=-1, keepdims=True)
            e2 = jnp.exp(v2 - v1)
            den = 1.0 + e2
            comb[...] = jnp.where(lane == i1, 1.0 / den, jnp.where(lane == i2, e2 / den, 0.0))

    xbv = xb[...]
    hg = jnp.dot(xbv, wg[0], preferred_element_type=F32)
    hu = jnp.dot(xbv, wu[0], preferred_element_type=F32)
    hid = (hg * jax.nn.sigmoid(hg)) * hu
    y = jnp.dot(hid.astype(BF16), wd[0], preferred_element_type=F32)
    if routed:
        lane = lax.broadcasted_iota(I32, comb.shape, 1)
        ce = jnp.sum(jnp.where(lane == e, comb[...], 0.0), axis=-1, keepdims=True)
        y = ce * y
    acc[...] += y

    @pl.when(e == pl.num_programs(1) - 1)
    def _():
        o_ref[...] = _layer_norm(ALPHA * x_ref[...] + acc[...], g2[...], b2[...])


def _experts(x, wr, br, wg, wu, wd, g2, b2, tm, routed):
    m = x.shape[0]
    ne, _, dff = wg.shape
    row = pl.BlockSpec((tm, D_MODEL), lambda i, e: (i, 0))
    full = lambda arr: pl.BlockSpec(arr.shape, lambda i, e: (0,) * arr.ndim)
    return pl.pallas_call(
        functools.partial(_experts_kernel, routed=routed),
        grid=(m // tm, ne),
        in_specs=[row, full(wr), full(br),
                  pl.BlockSpec((1, D_MODEL, dff), lambda i, e: (e, 0, 0)),
                  pl.BlockSpec((1, D_MODEL, dff), lambda i, e: (e, 0, 0)),
                  pl.BlockSpec((1, dff, D_MODEL), lambda i, e: (e, 0, 0)),
                  full(g2), full(b2)],
        out_specs=row,
        out_shape=jax.ShapeDtypeStruct((m, D_MODEL), F32),
        scratch_shapes=[pltpu.VMEM((tm, D_MODEL), BF16), pltpu.VMEM((tm, LANES), F32),
                        pltpu.VMEM((tm, D_MODEL), F32)],
        compiler_params=_cparams("parallel", "arbitrary"),
        name="experts_routed" if routed else "experts_dense",
    )(x, wr, br, wg, wu, wd, g2, b2)


def _dsa_kernel(q_ref, qi_ref, wi_ref, kt_ref, v_ref, kit_ref, prev_ref, o_ref, key_scr, jcut_scr,
                *, top_k, causal, limit_const, blk_off):
    del prev_ref
    tq = q_ref.shape[1]
    s_len = kt_ref.shape[2]
    if causal:
        row_chunk = lax.shift_right_logical(lax.broadcasted_iota(I32, (tq, 1), 0), CHUNK.bit_length() - 1)
        chunk = (pl.program_id(1) + blk_off) * (tq // CHUNK) + row_chunk
        limit = (chunk + 1) * CHUNK
        kth = jnp.minimum(top_k, limit).astype(F32)
    else:
        limit = limit_const
        kth = jnp.full((tq, 1), min(top_k, limit_const), F32)
    idx = lax.broadcasted_iota(I32, (tq, s_len), 1)

    kit = kit_ref[0]
    wi = wi_ref[0] * (IDX_HEADS ** -0.5)
    score = jnp.zeros((tq, s_len), F32)
    for h in range(IDX_HEADS):
        rel = jnp.dot(qi_ref[0, h], kit, preferred_element_type=F32)
        score = score + wi[:, IDX_DIM + h:IDX_DIM + h + 1] * jnp.maximum(rel, 0.0)
    score = jnp.where(score == 0.0, 0.0, score)
    bits = pltpu.bitcast(score, I32)
    key = jnp.where(bits < 0, bits ^ 0x7FFFFFFF, bits)
    key_scr[...] = jnp.where(idx < limit, key, INT_MIN)

    def bit_step(i, thr):
        cand = thr + lax.shift_left(jnp.int32(1), 31 - i)
        cnt = jnp.sum(jnp.where(key_scr[...] >= cand, 1.0, 0.0), axis=1, keepdims=True)
        return jnp.where(cnt >= kth, cand, thr)

    thr = lax.fori_loop(0, 32, bit_step, jnp.full((tq, 1), INT_MIN, I32))
    keyv = key_scr[...]
    n_ge = jnp.sum(jnp.where(keyv >= thr, 1.0, 0.0), axis=1, keepdims=True)
    jcut_scr[...] = jnp.full((tq, 1), s_len, I32)

    @pl.when(jnp.max(n_ge - kth) > 0.0)
    def _():
        need = kth - jnp.sum(jnp.where(key_scr[...] > thr, 1.0, 0.0), axis=1, keepdims=True)
        n_bits = max(1, (s_len - 1).bit_length())

        def idx_step(i, j):
            cand = j + lax.shift_left(jnp.int32(1), n_bits - 1 - i)
            cnt = jnp.sum(jnp.where(key_scr[...] == thr, jnp.where(idx < cand, 1.0, 0.0), 0.0),
                          axis=1, keepdims=True)
            return jnp.where(cnt < need, cand, j)

        jcut_scr[...] = lax.fori_loop(0, n_bits, idx_step, jnp.zeros((tq, 1), I32))

    jcut = jcut_scr[...]
    bias = jnp.where(keyv > thr, 0.0,
                     jnp.where(keyv == thr, jnp.where(idx <= jcut, 0.0, -jnp.inf), -jnp.inf))

    q = q_ref[0]
    for grp in range(C_KV_HEADS):
        ks = slice(grp * C_HEAD_DIM, (grp + 1) * C_HEAD_DIM)
        kt = kt_ref[0, ks, :]
        vg = v_ref[0, :, ks]
        hsl = [slice((grp * C_GROUP + hg) * C_HEAD_DIM, (grp * C_GROUP + hg + 1) * C_HEAD_DIM)
               for hg in range(C_GROUP)]
        qg = jnp.concatenate([q[:, sl] for sl in hsl], axis=0).astype(BF16)
        lg = jnp.dot(qg, kt, preferred_element_type=F32) * (C_HEAD_DIM ** -0.5)
        ps, dens = [], []
        for hg in range(C_GROUP):
            logits = lg[hg * tq:(hg + 1) * tq, :] + bias
            p = jnp.exp(logits - jnp.max(logits, axis=1, keepdims=True))
            dens.append(jnp.sum(p, axis=1, keepdims=True))
            ps.append(p.astype(BF16))
        out = jnp.dot(jnp.concatenate(ps, axis=0), vg, preferred_element_type=F32)
        for hg in range(C_GROUP):
            o_ref[0, :, hsl[hg]] = out[hg * tq:(hg + 1) * tq, :] / dens[hg]


DSA_KEY_STEP = 512


def _dsa_call(u, qi_heads, kt, v, kit, prev, tq, top_k, causal, limit_const, blk_off, n_blk, s_len):
    b, t, _ = u.shape
    return pl.pallas_call(
        functools.partial(_dsa_kernel, top_k=top_k, causal=causal, limit_const=limit_const, blk_off=blk_off),
        grid=(b, n_blk),
        in_specs=[pl.BlockSpec((1, tq, C_WIDTH), lambda i, j: (i, j + blk_off, OD_Q // C_WIDTH)),
                  pl.BlockSpec((1, IDX_HEADS, tq, LANES), lambda i, j: (i, 0, j + blk_off, 0)),
                  pl.BlockSpec((1, tq, LANES), lambda i, j: (i, j + blk_off, OD_KI // LANES)),
                  pl.BlockSpec((1, 2 * C_HEAD_DIM, s_len), lambda i, j: (i, 0, 0)),
                  pl.BlockSpec((1, s_len, 2 * C_HEAD_DIM), lambda i, j: (i, 0, 0)),
                  pl.BlockSpec((1, LANES, s_len), lambda i, j: (i, 0, 0)),
                  pl.BlockSpec(memory_space=pl.ANY)],
        out_specs=pl.BlockSpec((1, tq, C_WIDTH), lambda i, j: (i, j + blk_off, 0)),
        out_shape=jax.ShapeDtypeStruct((b, t, C_WIDTH), F32),
        scratch_shapes=[pltpu.VMEM((tq, s_len), I32), pltpu.VMEM((tq, 1), I32)],
        input_output_aliases={6: 0},
        compiler_params=_cparams("parallel", "arbitrary"),
        name="dsa",
    )(u, qi_heads, u, kt, v, kit, prev)


def _dsa(u, qi_heads, kt, v, kit, tq, top_k, causal, limit_const):
    b, t, _ = u.shape
    s_full = kt.shape[2]
    att = jnp.zeros((b, t, C_WIDTH), F32)
    if not causal or s_full % DSA_KEY_STEP:
        return _dsa_call(u, qi_heads, kt, v, kit, att, tq, top_k, causal, limit_const, 0, t // tq, s_full)
    per = DSA_KEY_STEP // tq
    for cls in range(s_full // DSA_KEY_STEP):
        att = _dsa_call(u, qi_heads, kt, v, kit, att, tq, top_k, causal, limit_const,
                        cls * per, per, (cls + 1) * DSA_KEY_STEP)
    return att


def _prep_even(w):
    win = w['w_in']
    a, bq = win[:, :A_COLS], win[:, A_COLS:]
    cols = [a[:, 0:3072], bq[:, 0:3072], a[:, 3072:A_COLS], bq[:, 3072:3088],
            jnp.zeros((D_MODEL, EV_COLS_PAD - EV_GATE - 2 * B_HEADS), F32)]
    p = {'w_in': jnp.concatenate(cols, axis=1).astype(BF16)}
    mu = w['mu']
    p['mu_main'] = mu[None, 0:3072]
    p['mu_lora'] = mu[None, 3072:A_COLS]
    row = lambda v: v.reshape(1, -1)
    p['w0'], p['a0'], p['k_k'], p['k_a'] = row(w['w0']), row(w['a0']), row(w['k_k']), row(w['k_a'])
    p['r_k'] = row(w['r_k'])
    z = lambda n: jnp.zeros((n, A_WIDTH), F32)
    p['w2p'] = jnp.concatenate([w['w2'], z(192)], axis=0).astype(BF16)
    p['a2p'] = jnp.concatenate([z(64), w['a2'], z(128)], axis=0).astype(BF16)
    p['g2p'] = jnp.concatenate([z(128), w['g2']], axis=0).astype(BF16)
    seg = jnp.arange(A_WIDTH) // A_HEAD_DIM
    p['seg64'] = (seg[:, None] == seg[None, :]).astype(BF16)
    p['a_ln_g'], p['a_ln_b'] = row(w['a_ln_g']), row(w['a_ln_b'])
    p['conv_w'], p['conv_b'] = w['conv_w'], row(w['conv_b'])
    p['gate_b'] = jnp.concatenate([w['b_i'], w['b_f'], jnp.zeros((LANES - 2 * B_HEADS,), F32)])[None]
    p['b_ln_g'], p['b_ln_b'] = row(w['b_ln_g']), row(w['b_ln_b'])
    p['w_out_a'] = w['w_out'][:A_WIDTH].astype(BF16)
    p['w_out_b'] = w['w_out'][A_WIDTH:].astype(BF16)
    p['ln1_g'], p['ln1_b'], p['ln2_g'], p['ln2_b'] = row(w['ln1_g']), row(w['ln1_b']), row(w['ln2_g']), row(w['ln2_b'])
    half = w['ffn_gate'].shape[1] // 2
    p['ffn_g'] = jnp.stack([w['ffn_gate'][:, :half], w['ffn_gate'][:, half:]]).astype(BF16)
    p['ffn_u'] = jnp.stack([w['ffn_up'][:, :half], w['ffn_up'][:, half:]]).astype(BF16)
    p['ffn_d'] = jnp.stack([w['ffn_down'][:half], w['ffn_down'][half:]]).astype(BF16)
    p['no_router_w'] = jnp.zeros((D_MODEL, LANES), F32)
    p['no_router_b'] = jnp.zeros((1, LANES), F32)
    return p


def _prep_odd(w):
    row = lambda v: v.reshape(1, -1)
    p = {'w_in': jnp.pad(w['w_in'], ((0, 0), (0, OD_COLS_PAD - w['w_in'].shape[1]))).astype(BF16)}
    p['w_out'] = w['w_out'].astype(BF16)
    p['ln1_g'], p['ln1_b'], p['ln2_g'], p['ln2_b'] = row(w['ln1_g']), row(w['ln1_b']), row(w['ln2_g']), row(w['ln2_b'])
    p['router_w'] = jnp.pad(w['router'], ((0, 0), (0, LANES - N_EXPERTS)))
    p['router_b'] = jnp.concatenate([w['router_b'], jnp.full((LANES - N_EXPERTS,), -1e30, F32)])[None]
    p['e_gate'], p['e_up'], p['e_down'] = w['e_gate'].astype(BF16), w['e_up'].astype(BF16), w['e_down'].astype(BF16)
    return p


def _pick(n, pref):
    for c in pref:
        if n % c == 0:
            return c
    return n


def _even_layer(x, shift0, wkv0, conv0, c0, n0, m0, p):
    b, t, _ = x.shape
    m = b * t
    tm = _pick(m, (512, 256, 128, 64, 32, 16, 8))
    u = _matmul(x.reshape(m, D_MODEL), p['w_in'], tm, EV_COLS_PAD // 3).reshape(b, t, EV_COLS_PAD)

    new_shift = jnp.concatenate([u[:, t - 1:, 0:3072], u[:, t - 1:, EV_LORA:EV_LORA + A_LORA]], axis=-1)
    new_conv = jnp.concatenate([conv0, u[:, :, EV_QK:EV_QK + 1024]], axis=1)[:, t:]

    s0_main, s0_lora = shift0[:, :, 0:3072], shift0[:, :, 3072:A_COLS]
    tp = _pick(t, (256, 128, 64, 32, 16, 8))
    r, w, k, v, a, bb, g, bonus = _rwkv_pre(u, s0_main, s0_lora, p, tp)
    rows = b * A_HEADS // 2
    assert rows % LANES == 0, "batch must be a multiple of 16"
    to_rows = lambda z: z.reshape(t, rows, LANES)
    s0 = wkv0.reshape(b, A_HEADS // 2, 2, A_HEAD_DIM, A_HEAD_DIM).transpose(2, 4, 3, 0, 1)
    s0 = s0.reshape(2, A_HEAD_DIM, A_HEAD_DIM, rows)
    tt = _pick(t, (16, 8))
    y, s_fin = _wkv_scan(*(to_rows(z) for z in (r, w, k, v, a, bb)), s0, tt)
    y = y.reshape(t, b * A_WIDTH)
    new_wkv = s_fin.reshape(2, A_HEAD_DIM, A_HEAD_DIM, b, A_HEADS // 2).transpose(3, 4, 0, 2, 1)
    new_wkv = new_wkv.reshape(b, A_HEADS, A_HEAD_DIM, A_HEAD_DIM)

    ct0 = jnp.concatenate([c0.transpose(0, 1, 3, 2).reshape(b, B_QK_WIDTH, B_V_DIM),
                           n0.reshape(b, B_QK_WIDTH, 1),
                           jnp.zeros((b, B_QK_WIDTH, LANES - 1), F32)], axis=-1)
    m0p = jnp.pad(m0, ((0, 0), (0, LANES - B_HEADS)))[:, None, :]
    lm = _pick(t, (256, 128))
    if t % LANES:
        t_pad = -t % LANES
        u_b = jnp.pad(u, ((0, 0), (0, t_pad), (0, 0)))
        lm = LANES
    else:
        u_b = u
    yb, ct, m_out = _mlstm(u_b, conv0, ct0, m0p, p, lm, t)
    yb = yb[:, :t].reshape(m, B_V_WIDTH)
    new_c = ct[:, :, 0:B_V_DIM].reshape(b, B_HEADS, B_QK_DIM, B_V_DIM).transpose(0, 1, 3, 2)
    new_n = ct[:, :, B_V_DIM].reshape(b, B_HEADS, B_QK_DIM)
    new_m = m_out[:, 0, 0:B_HEADS]

    x2 = x.reshape(m, D_MODEL)
    x2 = _even_out(y, bonus, g, yb, x2, p, b, _pick(t, (256, 128, 64, 32, 16, 8)))
    x2 = _experts(x2, p['no_router_w'], p['no_router_b'], p['ffn_g'], p['ffn_u'], p['ffn_d'],
                  p['ln2_g'], p['ln2_b'], tm, routed=False)
    return x2.reshape(b, t, D_MODEL), (new_shift, new_wkv, new_conv, new_c, new_n, new_m)


def _odd_layer(x, past_k, past_v, past_ki, p):
    b, t, _ = x.shape
    m = b * t
    tm = _pick(m, (512, 256, 128, 64, 32, 16, 8))
    u = _matmul(x.reshape(m, D_MODEL), p['w_in'], tm, OD_COLS_PAD).reshape(b, t, OD_COLS_PAD)
    k_new = u[:, :, OD_K:OD_K + 256]
    v_new = u[:, :, OD_V:OD_V + 256]
    ki_new = u[:, :, OD_KI:OD_KI + IDX_DIM]
    if past_k is None:
        keys_k, keys_v, keys_i = k_new, v_new, ki_new
        causal, limit, tq = True, 0, _pick(t, (2 * CHUNK, CHUNK))
        top_k = min(TOPK_MAX, t // 4)
    else:
        keys_k = jnp.concatenate([past_k.reshape(b, -1, 256), k_new], axis=1)
        keys_v = jnp.concatenate([past_v.reshape(b, -1, 256), v_new], axis=1)
        keys_i = jnp.concatenate([past_ki, ki_new], axis=1)
        limit = keys_k.shape[1]
        causal, tq = False, t
        top_k = min(TOPK_MAX, limit // 4)
    s_pad = -keys_k.shape[1] % LANES
    pad_s = lambda z: jnp.pad(z, ((0, 0), (0, s_pad), (0, 0))) if s_pad else z
    kt = pad_s(keys_k).transpose(0, 2, 1).astype(BF16)
    vv = pad_s(keys_v).astype(BF16)
    kit = jnp.pad(pad_s(keys_i), ((0, 0), (0, 0), (0, LANES - IDX_DIM))).transpose(0, 2, 1).astype(BF16)
    qi = u[:, :, OD_QI:OD_QI + IDX_HEADS * IDX_DIM].reshape(b, t, IDX_HEADS, IDX_DIM).transpose(0, 2, 1, 3)
    qi = jnp.pad(qi, ((0, 0), (0, 0), (0, 0), (0, LANES - IDX_DIM))).astype(BF16)
    att = _dsa(u, qi, kt, vv, kit, tq, top_k, causal, limit)
    x2 = _proj_ln(att.reshape(m, C_WIDTH), x.reshape(m, D_MODEL), p['w_out'], p['ln1_g'], p['ln1_b'],
                  _pick(m, (512, 256, 128, 64, 32, 16, 8)))
    x2 = _experts(x2, p['router_w'], p['router_b'], p['e_gate'], p['e_up'], p['e_down'],
                  p['ln2_g'], p['ln2_b'], tm, routed=True)
    st = (k_new.reshape(b, t, C_KV_HEADS, C_HEAD_DIM), v_new.reshape(b, t, C_KV_HEADS, C_HEAD_DIM), ki_new)
    return x2.reshape(b, t, D_MODEL), st


def _trunk(x, shift0, wkv0, conv0, c0, n0, m0, past_k, past_v, past_ki, ev, od):
    even_out = [[] for _ in range(6)]
    odd_out = [[] for _ in range(3)]
    for layer in range(DEPTH):
        i = layer // 2
        if layer % 2 == 0:
            x, st = _even_layer(x, shift0[i], wkv0[i], conv0[i], c0[i], n0[i], m0[i], ev[i])
            for acc, val in zip(even_out, st):
                acc.append(val)
        else:
            if past_k is None:
                x, st = _odd_layer(x, None, None, None, od[i])
            else:
                x, st = _odd_layer(x, past_k[i], past_v[i], past_ki[i], od[i])
            for acc, val in zip(odd_out, st):
                acc.append(val)
    return x, [jnp.stack(v) for v in even_out], [jnp.stack(v) for v in odd_out]


def kernel(x_prompt, x_sample, state_shift, state_wkv, state_conv, state_c, state_n, state_m,
           cache_k, cache_v, cache_idx_k,
           ev_w_in, a_mu, a_w0, a_w2, a_a0, a_a2, a_g2, a_k_k, a_k_a, a_r_k, a_ln_g, a_ln_b,
           b_conv_w, b_conv_b, b_i_bias, b_f_bias, b_ln_g, b_ln_b, ev_w_out, ev_ln1_g, ev_ln1_b,
           ffn_w_gate, ffn_w_up, ffn_w_down, ev_ln2_g, ev_ln2_b,
           od_w_in, od_w_out, od_ln1_g, od_ln1_b, moe_w_router, moe_b_router,
           moe_w_gate, moe_w_up, moe_w_down, od_ln2_g, od_ln2_b):
    ew = {'w_in': ev_w_in, 'mu': a_mu, 'w0': a_w0, 'w2': a_w2, 'a0': a_a0, 'a2': a_a2, 'g2': a_g2,
          'k_k': a_k_k, 'k_a': a_k_a, 'r_k': a_r_k, 'a_ln_g': a_ln_g, 'a_ln_b': a_ln_b,
          'conv_w': b_conv_w, 'conv_b': b_conv_b, 'b_i': b_i_bias, 'b_f': b_f_bias,
          'b_ln_g': b_ln_g, 'b_ln_b': b_ln_b, 'w_out': ev_w_out, 'ln1_g': ev_ln1_g, 'ln1_b': ev_ln1_b,
          'ffn_gate': ffn_w_gate, 'ffn_up': ffn_w_up, 'ffn_down': ffn_w_down,
          'ln2_g': ev_ln2_g, 'ln2_b': ev_ln2_b}
    ow = {'w_in': od_w_in, 'w_out': od_w_out, 'ln1_g': od_ln1_g, 'ln1_b': od_ln1_b,
          'router': moe_w_router, 'router_b': moe_b_router, 'e_gate': moe_w_gate, 'e_up': moe_w_up,
          'e_down': moe_w_down, 'ln2_g': od_ln2_g, 'ln2_b': od_ln2_b}
    n_even, n_odd = ev_w_in.shape[0], od_w_in.shape[0]
    ev = [_prep_even({k: v[i] for k, v in ew.items()}) for i in range(n_even)]
    od = [_prep_odd({k: v[i] for k, v in ow.items()}) for i in range(n_odd)]

    bp = x_prompt.shape[0]
    z = functools.partial(jnp.zeros, dtype=F32)
    y_p, ep, op = _trunk(x_prompt,
                         z((n_even, bp, 1, A_COLS)), z((n_even, bp, A_HEADS, A_HEAD_DIM, A_HEAD_DIM)),
                         z((n_even, bp, 3, 2 * B_QK_WIDTH)), z((n_even, bp, B_HEADS, B_V_DIM, B_QK_DIM)),
                         z((n_even, bp, B_HEADS, B_QK_DIM)), z((n_even, bp, B_HEADS)),
                         None, None, None, ev, od)
    y_s, es, os_ = _trunk(x_sample, state_shift, state_wkv, state_conv, state_c, state_n, state_m,
                          cache_k, cache_v, cache_idx_k, ev, od)
    return (y_p, y_s,
            ep[0], es[0], ep[1], es[1], ep[2], es[2], ep[3], es[3], ep[4], es[4], ep[5], es[5],
            op[0], os_[0], op[1], os_[1], op[2], os_[2])
```

```python
import functools
import math

import jax
import jax.numpy as jnp
from jax import lax
from jax.experimental import pallas as pl
from jax.experimental.pallas import tpu as pltpu

F32 = jnp.float32
BF16 = jnp.bfloat16
I32 = jnp.int32

D_MODEL = 1024
DEPTH = 4
CHUNK = 64
A_HEADS, A_HEAD_DIM, A_WIDTH = 16, 64, 1024
A_LORA = 256
A_COLS = 3 * A_WIDTH + A_LORA
B_HEADS, B_QK_DIM, B_V_DIM = 8, 64, 128
B_QK_WIDTH, B_V_WIDTH = 512, 1024
C_HEADS, C_KV_HEADS, C_HEAD_DIM, C_GROUP = 8, 2, 128, 4
C_WIDTH = 1024
IDX_HEADS, IDX_DIM = 8, 64
TOPK_MAX = 256
N_EXPERTS = 8
D_FF_EXPERT = 1408
ALPHA = (2 * DEPTH) ** 0.25
LN_EPS = 1e-5

LANES = 128
SUBLANES = 8
VMEM_LIMIT = 56 * 1024 * 1024

EV_R, EV_K, EV_V, EV_QK, EV_VB, EV_O, EV_LORA, EV_GATE = 0, 1024, 2048, 3072, 4096, 5120, 6144, 6400
EV_COLS_PAD = 6528
OD_Q, OD_K, OD_V, OD_QI, OD_KI = 0, 1024, 1280, 1536, 2048
OD_COLS_PAD = 2176

INT_MIN = -2 ** 31


def _cparams(*sem):
    return pltpu.CompilerParams(dimension_semantics=sem, vmem_limit_bytes=VMEM_LIMIT)


def _layer_norm(z, g, b):
    mu = jnp.mean(z, axis=-1, keepdims=True)
    d = z - mu
    var = jnp.mean(d * d, axis=-1, keepdims=True)
    return d * lax.rsqrt(var + LN_EPS) * g + b


def _segsum(x, e):
    hi = x.astype(BF16)
    lo = (x - hi.astype(F32)).astype(BF16)
    return jnp.dot(hi, e, preferred_element_type=F32) + jnp.dot(lo, e, preferred_element_type=F32)


def _mm_kernel(x_ref, w_ref, o_ref):
    o_ref[...] = jnp.dot(x_ref[...].astype(BF16), w_ref[...], preferred_element_type=F32)


def _matmul(x, w, tm, tn):
    m, k = x.shape
    n = w.shape[1]
    return pl.pallas_call(
        _mm_kernel,
        grid=(n // tn, m // tm),
        in_specs=[pl.BlockSpec((tm, k), lambda j, i: (i, 0)),
                  pl.BlockSpec((k, tn), lambda j, i: (0, j))],
        out_specs=pl.BlockSpec((tm, tn), lambda j, i: (i, j)),
        out_shape=jax.ShapeDtypeStruct((m, n), F32),
        compiler_params=_cparams("parallel", "parallel"),
        name="proj_in",
    )(x, w)


def _rwkv_pre_kernel(ur, uk, uv, ul, s0m, s0l, mum, mul_, w0, a0, kk_, ka_, rk_, w2, a2, g2, e_ref,
                     o_r, o_w, o_k, o_v, o_a, o_b, o_g, o_bonus, prev_m, prev_l):
    @pl.when(pl.program_id(1) == 0)
    def _():
        prev_m[...] = s0m[0]
        prev_l[...] = s0l[0]

    tm = ur.shape[1]
    row = lax.broadcasted_iota(I32, (tm, 1), 0)

    def lerp(x, prev, mu):
        shifted = jnp.where(row == 0, prev, pltpu.roll(x, 1, 0))
        return x + (shifted - x) * mu

    r_raw, k_raw, v_raw, l_raw = ur[0], uk[0], uv[0], ul[0]
    r = lerp(r_raw, prev_m[:, 0:1024], mum[:, 0:1024])
    k = lerp(k_raw, prev_m[:, 1024:2048], mum[:, 1024:2048])
    v = lerp(v_raw, prev_m[:, 2048:3072], mum[:, 2048:3072])
    lo = lerp(l_raw, prev_l[...], mul_[...])
    prev_m[:, 0:1024] = r_raw[tm - 1:tm, :]
    prev_m[:, 1024:2048] = k_raw[tm - 1:tm, :]
    prev_m[:, 2048:3072] = v_raw[tm - 1:tm, :]
    prev_l[...] = l_raw[tm - 1:tm, :]

    zw = w0[...] + jnp.dot(jnp.tanh(lo).astype(BF16), w2[...], preferred_element_type=F32)
    decay = jnp.exp(-(math.exp(-0.5) * jax.nn.sigmoid(zw)))
    a = jax.nn.sigmoid(a0[...] + jnp.dot(lo.astype(BF16), a2[...], preferred_element_type=F32))
    g = jnp.dot(jax.nn.sigmoid(lo).astype(BF16), g2[...], preferred_element_type=F32)

    e = e_ref[...]
    kk = k * kk_[...]
    kk = kk * lax.rsqrt(jnp.maximum(_segsum(kk * kk, e), 1e-24))
    k2 = k * (1.0 + (a - 1.0) * ka_[...])
    bonus = _segsum(r * k2 * rk_[...], e) * v

    o_r[...] = r
    o_w[...] = decay
    o_k[...] = k2
    o_v[...] = v
    o_a[...] = -kk
    o_b[...] = kk * a
    o_g[...] = g
    o_bonus[...] = bonus


def _rwkv_pre(u, s0_main, s0_lora, p, tm):
    b, t, _ = u.shape
    blk = lambda w, c: pl.BlockSpec((1, tm, w), lambda i, j, c=c: (i, j, c))
    full = lambda arr: pl.BlockSpec(arr.shape, lambda i, j: (0,) * arr.ndim)
    params = [p['mu_main'], p['mu_lora'], p['w0'], p['a0'], p['k_k'], p['k_a'], p['r_k'],
              p['w2p'], p['a2p'], p['g2p'], p['seg64']]
    out = jax.ShapeDtypeStruct((t, b * A_WIDTH), F32)
    return pl.pallas_call(
        _rwkv_pre_kernel,
        grid=(b, t // tm),
        in_specs=[blk(1024, EV_R // 1024), blk(1024, EV_K // 1024), blk(1024, EV_V // 1024),
                  blk(A_LORA, EV_LORA // A_LORA),
                  pl.BlockSpec((1, 1, 3072), lambda i, j: (i, 0, 0)),
                  pl.BlockSpec((1, 1, A_LORA), lambda i, j: (i, 0, 0))] + [full(a) for a in params],
        out_specs=[pl.BlockSpec((tm, A_WIDTH), lambda i, j: (j, i))] * 8,
        out_shape=[out] * 8,
        scratch_shapes=[pltpu.VMEM((1, 3072), F32), pltpu.VMEM((1, A_LORA), F32)],
        compiler_params=_cparams("parallel", "arbitrary"),
        name="rwkv_pre",
    )(u, u, u, u, s0_main, s0_lora, *params)


WKV_SRC_R, WKV_SRC_W, WKV_SRC_K, WKV_SRC_V, WKV_SRC_A, WKV_SRC_B = range(6)
WKV_VGROUP = 4
WKV_KCHUNK = 16


def _wkv_kernel(r_ref, w_ref, k_ref, v_ref, a_ref, b_ref, s0_ref, y_ref, s_ref, tr, sa_buf, ybuf):
    @pl.when(pl.program_id(1) == 0)
    def _():
        s_ref[...] = s0_ref[...]

    n_t = r_ref.shape[0]
    d = A_HEAD_DIM
    srcs = (r_ref, w_ref, k_ref, v_ref, a_ref, b_ref)
    vspan = WKV_VGROUP * SUBLANES
    groups = [(p, g) for p in range(2) for g in range(d // vspan)]

    def vrows(g, j):
        return slice(g * vspan + j * SUBLANES, g * vspan + (j + 1) * SUBLANES)

    def transpose_in(t, carry):
        for i, src in enumerate(srcs):
            tr[i, t] = jnp.transpose(src[t])
        return carry

    lax.fori_loop(0, n_t, transpose_in, 0)

    for p, g in groups:
        acc = [jnp.zeros((SUBLANES, LANES), F32) for _ in range(WKV_VGROUP)]
        for kx in range(d):
            ab = tr[WKV_SRC_A, 0, p * d + kx:p * d + kx + 1, :]
            for j in range(WKV_VGROUP):
                acc[j] = acc[j] + s_ref[p, kx, vrows(g, j), :] * ab
        for j in range(WKV_VGROUP):
            sa_buf[p, vrows(g, j), :] = acc[j]

    def step(t, carry):
        tn = jnp.minimum(t + 1, n_t - 1)
        for p, g in groups:
            vv = [tr[WKV_SRC_V, t, p * d + g * vspan + j * SUBLANES:p * d + g * vspan + (j + 1) * SUBLANES, :]
                  for j in range(WKV_VGROUP)]
            sa = [sa_buf[p, vrows(g, j), :] for j in range(WKV_VGROUP)]
            zero = tuple(jnp.zeros((SUBLANES, LANES), F32) for _ in range(2 * WKV_VGROUP))

            def key_chunk(kc, accs, p=p, g=g, vv=vv, sa=sa):
                yacc, san = list(accs[:WKV_VGROUP]), list(accs[WKV_VGROUP:])
                for i in range(WKV_KCHUNK):
                    kx = kc * WKV_KCHUNK + i
                    row = pl.ds(p * d + kx, 1)
                    wb = tr[WKV_SRC_W, t, row, :]
                    bb = tr[WKV_SRC_B, t, row, :]
                    kb = tr[WKV_SRC_K, t, row, :]
                    rb = tr[WKV_SRC_R, t, row, :]
                    an = tr[WKV_SRC_A, tn, row, :]
                    for j in range(WKV_VGROUP):
                        n = s_ref[p, kx, vrows(g, j), :] * wb + sa[j] * bb + vv[j] * kb
                        s_ref[p, kx, vrows(g, j), :] = n
                        yacc[j] = yacc[j] + n * rb
                        san[j] = san[j] + n * an
                return tuple(yacc) + tuple(san)

            accs = lax.fori_loop(0, d // WKV_KCHUNK, key_chunk, zero)
            yacc, san = accs[:WKV_VGROUP], accs[WKV_VGROUP:]
            for j in range(WKV_VGROUP):
                ybuf[t, p * d + g * vspan + j * SUBLANES:p * d + g * vspan + (j + 1) * SUBLANES, :] = yacc[j]
                sa_buf[p, vrows(g, j), :] = san[j]
        return carry

    lax.fori_loop(0, n_t, step, 0)

    def transpose_out(t, carry):
        y_ref[t] = jnp.transpose(ybuf[t])
        return carry

    lax.fori_loop(0, n_t, transpose_out, 0, unroll=4)


def _wkv_scan(r, w, k, v, a, b, s0, tt):
    t, rows, _ = r.shape
    d = A_HEAD_DIM
    seq = pl.BlockSpec((tt, LANES, LANES), lambda g, i: (i, g, 0))
    st = pl.BlockSpec((2, d, d, LANES), lambda g, i: (0, 0, 0, g))
    return pl.pallas_call(
        _wkv_kernel,
        grid=(rows // LANES, t // tt),
        in_specs=[seq] * 6 + [st],
        out_specs=[seq, st],
        out_shape=[jax.ShapeDtypeStruct((t, rows, LANES), F32), jax.ShapeDtypeStruct((2, d, d, rows), F32)],
        scratch_shapes=[pltpu.VMEM((6, tt, LANES, LANES), F32), pltpu.VMEM((2, d, LANES), F32),
                        pltpu.VMEM((tt, LANES, LANES), F32)],
        compiler_params=_cparams("parallel", "arbitrary"),
        name="wkv_scan",
    )(r, w, k, v, a, b, s0)


def _mlstm_kernel(uqk, uv, uo, ug, conv0, ct0, m0, cw, cb, gb, lng, lnb,
                  o_y, o_ct, o_m, ct, m_scr, carry, *, t_valid):
    tb = pl.program_id(1)

    @pl.when(tb == 0)
    def _():
        ct[...] = ct0[0]
        m_scr[...] = m0[0]
        carry[...] = conv0[0]

    L = uqk.shape[1]
    row = lax.broadcasted_iota(I32, (L, 1), 0)
    x = uqk[0]

    def shifted(j):
        out = pltpu.roll(x, j, 0)
        for i in range(j):
            out = jnp.where(row == i, carry[3 + i - j:4 + i - j, :], out)
        return out

    conv = x * cw[3:4, :] + shifted(1) * cw[2:3, :] + shifted(2) * cw[1:2, :] + shifted(3) * cw[0:1, :] + cb[...]
    carry[...] = x[L - 3:L, :]
    qk = conv * jax.nn.sigmoid(conv)
    q_all = qk[:, 0:B_QK_WIDTH]
    k_t = jnp.transpose(qk[:, B_QK_WIDTH:2 * B_QK_WIDTH] * (B_QK_DIM ** -0.5))

    lane = lax.broadcasted_iota(I32, (L, LANES), 1)
    gz = ug[0] + gb[...]
    lg = jnp.where(lane < B_HEADS, gz, jnp.minimum(gz, 0.0) - jnp.log(1.0 + jnp.exp(-jnp.abs(gz))))
    valid = (row + tb * L) < t_valid
    lg = jnp.where(valid, lg, jnp.where(lane < B_HEADS, -jnp.inf, 0.0))
    lg_t = jnp.transpose(lg)
    ti = lax.broadcasted_iota(I32, (L, L), 0)
    si = lax.broadcasted_iota(I32, (L, L), 1)
    tril = ti >= si
    tril_f = tril.astype(F32)
    triu_f = (ti <= si).astype(F32)
    lg_fin = jnp.where(lane < B_HEADS, 0.0, lg)
    bcum = jnp.dot(tril_f, lg_fin, preferred_element_type=F32, precision=lax.Precision.HIGHEST)
    bcum_t = jnp.dot(jnp.transpose(lg_fin), triu_f, preferred_element_type=F32,
                     precision=lax.Precision.HIGHEST)

    head_q = lax.shift_right_logical(lax.broadcasted_iota(I32, (L, B_QK_WIDTH), 1), 6)
    lane1 = lax.broadcasted_iota(I32, (1, LANES), 1)
    ones_col = (lax.broadcasted_iota(I32, (L, LANES), 1) == 0).astype(F32)
    m_vec = m_scr[...]
    m_new_vec = m_vec
    ct_old = ct[...]
    ct_old_bf = ct_old.astype(BF16)
    vv = uv[0]
    oo = uo[0]
    for h in range(B_HEADS):
        hs = slice(h * B_QK_DIM, (h + 1) * B_QK_DIM)
        vs = slice(h * B_V_DIM, (h + 1) * B_V_DIM)
        bc_col = bcum[:, B_HEADS + h:B_HEADS + h + 1]
        bc_row = bcum_t[B_HEADS + h:B_HEADS + h + 1, :]
        ic_row = lg_t[h:h + 1, :]
        gtot = bc_row[:, L - 1:L]
        m_prev = m_vec[:, h:h + 1]
        dmat = jnp.where(tril, bc_col - bc_row + ic_row, -jnp.inf)
        inter = bc_col + m_prev
        m_t = jnp.maximum(inter, jnp.max(dmat, axis=1, keepdims=True))
        w_intra = jnp.exp(dmat - m_t)
        w_inter = jnp.exp(inter - m_t)
        q_m = jnp.where(head_q == h, q_all, 0.0).astype(BF16)
        s = jnp.dot(q_m, k_t.astype(BF16), preferred_element_type=F32) * w_intra
        v_ext = jnp.concatenate([vv[:, vs], ones_col], axis=1).astype(BF16)
        intra = jnp.dot(s.astype(BF16), v_ext, preferred_element_type=F32)
        cross = jnp.dot(q_m, ct_old_bf, preferred_element_type=F32)
        num = intra[:, 0:B_V_DIM] + w_inter * cross[:, 0:B_V_DIM]
        den = jnp.sum(s, axis=1, keepdims=True) + w_inter * cross[:, B_V_DIM:B_V_DIM + 1]
        hh = num / jnp.maximum(jnp.abs(den), jnp.exp(-m_t))
        mu = jnp.mean(hh, axis=-1, keepdims=True)
        dd = hh - mu
        var = jnp.mean(dd * dd, axis=-1, keepdims=True)
        yn = dd * lax.rsqrt(var + LN_EPS) * lng[:, vs] + lnb[:, vs]
        o_y[0, :, vs] = jax.nn.sigmoid(oo[:, vs]) * yn
        lw = gtot - bc_row + ic_row
        m_new = jnp.maximum(gtot + m_prev, jnp.max(lw, axis=1, keepdims=True))
        w_s = jnp.exp(lw - m_new)
        dec = jnp.exp(gtot + m_prev - m_new)
        upd = jnp.dot((k_t[hs, :] * w_s).astype(BF16), v_ext, preferred_element_type=F32)
        ct[hs, :] = dec * ct_old[hs, :] + upd
        m_new_vec = jnp.where(lane1 == h, m_new, m_new_vec)
    m_scr[...] = m_new_vec

    @pl.when(tb == pl.num_programs(1) - 1)
    def _():
        o_ct[0] = ct[...]
        o_m[0] = m_scr[...]


def _mlstm(u, conv0, ct0, m0, p, L, t_valid):
    b, t, _ = u.shape
    blk = lambda w, c: pl.BlockSpec((1, L, w), lambda i, j, c=c: (i, j, c))
    full = lambda arr: pl.BlockSpec(arr.shape, lambda i, j: (0,) * arr.ndim)
    per_b = lambda arr: pl.BlockSpec((1,) + arr.shape[1:], lambda i, j: (i, 0, 0))
    params = [p['conv_w'], p['conv_b'], p['gate_b'], p['b_ln_g'], p['b_ln_b']]
    return pl.pallas_call(
        functools.partial(_mlstm_kernel, t_valid=t_valid),
        grid=(b, t // L),
        in_specs=[blk(1024, EV_QK // 1024), blk(1024, EV_VB // 1024), blk(1024, EV_O // 1024),
                  blk(LANES, EV_GATE // LANES), per_b(conv0), per_b(ct0), per_b(m0)] + [full(a) for a in params],
        out_specs=[pl.BlockSpec((1, L, B_V_WIDTH), lambda i, j: (i, j, 0)),
                   pl.BlockSpec((1, B_QK_WIDTH, 2 * LANES), lambda i, j: (i, 0, 0)),
                   pl.BlockSpec((1, 1, LANES), lambda i, j: (i, 0, 0))],
        out_shape=[jax.ShapeDtypeStruct((b, t, B_V_WIDTH), F32),
                   jax.ShapeDtypeStruct((b, B_QK_WIDTH, 2 * LANES), F32),
                   jax.ShapeDtypeStruct((b, 1, LANES), F32)],
        scratch_shapes=[pltpu.VMEM((B_QK_WIDTH, 2 * LANES), F32), pltpu.VMEM((1, LANES), F32),
                        pltpu.VMEM((3, 2 * B_QK_WIDTH), F32)],
        compiler_params=_cparams("parallel", "arbitrary"),
        name="mlstm",
    )(u, u, u, u, conv0, ct0, m0, *params)


def _even_out_kernel(y_ref, bonus_ref, g_ref, yb_ref, x_ref, lng, lnb, e_ref, woa, wob, g1, b1, o_ref):
    e = e_ref[...]
    y = y_ref[...]
    mu = _segsum(y, e) * (1.0 / A_HEAD_DIM)
    d = y - mu
    var = _segsum(d * d, e) * (1.0 / A_HEAD_DIM)
    yn = d * lax.rsqrt(var + LN_EPS) * lng[...] + lnb[...]
    ya = (yn + bonus_ref[...]) * g_ref[...]
    mix = (jnp.dot(ya.astype(BF16), woa[...], preferred_element_type=F32)
           + jnp.dot(yb_ref[...].astype(BF16), wob[...], preferred_element_type=F32))
    o_ref[...] = _layer_norm(ALPHA * x_ref[...] + mix, g1[...], b1[...])


def _even_out(y, bonus, g, yb, x, p, b, tm):
    m = x.shape[0]
    nt = m // b // tm
    tmaj = pl.BlockSpec((tm, D_MODEL), lambda i, j: (j, i))
    row = pl.BlockSpec((tm, D_MODEL), lambda i, j: (i * nt + j, 0))
    full = lambda arr: pl.BlockSpec(arr.shape, lambda i, j: (0,) * arr.ndim)
    params = [p['a_ln_g'], p['a_ln_b'], p['seg64'], p['w_out_a'], p['w_out_b'], p['ln1_g'], p['ln1_b']]
    return pl.pallas_call(
        _even_out_kernel,
        grid=(b, nt),
        in_specs=[tmaj] * 3 + [row] * 2 + [full(a) for a in params],
        out_specs=row,
        out_shape=jax.ShapeDtypeStruct((m, D_MODEL), F32),
        compiler_params=_cparams("parallel", "parallel"),
        name="even_out",
    )(y, bonus, g, yb, x, *params)


def _proj_ln_kernel(a_ref, x_ref, w_ref, g1, b1, o_ref):
    mix = jnp.dot(a_ref[...].astype(BF16), w_ref[...], preferred_element_type=F32)
    o_ref[...] = _layer_norm(ALPHA * x_ref[...] + mix, g1[...], b1[...])


def _proj_ln(a, x, w, g1, b1, tm):
    m = x.shape[0]
    row = pl.BlockSpec((tm, D_MODEL), lambda i: (i, 0))
    full = lambda arr: pl.BlockSpec(arr.shape, lambda i: (0,) * arr.ndim)
    return pl.pallas_call(
        _proj_ln_kernel,
        grid=(m // tm,),
        in_specs=[row, row, full(w), full(g1), full(b1)],
        out_specs=row,
        out_shape=jax.ShapeDtypeStruct((m, D_MODEL), F32),
        compiler_params=_cparams("parallel"),
        name="proj_ln",
    )(a, x, w, g1, b1)


def _experts_kernel(x_ref, wr, br, wg, wu, wd, g2, b2, o_ref, xb, comb, acc, *, routed):
    e = pl.program_id(1)

    @pl.when(e == 0)
    def _():
        x = x_ref[...]
        xb[...] = x.astype(BF16)
        acc[...] = jnp.zeros_like(acc)
        if routed:
            logits = jnp.dot(x, wr[...], preferred_element_type=F32, precision=lax.Precision.HIGHEST) + br[...]
            lane = lax.broadcasted_iota(I32, logits.shape, 1).astype(F32)
            v1 = jnp.max(logits, axis=-1, keepdims=True)
            i1 = jnp.min(jnp.where(logits == v1, lane, float(LANES)), axis=-1, keepdims=True)
            rest = jnp.where(lane == i1, -jnp.inf, logits)
            v2 = jnp.max(rest, axis=-1, keepdims=True)
            i2 = jnp.min(jnp.where(rest == v2, lane, float(LANES)), axis=-1, keepdims=True)
            e2 = jnp.exp(v2 - v1)
            den = 1.0 + e2
            comb[...] = jnp.where(lane == i1, 1.0 / den, jnp.where(lane == i2, e2 / den, 0.0))

    xbv = xb[...]
    hg = jnp.dot(xbv, wg[0], preferred_element_type=F32)
    hu = jnp.dot(xbv, wu[0], preferred_element_type=F32)
    hid = (hg * jax.nn.sigmoid(hg)) * hu
    y = jnp.dot(hid.astype(BF16), wd[0], preferred_element_type=F32)
    if routed:
        lane = lax.broadcasted_iota(I32, comb.shape, 1)
        ce = jnp.sum(jnp.where(lane == e, comb[...], 0.0), axis=-1, keepdims=True)
        y = ce * y
    acc[...] += y

    @pl.when(e == pl.num_programs(1) - 1)
    def _():
        o_ref[...] = _layer_norm(ALPHA * x_ref[...] + acc[...], g2[...], b2[...])


def _experts(x, wr, br, wg, wu, wd, g2, b2, tm, routed):
    m = x.shape[0]
    ne, _, dff = wg.shape
    row = pl.BlockSpec((tm, D_MODEL), lambda i, e: (i, 0))
    full = lambda arr: pl.BlockSpec(arr.shape, lambda i, e: (0,) * arr.ndim)
    return pl.pallas_call(
        functools.partial(_experts_kernel, routed=routed),
        grid=(m // tm, ne),
        in_specs=[row, full(wr), full(br),
                  pl.BlockSpec((1, D_MODEL, dff), lambda i, e: (e, 0, 0)),
                  pl.BlockSpec((1, D_MODEL, dff), lambda i, e: (e, 0, 0)),
                  pl.BlockSpec((1, dff, D_MODEL), lambda i, e: (e, 0, 0)),
                  full(g2), full(b2)],
        out_specs=row,
        out_shape=jax.ShapeDtypeStruct((m, D_MODEL), F32),
        scratch_shapes=[pltpu.VMEM((tm, D_MODEL), BF16), pltpu.VMEM((tm, LANES), F32),
                        pltpu.VMEM((tm, D_MODEL), F32)],
        compiler_params=_cparams("parallel", "arbitrary"),
        name="experts_routed" if routed else "experts_dense",
    )(x, wr, br, wg, wu, wd, g2, b2)


DSA_COARSE_BITS = 28


def _dsa_kernel(q_ref, qi_ref, wi_ref, kt_ref, v_ref, kit_ref, prev_ref, o_ref, key_scr, jcut_scr, thr_scr,
                *, top_k, causal, limit_const, blk_off):
    del prev_ref
    tq = q_ref.shape[1]
    s_len = kt_ref.shape[2]
    if causal:
        row_chunk = lax.shift_right_logical(lax.broadcasted_iota(I32, (tq, 1), 0), CHUNK.bit_length() - 1)
        chunk = (pl.program_id(1) + blk_off) * (tq // CHUNK) + row_chunk
        limit = (chunk + 1) * CHUNK
        kth = jnp.minimum(top_k, limit).astype(F32)
    else:
        limit = limit_const
        kth = jnp.full((tq, 1), min(top_k, limit_const), F32)
    idx = lax.broadcasted_iota(I32, (tq, s_len), 1)

    kit = kit_ref[0]
    wi = wi_ref[0] * (IDX_HEADS ** -0.5)
    score = jnp.zeros((tq, s_len), F32)
    for h in range(IDX_HEADS):
        rel = jnp.dot(qi_ref[0, h], kit, preferred_element_type=F32)
        score = score + wi[:, IDX_DIM + h:IDX_DIM + h + 1] * jnp.maximum(rel, 0.0)
    score = jnp.where(score == 0.0, 0.0, score)
    bits = pltpu.bitcast(score, I32)
    key = jnp.where(bits < 0, bits ^ 0x7FFFFFFF, bits)
    key_scr[...] = jnp.where(idx < limit, key, INT_MIN)

    n_grp = 2 if tq % (4 * SUBLANES) == 0 else 1
    rg = tq // n_grp
    rows_of = lambda g: slice(g * rg, (g + 1) * rg)

    def count_ge(thrs):
        return [jnp.sum(jnp.where(key_scr[rows_of(g), :] >= thrs[g], 1.0, 0.0), axis=1, keepdims=True)
                for g in range(n_grp)]

    def bit_step(i, thrs):
        bit = lax.shift_left(jnp.int32(1), 31 - i)
        cands = [t + bit for t in thrs]
        cnts = count_ge(cands)
        return tuple(jnp.where(cnts[g] >= kth[rows_of(g)], cands[g], thrs[g]) for g in range(n_grp))

    thrs = lax.fori_loop(0, DSA_COARSE_BITS, bit_step,
                         tuple(jnp.full((rg, 1), INT_MIN, I32) for _ in range(n_grp)), unroll=4)
    thr_scr[...] = jnp.concatenate(thrs, axis=0)
    n_coarse = jnp.concatenate(count_ge(thrs), axis=0)

    @pl.when(jnp.max(n_coarse - kth) > 0.0)
    def _():
        fine = lax.fori_loop(DSA_COARSE_BITS, 32, bit_step, tuple(thr_scr[rows_of(g), :] for g in range(n_grp)),
                             unroll=4)
        thr_scr[...] = jnp.concatenate(fine, axis=0)

    thr = thr_scr[...]
    keyv = key_scr[...]
    n_ge = jnp.sum(jnp.where(keyv >= thr, 1.0, 0.0), axis=1, keepdims=True)
    jcut_scr[...] = jnp.full((tq, 1), s_len, I32)

    @pl.when(jnp.max(n_ge - kth) > 0.0)
    def _():
        need = kth - jnp.sum(jnp.where(key_scr[...] > thr, 1.0, 0.0), axis=1, keepdims=True)
        n_bits = max(1, (s_len - 1).bit_length())

        def idx_step(i, j):
            cand = j + lax.shift_left(jnp.int32(1), n_bits - 1 - i)
            cnt = jnp.sum(jnp.where(key_scr[...] == thr, jnp.where(idx < cand, 1.0, 0.0), 0.0),
                          axis=1, keepdims=True)
            return jnp.where(cnt < need, cand, j)

        jcut_scr[...] = lax.fori_loop(0, n_bits, idx_step, jnp.zeros((tq, 1), I32))

    jcut = jcut_scr[...]
    bias = jnp.where(keyv > thr, 0.0,
                     jnp.where(keyv == thr, jnp.where(idx <= jcut, 0.0, -jnp.inf), -jnp.inf)).astype(BF16)

    q = q_ref[0] * (C_HEAD_DIM ** -0.5)
    for grp in range(C_KV_HEADS):
        kt = kt_ref[0, grp * C_HEAD_DIM:(grp + 1) * C_HEAD_DIM, :]
        v_ext = v_ref[0, :, grp * 2 * C_HEAD_DIM:(grp + 1) * 2 * C_HEAD_DIM]
        hsl = [slice((grp * C_GROUP + hg) * C_HEAD_DIM, (grp * C_GROUP + hg + 1) * C_HEAD_DIM)
               for hg in range(C_GROUP)]
        qg = jnp.concatenate([q[:, sl] for sl in hsl], axis=0).astype(BF16)
        lg = jnp.dot(qg, kt, preferred_element_type=F32).astype(BF16)
        ps = []
        for hg in range(C_GROUP):
            logits = lg[hg * tq:(hg + 1) * tq, :] + bias
            m_tile = functools.reduce(jnp.maximum, [logits[:, i * LANES:(i + 1) * LANES]
                                                    for i in range(s_len // LANES)])
            mx = jnp.max(m_tile.astype(F32), axis=1, keepdims=True).astype(BF16)
            ps.append(jnp.exp(logits - mx))
        out = jnp.dot(jnp.concatenate(ps, axis=0), v_ext, preferred_element_type=F32)
        for hg in range(C_GROUP):
            rows = slice(hg * tq, (hg + 1) * tq)
            o_ref[0, :, hsl[hg]] = out[rows, 0:C_HEAD_DIM] / out[rows, C_HEAD_DIM:C_HEAD_DIM + 1]


DSA_KEY_STEP = 512


def _dsa_call(u, qi_heads, kt, v, kit, prev, tq, top_k, causal, limit_const, blk_off, n_blk, s_len):
    b, t, _ = u.shape
    return pl.pallas_call(
        functools.partial(_dsa_kernel, top_k=top_k, causal=causal, limit_const=limit_const, blk_off=blk_off),
        grid=(b, n_blk),
        in_specs=[pl.BlockSpec((1, tq, C_WIDTH), lambda i, j: (i, j + blk_off, OD_Q // C_WIDTH)),
                  pl.BlockSpec((1, IDX_HEADS, tq, LANES), lambda i, j: (i, 0, j + blk_off, 0)),
                  pl.BlockSpec((1, tq, LANES), lambda i, j: (i, j + blk_off, OD_KI // LANES)),
                  pl.BlockSpec((1, 2 * C_HEAD_DIM, s_len), lambda i, j: (i, 0, 0)),
                  pl.BlockSpec((1, s_len, 2 * C_KV_HEADS * C_HEAD_DIM), lambda i, j: (i, 0, 0)),
                  pl.BlockSpec((1, LANES, s_len), lambda i, j: (i, 0, 0)),
                  pl.BlockSpec(memory_space=pl.ANY)],
        out_specs=pl.BlockSpec((1, tq, C_WIDTH), lambda i, j: (i, j + blk_off, 0)),
        out_shape=jax.ShapeDtypeStruct((b, t, C_WIDTH), F32),
        scratch_shapes=[pltpu.VMEM((tq, s_len), I32), pltpu.VMEM((tq, 1), I32), pltpu.VMEM((tq, 1), I32)],
        input_output_aliases={6: 0},
        compiler_params=_cparams("parallel", "arbitrary"),
        name="dsa",
    )(u, qi_heads, u, kt, v, kit, prev)


def _dsa(u, qi_heads, kt, v, kit, tq, top_k, causal, limit_const):
    b, t, _ = u.shape
    s_full = kt.shape[2]
    att = jnp.zeros((b, t, C_WIDTH), F32)
    if not causal or s_full % DSA_KEY_STEP:
        return _dsa_call(u, qi_heads, kt, v, kit, att, tq, top_k, causal, limit_const, 0, t // tq, s_full)
    per = DSA_KEY_STEP // tq
    for cls in range(s_full // DSA_KEY_STEP):
        att = _dsa_call(u, qi_heads, kt, v, kit, att, tq, top_k, causal, limit_const,
                        cls * per, per, (cls + 1) * DSA_KEY_STEP)
    return att


def _prep_even(w):
    win = w['w_in']
    a, bq = win[:, :A_COLS], win[:, A_COLS:]
    cols = [a[:, 0:3072], bq[:, 0:3072], a[:, 3072:A_COLS], bq[:, 3072:3088],
            jnp.zeros((D_MODEL, EV_COLS_PAD - EV_GATE - 2 * B_HEADS), F32)]
    p = {'w_in': jnp.concatenate(cols, axis=1).astype(BF16)}
    mu = w['mu']
    p['mu_main'] = mu[None, 0:3072]
    p['mu_lora'] = mu[None, 3072:A_COLS]
    row = lambda v: v.reshape(1, -1)
    p['w0'], p['a0'], p['k_k'], p['k_a'] = row(w['w0']), row(w['a0']), row(w['k_k']), row(w['k_a'])
    p['r_k'] = row(w['r_k'])
    z = lambda n: jnp.zeros((n, A_WIDTH), F32)
    p['w2p'] = jnp.concatenate([w['w2'], z(192)], axis=0).astype(BF16)
    p['a2p'] = jnp.concatenate([z(64), w['a2'], z(128)], axis=0).astype(BF16)
    p['g2p'] = jnp.concatenate([z(128), w['g2']], axis=0).astype(BF16)
    seg = jnp.arange(A_WIDTH) // A_HEAD_DIM
    p['seg64'] = (seg[:, None] == seg[None, :]).astype(BF16)
    p['a_ln_g'], p['a_ln_b'] = row(w['a_ln_g']), row(w['a_ln_b'])
    p['conv_w'], p['conv_b'] = w['conv_w'], row(w['conv_b'])
    p['gate_b'] = jnp.concatenate([w['b_i'], w['b_f'], jnp.zeros((LANES - 2 * B_HEADS,), F32)])[None]
    p['b_ln_g'], p['b_ln_b'] = row(w['b_ln_g']), row(w['b_ln_b'])
    p['w_out_a'] = w['w_out'][:A_WIDTH].astype(BF16)
    p['w_out_b'] = w['w_out'][A_WIDTH:].astype(BF16)
    p['ln1_g'], p['ln1_b'], p['ln2_g'], p['ln2_b'] = row(w['ln1_g']), row(w['ln1_b']), row(w['ln2_g']), row(w['ln2_b'])
    half = w['ffn_gate'].shape[1] // 2
    p['ffn_g'] = jnp.stack([w['ffn_gate'][:, :half], w['ffn_gate'][:, half:]]).astype(BF16)
    p['ffn_u'] = jnp.stack([w['ffn_up'][:, :half], w['ffn_up'][:, half:]]).astype(BF16)
    p['ffn_d'] = jnp.stack([w['ffn_down'][:half], w['ffn_down'][half:]]).astype(BF16)
    p['no_router_w'] = jnp.zeros((D_MODEL, LANES), F32)
    p['no_router_b'] = jnp.zeros((1, LANES), F32)
    return p


def _prep_odd(w):
    row = lambda v: v.reshape(1, -1)
    p = {'w_in': jnp.pad(w['w_in'], ((0, 0), (0, OD_COLS_PAD - w['w_in'].shape[1]))).astype(BF16)}
    p['w_out'] = w['w_out'].astype(BF16)
    p['ln1_g'], p['ln1_b'], p['ln2_g'], p['ln2_b'] = row(w['ln1_g']), row(w['ln1_b']), row(w['ln2_g']), row(w['ln2_b'])
    p['router_w'] = jnp.pad(w['router'], ((0, 0), (0, LANES - N_EXPERTS)))
    p['router_b'] = jnp.concatenate([w['router_b'], jnp.full((LANES - N_EXPERTS,), -1e30, F32)])[None]
    p['e_gate'], p['e_up'], p['e_down'] = w['e_gate'].astype(BF16), w['e_up'].astype(BF16), w['e_down'].astype(BF16)
    return p


def _pick(n, pref):
    for c in pref:
        if n % c == 0:
            return c
    return n


def _even_layer(x, shift0, wkv0, conv0, c0, n0, m0, p):
    b, t, _ = x.shape
    m = b * t
    tm = _pick(m, (512, 256, 128, 64, 32, 16, 8))
    u = _matmul(x.reshape(m, D_MODEL), p['w_in'], tm, EV_COLS_PAD // 3).reshape(b, t, EV_COLS_PAD)

    new_shift = jnp.concatenate([u[:, t - 1:, 0:3072], u[:, t - 1:, EV_LORA:EV_LORA + A_LORA]], axis=-1)
    new_conv = jnp.concatenate([conv0, u[:, :, EV_QK:EV_QK + 1024]], axis=1)[:, t:]

    s0_main, s0_lora = shift0[:, :, 0:3072], shift0[:, :, 3072:A_COLS]
    tp = _pick(t, (256, 128, 64, 32, 16, 8))
    r, w, k, v, a, bb, g, bonus = _rwkv_pre(u, s0_main, s0_lora, p, tp)
    rows = b * A_HEADS // 2
    assert rows % LANES == 0, "batch must be a multiple of 16"
    to_rows = lambda z: z.reshape(t, rows, LANES)
    s0 = wkv0.reshape(b, A_HEADS // 2, 2, A_HEAD_DIM, A_HEAD_DIM).transpose(2, 4, 3, 0, 1)
    s0 = s0.reshape(2, A_HEAD_DIM, A_HEAD_DIM, rows)
    tt = _pick(t, (16, 8))
    y, s_fin = _wkv_scan(*(to_rows(z) for z in (r, w, k, v, a, bb)), s0, tt)
    y = y.reshape(t, b * A_WIDTH)
    new_wkv = s_fin.reshape(2, A_HEAD_DIM, A_HEAD_DIM, b, A_HEADS // 2).transpose(3, 4, 0, 2, 1)
    new_wkv = new_wkv.reshape(b, A_HEADS, A_HEAD_DIM, A_HEAD_DIM)

    ct0 = jnp.concatenate([c0.transpose(0, 1, 3, 2).reshape(b, B_QK_WIDTH, B_V_DIM),
                           n0.reshape(b, B_QK_WIDTH, 1),
                           jnp.zeros((b, B_QK_WIDTH, LANES - 1), F32)], axis=-1)
    m0p = jnp.pad(m0, ((0, 0), (0, LANES - B_HEADS)))[:, None, :]
    lm = _pick(t, (256, 128))
    if t % LANES:
        t_pad = -t % LANES
        u_b = jnp.pad(u, ((0, 0), (0, t_pad), (0, 0)))
        lm = LANES
    else:
        u_b = u
    yb, ct, m_out = _mlstm(u_b, conv0, ct0, m0p, p, lm, t)
    yb = yb[:, :t].reshape(m, B_V_WIDTH)
    new_c = ct[:, :, 0:B_V_DIM].reshape(b, B_HEADS, B_QK_DIM, B_V_DIM).transpose(0, 1, 3, 2)
    new_n = ct[:, :, B_V_DIM].reshape(b, B_HEADS, B_QK_DIM)
    new_m = m_out[:, 0, 0:B_HEADS]

    x2 = x.reshape(m, D_MODEL)
    x2 = _even_out(y, bonus, g, yb, x2, p, b, _pick(t, (256, 128, 64, 32, 16, 8)))
    x2 = _experts(x2, p['no_router_w'], p['no_router_b'], p['ffn_g'], p['ffn_u'], p['ffn_d'],
                  p['ln2_g'], p['ln2_b'], tm, routed=False)
    return x2.reshape(b, t, D_MODEL), (new_shift, new_wkv, new_conv, new_c, new_n, new_m)


def _odd_layer(x, past_k, past_v, past_ki, p):
    b, t, _ = x.shape
    m = b * t
    tm = _pick(m, (512, 256, 128, 64, 32, 16, 8))
    u = _matmul(x.reshape(m, D_MODEL), p['w_in'], tm, OD_COLS_PAD).reshape(b, t, OD_COLS_PAD)
    k_new = u[:, :, OD_K:OD_K + 256]
    v_new = u[:, :, OD_V:OD_V + 256]
    ki_new = u[:, :, OD_KI:OD_KI + IDX_DIM]
    if past_k is None:
        keys_k, keys_v, keys_i = k_new, v_new, ki_new
        causal, limit, tq = True, 0, _pick(t, (2 * CHUNK, CHUNK))
        top_k = min(TOPK_MAX, t // 4)
    else:
        keys_k = jnp.concatenate([past_k.reshape(b, -1, 256), k_new], axis=1)
        keys_v = jnp.concatenate([past_v.reshape(b, -1, 256), v_new], axis=1)
        keys_i = jnp.concatenate([past_ki, ki_new], axis=1)
        limit = keys_k.shape[1]
        causal, tq = False, t
        top_k = min(TOPK_MAX, limit // 4)
    s_pad = -keys_k.shape[1] % LANES
    pad_s = lambda z: jnp.pad(z, ((0, 0), (0, s_pad), (0, 0))) if s_pad else z
    kt = pad_s(keys_k).transpose(0, 2, 1).astype(BF16)
    vv = pad_s(keys_v).astype(BF16).reshape(b, -1, C_KV_HEADS, C_HEAD_DIM)
    vv = jnp.concatenate([vv, jnp.ones_like(vv)], axis=-1).reshape(b, -1, 2 * C_KV_HEADS * C_HEAD_DIM)
    kit = jnp.pad(pad_s(keys_i), ((0, 0), (0, 0), (0, LANES - IDX_DIM))).transpose(0, 2, 1).astype(BF16)
    qi = u[:, :, OD_QI:OD_QI + IDX_HEADS * IDX_DIM].reshape(b, t, IDX_HEADS, IDX_DIM).transpose(0, 2, 1, 3)
    qi = jnp.pad(qi, ((0, 0), (0, 0), (0, 0), (0, LANES - IDX_DIM))).astype(BF16)
    att = _dsa(u, qi, kt, vv, kit, tq, top_k, causal, limit)
    x2 = _proj_ln(att.reshape(m, C_WIDTH), x.reshape(m, D_MODEL), p['w_out'], p['ln1_g'], p['ln1_b'],
                  _pick(m, (512, 256, 128, 64, 32, 16, 8)))
    x2 = _experts(x2, p['router_w'], p['router_b'], p['e_gate'], p['e_up'], p['e_down'],
                  p['ln2_g'], p['ln2_b'], tm, routed=True)
    st = (k_new.reshape(b, t, C_KV_HEADS, C_HEAD_DIM), v_new.reshape(b, t, C_KV_HEADS, C_HEAD_DIM), ki_new)
    return x2.reshape(b, t, D_MODEL), st


def _trunk(x, shift0, wkv0, conv0, c0, n0, m0, past_k, past_v, past_ki, ev, od):
    even_out = [[] for _ in range(6)]
    odd_out = [[] for _ in range(3)]
    for layer in range(DEPTH):
        i = layer // 2
        if layer % 2 == 0:
            x, st = _even_layer(x, shift0[i], wkv0[i], conv0[i], c0[i], n0[i], m0[i], ev[i])
            for acc, val in zip(even_out, st):
                acc.append(val)
        else:
            if past_k is None:
                x, st = _odd_layer(x, None, None, None, od[i])
            else:
                x, st = _odd_layer(x, past_k[i], past_v[i], past_ki[i], od[i])
            for acc, val in zip(odd_out, st):
                acc.append(val)
    return x, [jnp.stack(v) for v in even_out], [jnp.stack(v) for v in odd_out]


def kernel(x_prompt, x_sample, state_shift, state_wkv, state_conv, state_c, state_n, state_m,
           cache_k, cache_v, cache_idx_k,
           ev_w_in, a_mu, a_w0, a_w2, a_a0, a_a2, a_g2, a_k_k, a_k_a, a_r_k, a_ln_g, a_ln_b,
           b_conv_w, b_conv_b, b_i_bias, b_f_bias, b_ln_g, b_ln_b, ev_w_out, ev_ln1_g, ev_ln1_b,
           ffn_w_gate, ffn_w_up, ffn_w_down, ev_ln2_g, ev_ln2_b,
           od_w_in, od_w_out, od_ln1_g, od_ln1_b, moe_w_router, moe_b_router,
           moe_w_gate, moe_w_up, moe_w_down, od_ln2_g, od_ln2_b):
    ew = {'w_in': ev_w_in, 'mu': a_mu, 'w0': a_w0, 'w2': a_w2, 'a0': a_a0, 'a2': a_a2, 'g2': a_g2,
          'k_k': a_k_k, 'k_a': a_k_a, 'r_k': a_r_k, 'a_ln_g': a_ln_g, 'a_ln_b': a_ln_b,
          'conv_w': b_conv_w, 'conv_b': b_conv_b, 'b_i': b_i_bias, 'b_f': b_f_bias,
          'b_ln_g': b_ln_g, 'b_ln_b': b_ln_b, 'w_out': ev_w_out, 'ln1_g': ev_ln1_g, 'ln1_b': ev_ln1_b,
          'ffn_gate': ffn_w_gate, 'ffn_up': ffn_w_up, 'ffn_down': ffn_w_down,
          'ln2_g': ev_ln2_g, 'ln2_b': ev_ln2_b}
    ow = {'w_in': od_w_in, 'w_out': od_w_out, 'ln1_g': od_ln1_g, 'ln1_b': od_ln1_b,
          'router': moe_w_router, 'router_b': moe_b_router, 'e_gate': moe_w_gate, 'e_up': moe_w_up,
          'e_down': moe_w_down, 'ln2_g': od_ln2_g, 'ln2_b': od_ln2_b}
    n_even, n_odd = ev_w_in.shape[0], od_w_in.shape[0]
    ev = [_prep_even({k: v[i] for k, v in ew.items()}) for i in range(n_even)]
    od = [_prep_odd({k: v[i] for k, v in ow.items()}) for i in range(n_odd)]

    bp = x_prompt.shape[0]
    z = functools.partial(jnp.zeros, dtype=F32)
    y_p, ep, op = _trunk(x_prompt,
                         z((n_even, bp, 1, A_COLS)), z((n_even, bp, A_HEADS, A_HEAD_DIM, A_HEAD_DIM)),
                         z((n_even, bp, 3, 2 * B_QK_WIDTH)), z((n_even, bp, B_HEADS, B_V_DIM, B_QK_DIM)),
                         z((n_even, bp, B_HEADS, B_QK_DIM)), z((n_even, bp, B_HEADS)),
                         None, None, None, ev, od)
    y_s, es, os_ = _trunk(x_sample, state_shift, state_wkv, state_conv, state_c, state_n, state_m,
                          cache_k, cache_v, cache_idx_k, ev, od)
    return (y_p, y_s,
            ep[0], es[0], ep[1], es[1], ep[2], es[2], ep[3], es[3], ep[4], es[4], ep[5], es[5],
            op[0], os_[0], op[1], os_[1], op[2], os_[2])
```

```python
import functools
import math

import jax
import jax.numpy as jnp
from jax import lax
from jax.experimental import pallas as pl
from jax.experimental.pallas import tpu as pltpu

F32 = jnp.float32
BF16 = jnp.bfloat16
I32 = jnp.int32

D_MODEL = 1024
DEPTH = 4
CHUNK = 64
A_HEADS, A_HEAD_DIM, A_WIDTH = 16, 64, 1024
A_LORA = 256
A_COLS = 3 * A_WIDTH + A_LORA
B_HEADS, B_QK_DIM, B_V_DIM = 8, 64, 128
B_QK_WIDTH, B_V_WIDTH = 512, 1024
C_HEADS, C_KV_HEADS, C_HEAD_DIM, C_GROUP = 8, 2, 128, 4
C_WIDTH = 1024
IDX_HEADS, IDX_DIM = 8, 64
TOPK_MAX = 256
N_EXPERTS = 8
D_FF_EXPERT = 1408
ALPHA = (2 * DEPTH) ** 0.25
LN_EPS = 1e-5

LANES = 128
SUBLANES = 8
VMEM_LIMIT = 56 * 1024 * 1024

EV_R, EV_K, EV_V, EV_QK, EV_VB, EV_O, EV_LORA, EV_GATE = 0, 1024, 2048, 3072, 4096, 5120, 6144, 6400
EV_COLS_PAD = 6528
OD_Q, OD_K, OD_V, OD_QI, OD_KI = 0, 1024, 1280, 1536, 2048
OD_COLS_PAD = 2176

INT_MIN = -2 ** 31


def _cparams(*sem):
    return pltpu.CompilerParams(dimension_semantics=sem, vmem_limit_bytes=VMEM_LIMIT)


def _layer_norm(z, g, b):
    mu = jnp.mean(z, axis=-1, keepdims=True)
    d = z - mu
    var = jnp.mean(d * d, axis=-1, keepdims=True)
    return d * lax.rsqrt(var + LN_EPS) * g + b


def _segsum(x, e):
    hi = x.astype(BF16)
    lo = (x - hi.astype(F32)).astype(BF16)
    return jnp.dot(hi, e, preferred_element_type=F32) + jnp.dot(lo, e, preferred_element_type=F32)


def _mm_kernel(x_ref, w_ref, o_ref):
    o_ref[...] = jnp.dot(x_ref[...].astype(BF16), w_ref[...], preferred_element_type=F32)


def _matmul(x, w, tm, tn):
    m, k = x.shape
    n = w.shape[1]
    return pl.pallas_call(
        _mm_kernel,
        grid=(n // tn, m // tm),
        in_specs=[pl.BlockSpec((tm, k), lambda j, i: (i, 0)),
                  pl.BlockSpec((k, tn), lambda j, i: (0, j))],
        out_specs=pl.BlockSpec((tm, tn), lambda j, i: (i, j)),
        out_shape=jax.ShapeDtypeStruct((m, n), F32),
        compiler_params=_cparams("parallel", "parallel"),
        name="proj_in",
    )(x, w)


def _rwkv_pre_kernel(ur, uk, uv, ul, s0m, s0l, mum, mul_, w0, a0, kk_, ka_, rk_, w2, a2, g2, e_ref,
                     o_r, o_w, o_k, o_v, o_a, o_b, o_g, o_bonus, prev_m, prev_l):
    @pl.when(pl.program_id(1) == 0)
    def _():
        prev_m[...] = s0m[0]
        prev_l[...] = s0l[0]

    tm = ur.shape[1]
    row = lax.broadcasted_iota(I32, (tm, 1), 0)

    def lerp(x, prev, mu):
        shifted = jnp.where(row == 0, prev, pltpu.roll(x, 1, 0))
        return x + (shifted - x) * mu

    r_raw, k_raw, v_raw, l_raw = ur[0], uk[0], uv[0], ul[0]
    r = lerp(r_raw, prev_m[:, 0:1024], mum[:, 0:1024])
    k = lerp(k_raw, prev_m[:, 1024:2048], mum[:, 1024:2048])
    v = lerp(v_raw, prev_m[:, 2048:3072], mum[:, 2048:3072])
    lo = lerp(l_raw, prev_l[...], mul_[...])
    prev_m[:, 0:1024] = r_raw[tm - 1:tm, :]
    prev_m[:, 1024:2048] = k_raw[tm - 1:tm, :]
    prev_m[:, 2048:3072] = v_raw[tm - 1:tm, :]
    prev_l[...] = l_raw[tm - 1:tm, :]

    zw = w0[...] + jnp.dot(jnp.tanh(lo).astype(BF16), w2[...], preferred_element_type=F32)
    decay = jnp.exp(-(math.exp(-0.5) * jax.nn.sigmoid(zw)))
    a = jax.nn.sigmoid(a0[...] + jnp.dot(lo.astype(BF16), a2[...], preferred_element_type=F32))
    g = jnp.dot(jax.nn.sigmoid(lo).astype(BF16), g2[...], preferred_element_type=F32)

    e = e_ref[...]
    kk = k * kk_[...]
    kk = kk * lax.rsqrt(jnp.maximum(_segsum(kk * kk, e), 1e-24))
    k2 = k * (1.0 + (a - 1.0) * ka_[...])
    bonus = _segsum(r * k2 * rk_[...], e) * v

    o_r[...] = r
    o_w[...] = decay
    o_k[...] = k2
    o_v[...] = v
    o_a[...] = -kk
    o_b[...] = kk * a
    o_g[...] = g
    o_bonus[...] = bonus


def _rwkv_pre(u, s0_main, s0_lora, p, tm):
    b, t, _ = u.shape
    blk = lambda w, c: pl.BlockSpec((1, tm, w), lambda i, j, c=c: (i, j, c))
    full = lambda arr: pl.BlockSpec(arr.shape, lambda i, j: (0,) * arr.ndim)
    params = [p['mu_main'], p['mu_lora'], p['w0'], p['a0'], p['k_k'], p['k_a'], p['r_k'],
              p['w2p'], p['a2p'], p['g2p'], p['seg64']]
    out = jax.ShapeDtypeStruct((t, b * A_WIDTH), F32)
    return pl.pallas_call(
        _rwkv_pre_kernel,
        grid=(b, t // tm),
        in_specs=[blk(1024, EV_R // 1024), blk(1024, EV_K // 1024), blk(1024, EV_V // 1024),
                  blk(A_LORA, EV_LORA // A_LORA),
                  pl.BlockSpec((1, 1, 3072), lambda i, j: (i, 0, 0)),
                  pl.BlockSpec((1, 1, A_LORA), lambda i, j: (i, 0, 0))] + [full(a) for a in params],
        out_specs=[pl.BlockSpec((tm, A_WIDTH), lambda i, j: (j, i))] * 8,
        out_shape=[out] * 8,
        scratch_shapes=[pltpu.VMEM((1, 3072), F32), pltpu.VMEM((1, A_LORA), F32)],
        compiler_params=_cparams("parallel", "arbitrary"),
        name="rwkv_pre",
    )(u, u, u, u, s0_main, s0_lora, *params)


WKV_SRC_R, WKV_SRC_W, WKV_SRC_K, WKV_SRC_V, WKV_SRC_A, WKV_SRC_B = range(6)
WKV_VGROUP = 4
WKV_KCHUNK = 16


def _wkv_kernel(r_ref, w_ref, k_ref, v_ref, a_ref, b_ref, s0_ref, y_ref, s_ref, tr, sa_buf, ybuf):
    @pl.when(pl.program_id(1) == 0)
    def _():
        s_ref[...] = s0_ref[...]

    n_t = r_ref.shape[0]
    d = A_HEAD_DIM
    srcs = (r_ref, w_ref, k_ref, v_ref, a_ref, b_ref)
    vspan = WKV_VGROUP * SUBLANES
    groups = [(p, g) for p in range(2) for g in range(d // vspan)]

    def vrows(g, j):
        return slice(g * vspan + j * SUBLANES, g * vspan + (j + 1) * SUBLANES)

    def transpose_in(t, carry):
        for i, src in enumerate(srcs):
            tr[i, t] = jnp.transpose(src[t])
        return carry

    lax.fori_loop(0, n_t, transpose_in, 0)

    for p, g in groups:
        acc = [jnp.zeros((SUBLANES, LANES), F32) for _ in range(WKV_VGROUP)]
        for kx in range(d):
            ab = tr[WKV_SRC_A, 0, p * d + kx:p * d + kx + 1, :]
            for j in range(WKV_VGROUP):
                acc[j] = acc[j] + s_ref[p, kx, vrows(g, j), :] * ab
        for j in range(WKV_VGROUP):
            sa_buf[p, vrows(g, j), :] = acc[j]

    def step(t, carry):
        tn = jnp.minimum(t + 1, n_t - 1)
        for p, g in groups:
            vv = [tr[WKV_SRC_V, t, p * d + g * vspan + j * SUBLANES:p * d + g * vspan + (j + 1) * SUBLANES, :]
                  for j in range(WKV_VGROUP)]
            sa = [sa_buf[p, vrows(g, j), :] for j in range(WKV_VGROUP)]
            zero = tuple(jnp.zeros((SUBLANES, LANES), F32) for _ in range(2 * WKV_VGROUP))

            def key_chunk(kc, accs, p=p, g=g, vv=vv, sa=sa):
                yacc, san = list(accs[:WKV_VGROUP]), list(accs[WKV_VGROUP:])
                for i in range(WKV_KCHUNK):
                    kx = kc * WKV_KCHUNK + i
                    row = pl.ds(p * d + kx, 1)
                    wb = tr[WKV_SRC_W, t, row, :]
                    bb = tr[WKV_SRC_B, t, row, :]
                    kb = tr[WKV_SRC_K, t, row, :]
                    rb = tr[WKV_SRC_R, t, row, :]
                    an = tr[WKV_SRC_A, tn, row, :]
                    for j in range(WKV_VGROUP):
                        n = s_ref[p, kx, vrows(g, j), :] * wb + sa[j] * bb + vv[j] * kb
                        s_ref[p, kx, vrows(g, j), :] = n
                        yacc[j] = yacc[j] + n * rb
                        san[j] = san[j] + n * an
                return tuple(yacc) + tuple(san)

            accs = lax.fori_loop(0, d // WKV_KCHUNK, key_chunk, zero)
            yacc, san = accs[:WKV_VGROUP], accs[WKV_VGROUP:]
            for j in range(WKV_VGROUP):
                ybuf[t, p * d + g * vspan + j * SUBLANES:p * d + g * vspan + (j + 1) * SUBLANES, :] = yacc[j]
                sa_buf[p, vrows(g, j), :] = san[j]
        return carry

    lax.fori_loop(0, n_t, step, 0)

    def transpose_out(t, carry):
        y_ref[t] = jnp.transpose(ybuf[t])
        return carry

    lax.fori_loop(0, n_t, transpose_out, 0, unroll=4)


def _wkv_scan(r, w, k, v, a, b, s0, tt):
    t, rows, _ = r.shape
    d = A_HEAD_DIM
    seq = pl.BlockSpec((tt, LANES, LANES), lambda g, i: (i, g, 0))
    st = pl.BlockSpec((2, d, d, LANES), lambda g, i: (0, 0, 0, g))
    return pl.pallas_call(
        _wkv_kernel,
        grid=(rows // LANES, t // tt),
        in_specs=[seq] * 6 + [st],
        out_specs=[seq, st],
        out_shape=[jax.ShapeDtypeStruct((t, rows, LANES), F32), jax.ShapeDtypeStruct((2, d, d, rows), F32)],
        scratch_shapes=[pltpu.VMEM((6, tt, LANES, LANES), F32), pltpu.VMEM((2, d, LANES), F32),
                        pltpu.VMEM((tt, LANES, LANES), F32)],
        compiler_params=_cparams("parallel", "arbitrary"),
        name="wkv_scan",
    )(r, w, k, v, a, b, s0)


def _mlstm_kernel(uqk, uv, uo, ug, conv0, ct0, m0, cw, cb, gb, lng, lnb,
                  o_y, o_ct, o_m, ct, m_scr, carry, *, t_valid):
    tb = pl.program_id(1)

    @pl.when(tb == 0)
    def _():
        ct[...] = ct0[0]
        m_scr[...] = m0[0]
        carry[...] = conv0[0]

    L = uqk.shape[1]
    row = lax.broadcasted_iota(I32, (L, 1), 0)
    x = uqk[0]

    def shifted(j):
        out = pltpu.roll(x, j, 0)
        for i in range(j):
            out = jnp.where(row == i, carry[3 + i - j:4 + i - j, :], out)
        return out

    conv = x * cw[3:4, :] + shifted(1) * cw[2:3, :] + shifted(2) * cw[1:2, :] + shifted(3) * cw[0:1, :] + cb[...]
    carry[...] = x[L - 3:L, :]
    qk = conv * jax.nn.sigmoid(conv)
    q_all = qk[:, 0:B_QK_WIDTH]
    k_t = jnp.transpose(qk[:, B_QK_WIDTH:2 * B_QK_WIDTH] * (B_QK_DIM ** -0.5))

    lane = lax.broadcasted_iota(I32, (L, LANES), 1)
    gz = ug[0] + gb[...]
    lg = jnp.where(lane < B_HEADS, gz, jnp.minimum(gz, 0.0) - jnp.log(1.0 + jnp.exp(-jnp.abs(gz))))
    valid = (row + tb * L) < t_valid
    lg = jnp.where(valid, lg, jnp.where(lane < B_HEADS, -jnp.inf, 0.0))
    lg_t = jnp.transpose(lg)
    ti = lax.broadcasted_iota(I32, (L, L), 0)
    si = lax.broadcasted_iota(I32, (L, L), 1)
    tril = ti >= si
    tril_f = tril.astype(F32)
    triu_f = (ti <= si).astype(F32)
    lg_fin = jnp.where(lane < B_HEADS, 0.0, lg)
    bcum = jnp.dot(tril_f, lg_fin, preferred_element_type=F32, precision=lax.Precision.HIGHEST)
    bcum_t = jnp.dot(jnp.transpose(lg_fin), triu_f, preferred_element_type=F32,
                     precision=lax.Precision.HIGHEST)

    head_q = lax.shift_right_logical(lax.broadcasted_iota(I32, (L, B_QK_WIDTH), 1), 6)
    lane1 = lax.broadcasted_iota(I32, (1, LANES), 1)
    ones_col = (lax.broadcasted_iota(I32, (L, LANES), 1) == 0).astype(F32)
    m_vec = m_scr[...]
    m_new_vec = m_vec
    ct_old = ct[...]
    ct_old_bf = ct_old.astype(BF16)
    vv = uv[0]
    oo = uo[0]
    for h in range(B_HEADS):
        hs = slice(h * B_QK_DIM, (h + 1) * B_QK_DIM)
        vs = slice(h * B_V_DIM, (h + 1) * B_V_DIM)
        bc_col = bcum[:, B_HEADS + h:B_HEADS + h + 1]
        bc_row = bcum_t[B_HEADS + h:B_HEADS + h + 1, :]
        ic_row = lg_t[h:h + 1, :]
        gtot = bc_row[:, L - 1:L]
        m_prev = m_vec[:, h:h + 1]
        dmat = jnp.where(tril, bc_col - bc_row + ic_row, -jnp.inf)
        inter = bc_col + m_prev
        m_t = jnp.maximum(inter, jnp.max(dmat, axis=1, keepdims=True))
        w_intra = jnp.exp(dmat - m_t)
        w_inter = jnp.exp(inter - m_t)
        q_m = jnp.where(head_q == h, q_all, 0.0).astype(BF16)
        s = jnp.dot(q_m, k_t.astype(BF16), preferred_element_type=F32) * w_intra
        v_ext = jnp.concatenate([vv[:, vs], ones_col], axis=1).astype(BF16)
        intra = jnp.dot(s.astype(BF16), v_ext, preferred_element_type=F32)
        cross = jnp.dot(q_m, ct_old_bf, preferred_element_type=F32)
        num = intra[:, 0:B_V_DIM] + w_inter * cross[:, 0:B_V_DIM]
        den = jnp.sum(s, axis=1, keepdims=True) + w_inter * cross[:, B_V_DIM:B_V_DIM + 1]
        hh = num / jnp.maximum(jnp.abs(den), jnp.exp(-m_t))
        mu = jnp.mean(hh, axis=-1, keepdims=True)
        dd = hh - mu
        var = jnp.mean(dd * dd, axis=-1, keepdims=True)
        yn = dd * lax.rsqrt(var + LN_EPS) * lng[:, vs] + lnb[:, vs]
        o_y[0, :, vs] = jax.nn.sigmoid(oo[:, vs]) * yn
        lw = gtot - bc_row + ic_row
        m_new = jnp.maximum(gtot + m_prev, jnp.max(lw, axis=1, keepdims=True))
        w_s = jnp.exp(lw - m_new)
        dec = jnp.exp(gtot + m_prev - m_new)
        upd = jnp.dot((k_t[hs, :] * w_s).astype(BF16), v_ext, preferred_element_type=F32)
        ct[hs, :] = dec * ct_old[hs, :] + upd
        m_new_vec = jnp.where(lane1 == h, m_new, m_new_vec)
    m_scr[...] = m_new_vec

    @pl.when(tb == pl.num_programs(1) - 1)
    def _():
        o_ct[0] = ct[...]
        o_m[0] = m_scr[...]


def _mlstm(u, conv0, ct0, m0, p, L, t_valid):
    b, t, _ = u.shape
    blk = lambda w, c: pl.BlockSpec((1, L, w), lambda i, j, c=c: (i, j, c))
    full = lambda arr: pl.BlockSpec(arr.shape, lambda i, j: (0,) * arr.ndim)
    per_b = lambda arr: pl.BlockSpec((1,) + arr.shape[1:], lambda i, j: (i, 0, 0))
    params = [p['conv_w'], p['conv_b'], p['gate_b'], p['b_ln_g'], p['b_ln_b']]
    return pl.pallas_call(
        functools.partial(_mlstm_kernel, t_valid=t_valid),
        grid=(b, t // L),
        in_specs=[blk(1024, EV_QK // 1024), blk(1024, EV_VB // 1024), blk(1024, EV_O // 1024),
                  blk(LANES, EV_GATE // LANES), per_b(conv0), per_b(ct0), per_b(m0)] + [full(a) for a in params],
        out_specs=[pl.BlockSpec((1, L, B_V_WIDTH), lambda i, j: (i, j, 0)),
                   pl.BlockSpec((1, B_QK_WIDTH, 2 * LANES), lambda i, j: (i, 0, 0)),
                   pl.BlockSpec((1, 1, LANES), lambda i, j: (i, 0, 0))],
        out_shape=[jax.ShapeDtypeStruct((b, t, B_V_WIDTH), F32),
                   jax.ShapeDtypeStruct((b, B_QK_WIDTH, 2 * LANES), F32),
                   jax.ShapeDtypeStruct((b, 1, LANES), F32)],
        scratch_shapes=[pltpu.VMEM((B_QK_WIDTH, 2 * LANES), F32), pltpu.VMEM((1, LANES), F32),
                        pltpu.VMEM((3, 2 * B_QK_WIDTH), F32)],
        compiler_params=_cparams("parallel", "arbitrary"),
        name="mlstm",
    )(u, u, u, u, conv0, ct0, m0, *params)


def _even_out_kernel(y_ref, bonus_ref, g_ref, yb_ref, x_ref, lng, lnb, e_ref, woa, wob, g1, b1, o_ref):
    e = e_ref[...]
    y = y_ref[...]
    mu = _segsum(y, e) * (1.0 / A_HEAD_DIM)
    d = y - mu
    var = _segsum(d * d, e) * (1.0 / A_HEAD_DIM)
    yn = d * lax.rsqrt(var + LN_EPS) * lng[...] + lnb[...]
    ya = (yn + bonus_ref[...]) * g_ref[...]
    mix = (jnp.dot(ya.astype(BF16), woa[...], preferred_element_type=F32)
           + jnp.dot(yb_ref[...].astype(BF16), wob[...], preferred_element_type=F32))
    o_ref[...] = _layer_norm(ALPHA * x_ref[...] + mix, g1[...], b1[...])


def _even_out(y, bonus, g, yb, x, p, b, tm):
    m = x.shape[0]
    nt = m // b // tm
    tmaj = pl.BlockSpec((tm, D_MODEL), lambda i, j: (j, i))
    row = pl.BlockSpec((tm, D_MODEL), lambda i, j: (i * nt + j, 0))
    full = lambda arr: pl.BlockSpec(arr.shape, lambda i, j: (0,) * arr.ndim)
    params = [p['a_ln_g'], p['a_ln_b'], p['seg64'], p['w_out_a'], p['w_out_b'], p['ln1_g'], p['ln1_b']]
    return pl.pallas_call(
        _even_out_kernel,
        grid=(b, nt),
        in_specs=[tmaj] * 3 + [row] * 2 + [full(a) for a in params],
        out_specs=row,
        out_shape=jax.ShapeDtypeStruct((m, D_MODEL), F32),
        compiler_params=_cparams("parallel", "parallel"),
        name="even_out",
    )(y, bonus, g, yb, x, *params)


def _proj_ln_kernel(a_ref, x_ref, w_ref, g1, b1, o_ref):
    mix = jnp.dot(a_ref[...].astype(BF16), w_ref[...], preferred_element_type=F32)
    o_ref[...] = _layer_norm(ALPHA * x_ref[...] + mix, g1[...], b1[...])


def _proj_ln(a, x, w, g1, b1, tm):
    m = x.shape[0]
    row = pl.BlockSpec((tm, D_MODEL), lambda i: (i, 0))
    full = lambda arr: pl.BlockSpec(arr.shape, lambda i: (0,) * arr.ndim)
    return pl.pallas_call(
        _proj_ln_kernel,
        grid=(m // tm,),
        in_specs=[row, row, full(w), full(g1), full(b1)],
        out_specs=row,
        out_shape=jax.ShapeDtypeStruct((m, D_MODEL), F32),
        compiler_params=_cparams("parallel"),
        name="proj_ln",
    )(a, x, w, g1, b1)


def _swiglu(xb, wg, wu, wd):
    hg = jnp.dot(xb, wg, preferred_element_type=F32)
    hu = jnp.dot(xb, wu, preferred_element_type=F32)
    hid = (hg * jax.nn.sigmoid(hg)) * hu
    return jnp.dot(hid.astype(BF16), wd, preferred_element_type=F32)


def _ffn_kernel(x_ref, wg, wu, wd, g2, b2, o_ref, xb):
    e = pl.program_id(1)

    @pl.when(e == 0)
    def _():
        xb[...] = x_ref[...].astype(BF16)
        o_ref[...] = jnp.zeros_like(o_ref)

    o_ref[...] += _swiglu(xb[...], wg[0], wu[0], wd[0])

    @pl.when(e == pl.num_programs(1) - 1)
    def _():
        o_ref[...] = _layer_norm(ALPHA * x_ref[...] + o_ref[...], g2[...], b2[...])


MOE_CHUNK = 256


def _moe_kernel(x_ref, wr, br, wg, wu, wd, g2, b2, o_ref, xb, comb, pos_scr, post_scr):
    e = pl.program_id(1)
    tm = x_ref.shape[0]

    @pl.when(e == 0)
    def _():
        x = x_ref[...]
        xb[...] = x.astype(BF16)
        o_ref[...] = jnp.zeros_like(o_ref)
        logits = jnp.dot(x, wr[...], preferred_element_type=F32, precision=lax.Precision.HIGHEST) + br[...]
        lane = lax.broadcasted_iota(I32, logits.shape, 1).astype(F32)
        v1 = jnp.max(logits, axis=-1, keepdims=True)
        i1 = jnp.min(jnp.where(logits == v1, lane, float(LANES)), axis=-1, keepdims=True)
        rest = jnp.where(lane == i1, -jnp.inf, logits)
        v2 = jnp.max(rest, axis=-1, keepdims=True)
        i2 = jnp.min(jnp.where(rest == v2, lane, float(LANES)), axis=-1, keepdims=True)
        e2 = jnp.exp(v2 - v1)
        den = 1.0 + e2
        comb[...] = jnp.where(lane == i1, 1.0 / den, jnp.where(lane == i2, e2 / den, 0.0))
        member = jnp.where(lane == i1, 1.0, jnp.where(lane == i2, 1.0, 0.0))
        earlier = (lax.broadcasted_iota(I32, (tm, tm), 0) > lax.broadcasted_iota(I32, (tm, tm), 1))
        pos = jnp.dot(jnp.where(earlier, 1.0, 0.0).astype(BF16), member.astype(BF16), preferred_element_type=F32)
        posm = jnp.where(member > 0.0, pos, -1.0)
        pos_scr[...] = posm
        post_scr[...] = jnp.transpose(posm)

    sel = lax.broadcasted_iota(I32, (tm, LANES), 1) == e
    gate = jnp.sum(jnp.where(sel, comb[...], 0.0), axis=-1, keepdims=True)
    slot_col = jnp.sum(jnp.where(sel, pos_scr[...], 0.0), axis=-1, keepdims=True)
    slot_row = post_scr[pl.ds(e, 1), :]
    count = jnp.max(slot_row) + 1.0

    chunk = min(MOE_CHUNK, tm)
    for c in range(tm // chunk):
        @pl.when(count > c * chunk)
        def _(c=c):
            base = float(c * chunk)
            rows = lax.broadcasted_iota(I32, (chunk, 1), 0).astype(F32) + base
            gather = jnp.where(slot_row == rows, 1.0, 0.0).astype(BF16)
            xg = jnp.dot(gather, xb[...], preferred_element_type=F32).astype(BF16)
            y = _swiglu(xg, wg[0], wu[0], wd[0])
            cols = lax.broadcasted_iota(I32, (1, chunk), 1).astype(F32) + base
            scatter = jnp.where(slot_col == cols, 1.0, 0.0).astype(BF16)
            y_hi = y.astype(BF16)
            y_lo = (y - y_hi.astype(F32)).astype(BF16)
            back = (jnp.dot(scatter, y_hi, preferred_element_type=F32)
                    + jnp.dot(scatter, y_lo, preferred_element_type=F32))
            o_ref[...] += gate * back

    @pl.when(e == pl.num_programs(1) - 1)
    def _():
        o_ref[...] = _layer_norm(ALPHA * x_ref[...] + o_ref[...], g2[...], b2[...])


def _experts(x, wr, br, wg, wu, wd, g2, b2, tm, routed):
    m = x.shape[0]
    ne, _, dff = wg.shape
    row = pl.BlockSpec((tm, D_MODEL), lambda i, e: (i, 0))
    full = lambda arr: pl.BlockSpec(arr.shape, lambda i, e: (0,) * arr.ndim)
    wspecs = [pl.BlockSpec((1, D_MODEL, dff), lambda i, e: (e, 0, 0)),
              pl.BlockSpec((1, D_MODEL, dff), lambda i, e: (e, 0, 0)),
              pl.BlockSpec((1, dff, D_MODEL), lambda i, e: (e, 0, 0))]
    common = dict(grid=(m // tm, ne), out_specs=row, out_shape=jax.ShapeDtypeStruct((m, D_MODEL), F32),
                  compiler_params=_cparams("parallel", "arbitrary"))
    if not routed:
        return pl.pallas_call(
            _ffn_kernel, in_specs=[row] + wspecs + [full(g2), full(b2)],
            scratch_shapes=[pltpu.VMEM((tm, D_MODEL), BF16)], name="experts_dense", **common,
        )(x, wg, wu, wd, g2, b2)
    return pl.pallas_call(
        _moe_kernel, in_specs=[row, full(wr), full(br)] + wspecs + [full(g2), full(b2)],
        scratch_shapes=[pltpu.VMEM((tm, D_MODEL), BF16), pltpu.VMEM((tm, LANES), F32),
                        pltpu.VMEM((tm, LANES), F32), pltpu.VMEM((LANES, tm), F32)],
        name="experts_routed", **common,
    )(x, wr, br, wg, wu, wd, g2, b2)


DSA_COARSE_BITS = 28


def _dsa_kernel(q_ref, qi_ref, wi_ref, kt_ref, v_ref, kit_ref, prev_ref, o_ref, key_scr, jcut_scr, thr_scr,
                *, top_k, causal, limit_const, blk_off):
    del prev_ref
    tq = q_ref.shape[1]
    s_len = kt_ref.shape[2]
    if causal:
        row_chunk = lax.shift_right_logical(lax.broadcasted_iota(I32, (tq, 1), 0), CHUNK.bit_length() - 1)
        chunk = (pl.program_id(1) + blk_off) * (tq // CHUNK) + row_chunk
        limit = (chunk + 1) * CHUNK
        kth = jnp.minimum(top_k, limit).astype(F32)
    else:
        limit = limit_const
        kth = jnp.full((tq, 1), min(top_k, limit_const), F32)
    idx = lax.broadcasted_iota(I32, (tq, s_len), 1)

    kit = kit_ref[0]
    wi = wi_ref[0] * (IDX_HEADS ** -0.5)
    score = jnp.zeros((tq, s_len), F32)
    for h in range(IDX_HEADS):
        rel = jnp.dot(qi_ref[0, h], kit, preferred_element_type=F32)
        score = score + wi[:, IDX_DIM + h:IDX_DIM + h + 1] * jnp.maximum(rel, 0.0)
    score = jnp.where(score == 0.0, 0.0, score)
    bits = pltpu.bitcast(score, I32)
    key = jnp.where(bits < 0, bits ^ 0x7FFFFFFF, bits)
    key_scr[...] = jnp.where(idx < limit, key, INT_MIN)

    n_grp = 2 if tq % (4 * SUBLANES) == 0 else 1
    rg = tq // n_grp
    rows_of = lambda g: slice(g * rg, (g + 1) * rg)

    def count_ge(thrs):
        return [jnp.sum(jnp.where(key_scr[rows_of(g), :] >= thrs[g], 1.0, 0.0), axis=1, keepdims=True)
                for g in range(n_grp)]

    def bit_step(i, thrs):
        bit = lax.shift_left(jnp.int32(1), 31 - i)
        cands = [t + bit for t in thrs]
        cnts = count_ge(cands)
        return tuple(jnp.where(cnts[g] >= kth[rows_of(g)], cands[g], thrs[g]) for g in range(n_grp))

    thrs = lax.fori_loop(0, DSA_COARSE_BITS, bit_step,
                         tuple(jnp.full((rg, 1), INT_MIN, I32) for _ in range(n_grp)), unroll=4)
    thr_scr[...] = jnp.concatenate(thrs, axis=0)
    n_coarse = jnp.concatenate(count_ge(thrs), axis=0)

    @pl.when(jnp.max(n_coarse - kth) > 0.0)
    def _():
        fine = lax.fori_loop(DSA_COARSE_BITS, 32, bit_step, tuple(thr_scr[rows_of(g), :] for g in range(n_grp)),
                             unroll=4)
        thr_scr[...] = jnp.concatenate(fine, axis=0)

    thr = thr_scr[...]
    keyv = key_scr[...]
    n_ge = jnp.sum(jnp.where(keyv >= thr, 1.0, 0.0), axis=1, keepdims=True)
    jcut_scr[...] = jnp.full((tq, 1), s_len, I32)

    @pl.when(jnp.max(n_ge - kth) > 0.0)
    def _():
        need = kth - jnp.sum(jnp.where(key_scr[...] > thr, 1.0, 0.0), axis=1, keepdims=True)
        n_bits = max(1, (s_len - 1).bit_length())

        def idx_step(i, j):
            cand = j + lax.shift_left(jnp.int32(1), n_bits - 1 - i)
            cnt = jnp.sum(jnp.where(key_scr[...] == thr, jnp.where(idx < cand, 1.0, 0.0), 0.0),
                          axis=1, keepdims=True)
            return jnp.where(cnt < need, cand, j)

        jcut_scr[...] = lax.fori_loop(0, n_bits, idx_step, jnp.zeros((tq, 1), I32))

    jcut = jcut_scr[...]
    bias = jnp.where(keyv > thr, 0.0,
                     jnp.where(keyv == thr, jnp.where(idx <= jcut, 0.0, -jnp.inf), -jnp.inf)).astype(BF16)

    q = q_ref[0] * (C_HEAD_DIM ** -0.5)
    for grp in range(C_KV_HEADS):
        kt = kt_ref[0, grp * C_HEAD_DIM:(grp + 1) * C_HEAD_DIM, :]
        v_ext = v_ref[0, :, grp * 2 * C_HEAD_DIM:(grp + 1) * 2 * C_HEAD_DIM]
        hsl = [slice((grp * C_GROUP + hg) * C_HEAD_DIM, (grp * C_GROUP + hg + 1) * C_HEAD_DIM)
               for hg in range(C_GROUP)]
        qg = jnp.concatenate([q[:, sl] for sl in hsl], axis=0).astype(BF16)
        lg = jnp.dot(qg, kt, preferred_element_type=F32).astype(BF16)
        ps = []
        for hg in range(C_GROUP):
            logits = lg[hg * tq:(hg + 1) * tq, :] + bias
            m_tile = functools.reduce(jnp.maximum, [logits[:, i * LANES:(i + 1) * LANES]
                                                    for i in range(s_len // LANES)])
            mx = jnp.max(m_tile.astype(F32), axis=1, keepdims=True).astype(BF16)
            ps.append(jnp.exp(logits - mx))
        out = jnp.dot(jnp.concatenate(ps, axis=0), v_ext, preferred_element_type=F32)
        for hg in range(C_GROUP):
            rows = slice(hg * tq, (hg + 1) * tq)
            o_ref[0, :, hsl[hg]] = out[rows, 0:C_HEAD_DIM] / out[rows, C_HEAD_DIM:C_HEAD_DIM + 1]


DSA_KEY_STEP = 512


def _dsa_call(u, qi_heads, kt, v, kit, prev, tq, top_k, causal, limit_const, blk_off, n_blk, s_len):
    b, t, _ = u.shape
    return pl.pallas_call(
        functools.partial(_dsa_kernel, top_k=top_k, causal=causal, limit_const=limit_const, blk_off=blk_off),
        grid=(b, n_blk),
        in_specs=[pl.BlockSpec((1, tq, C_WIDTH), lambda i, j: (i, j + blk_off, OD_Q // C_WIDTH)),
                  pl.BlockSpec((1, IDX_HEADS, tq, LANES), lambda i, j: (i, 0, j + blk_off, 0)),
                  pl.BlockSpec((1, tq, LANES), lambda i, j: (i, j + blk_off, OD_KI // LANES)),
                  pl.BlockSpec((1, 2 * C_HEAD_DIM, s_len), lambda i, j: (i, 0, 0)),
                  pl.BlockSpec((1, s_len, 2 * C_KV_HEADS * C_HEAD_DIM), lambda i, j: (i, 0, 0)),
                  pl.BlockSpec((1, LANES, s_len), lambda i, j: (i, 0, 0)),
                  pl.BlockSpec(memory_space=pl.ANY)],
        out_specs=pl.BlockSpec((1, tq, C_WIDTH), lambda i, j: (i, j + blk_off, 0)),
        out_shape=jax.ShapeDtypeStruct((b, t, C_WIDTH), F32),
        scratch_shapes=[pltpu.VMEM((tq, s_len), I32), pltpu.VMEM((tq, 1), I32), pltpu.VMEM((tq, 1), I32)],
        input_output_aliases={6: 0},
        compiler_params=_cparams("parallel", "arbitrary"),
        name="dsa",
    )(u, qi_heads, u, kt, v, kit, prev)


def _dsa(u, qi_heads, kt, v, kit, tq, top_k, causal, limit_const):
    b, t, _ = u.shape
    s_full = kt.shape[2]
    att = jnp.zeros((b, t, C_WIDTH), F32)
    if not causal or s_full % DSA_KEY_STEP:
        return _dsa_call(u, qi_heads, kt, v, kit, att, tq, top_k, causal, limit_const, 0, t // tq, s_full)
    per = DSA_KEY_STEP // tq
    for cls in range(s_full // DSA_KEY_STEP):
        att = _dsa_call(u, qi_heads, kt, v, kit, att, tq, top_k, causal, limit_const,
                        cls * per, per, (cls + 1) * DSA_KEY_STEP)
    return att


def _prep_even(w):
    win = w['w_in']
    a, bq = win[:, :A_COLS], win[:, A_COLS:]
    cols = [a[:, 0:3072], bq[:, 0:3072], a[:, 3072:A_COLS], bq[:, 3072:3088],
            jnp.zeros((D_MODEL, EV_COLS_PAD - EV_GATE - 2 * B_HEADS), F32)]
    p = {'w_in': jnp.concatenate(cols, axis=1).astype(BF16)}
    mu = w['mu']
    p['mu_main'] = mu[None, 0:3072]
    p['mu_lora'] = mu[None, 3072:A_COLS]
    row = lambda v: v.reshape(1, -1)
    p['w0'], p['a0'], p['k_k'], p['k_a'] = row(w['w0']), row(w['a0']), row(w['k_k']), row(w['k_a'])
    p['r_k'] = row(w['r_k'])
    z = lambda n: jnp.zeros((n, A_WIDTH), F32)
    p['w2p'] = jnp.concatenate([w['w2'], z(192)], axis=0).astype(BF16)
    p['a2p'] = jnp.concatenate([z(64), w['a2'], z(128)], axis=0).astype(BF16)
    p['g2p'] = jnp.concatenate([z(128), w['g2']], axis=0).astype(BF16)
    seg = jnp.arange(A_WIDTH) // A_HEAD_DIM
    p['seg64'] = (seg[:, None] == seg[None, :]).astype(BF16)
    p['a_ln_g'], p['a_ln_b'] = row(w['a_ln_g']), row(w['a_ln_b'])
    p['conv_w'], p['conv_b'] = w['conv_w'], row(w['conv_b'])
    p['gate_b'] = jnp.concatenate([w['b_i'], w['b_f'], jnp.zeros((LANES - 2 * B_HEADS,), F32)])[None]
    p['b_ln_g'], p['b_ln_b'] = row(w['b_ln_g']), row(w['b_ln_b'])
    p['w_out_a'] = w['w_out'][:A_WIDTH].astype(BF16)
    p['w_out_b'] = w['w_out'][A_WIDTH:].astype(BF16)
    p['ln1_g'], p['ln1_b'], p['ln2_g'], p['ln2_b'] = row(w['ln1_g']), row(w['ln1_b']), row(w['ln2_g']), row(w['ln2_b'])
    half = w['ffn_gate'].shape[1] // 2
    p['ffn_g'] = jnp.stack([w['ffn_gate'][:, :half], w['ffn_gate'][:, half:]]).astype(BF16)
    p['ffn_u'] = jnp.stack([w['ffn_up'][:, :half], w['ffn_up'][:, half:]]).astype(BF16)
    p['ffn_d'] = jnp.stack([w['ffn_down'][:half], w['ffn_down'][half:]]).astype(BF16)
    return p


def _prep_odd(w):
    row = lambda v: v.reshape(1, -1)
    p = {'w_in': jnp.pad(w['w_in'], ((0, 0), (0, OD_COLS_PAD - w['w_in'].shape[1]))).astype(BF16)}
    p['w_out'] = w['w_out'].astype(BF16)
    p['ln1_g'], p['ln1_b'], p['ln2_g'], p['ln2_b'] = row(w['ln1_g']), row(w['ln1_b']), row(w['ln2_g']), row(w['ln2_b'])
    p['router_w'] = jnp.pad(w['router'], ((0, 0), (0, LANES - N_EXPERTS)))
    p['router_b'] = jnp.concatenate([w['router_b'], jnp.full((LANES - N_EXPERTS,), -1e30, F32)])[None]
    p['e_gate'], p['e_up'], p['e_down'] = w['e_gate'].astype(BF16), w['e_up'].astype(BF16), w['e_down'].astype(BF16)
    return p


def _pick(n, pref):
    for c in pref:
        if n % c == 0:
            return c
    return n


def _even_layer(x, shift0, wkv0, conv0, c0, n0, m0, p):
    b, t, _ = x.shape
    m = b * t
    tm = _pick(m, (512, 256, 128, 64, 32, 16, 8))
    u = _matmul(x.reshape(m, D_MODEL), p['w_in'], tm, EV_COLS_PAD // 3).reshape(b, t, EV_COLS_PAD)

    new_shift = jnp.concatenate([u[:, t - 1:, 0:3072], u[:, t - 1:, EV_LORA:EV_LORA + A_LORA]], axis=-1)
    new_conv = jnp.concatenate([conv0, u[:, :, EV_QK:EV_QK + 1024]], axis=1)[:, t:]

    s0_main, s0_lora = shift0[:, :, 0:3072], shift0[:, :, 3072:A_COLS]
    tp = _pick(t, (256, 128, 64, 32, 16, 8))
    r, w, k, v, a, bb, g, bonus = _rwkv_pre(u, s0_main, s0_lora, p, tp)
    rows = b * A_HEADS // 2
    assert rows % LANES == 0, "batch must be a multiple of 16"
    to_rows = lambda z: z.reshape(t, rows, LANES)
    s0 = wkv0.reshape(b, A_HEADS // 2, 2, A_HEAD_DIM, A_HEAD_DIM).transpose(2, 4, 3, 0, 1)
    s0 = s0.reshape(2, A_HEAD_DIM, A_HEAD_DIM, rows)
    tt = _pick(t, (16, 8))
    y, s_fin = _wkv_scan(*(to_rows(z) for z in (r, w, k, v, a, bb)), s0, tt)
    y = y.reshape(t, b * A_WIDTH)
    new_wkv = s_fin.reshape(2, A_HEAD_DIM, A_HEAD_DIM, b, A_HEADS // 2).transpose(3, 4, 0, 2, 1)
    new_wkv = new_wkv.reshape(b, A_HEADS, A_HEAD_DIM, A_HEAD_DIM)

    ct0 = jnp.concatenate([c0.transpose(0, 1, 3, 2).reshape(b, B_QK_WIDTH, B_V_DIM),
                           n0.reshape(b, B_QK_WIDTH, 1),
                           jnp.zeros((b, B_QK_WIDTH, LANES - 1), F32)], axis=-1)
    m0p = jnp.pad(m0, ((0, 0), (0, LANES - B_HEADS)))[:, None, :]
    lm = _pick(t, (256, 128))
    if t % LANES:
        t_pad = -t % LANES
        u_b = jnp.pad(u, ((0, 0), (0, t_pad), (0, 0)))
        lm = LANES
    else:
        u_b = u
    yb, ct, m_out = _mlstm(u_b, conv0, ct0, m0p, p, lm, t)
    yb = yb[:, :t].reshape(m, B_V_WIDTH)
    new_c = ct[:, :, 0:B_V_DIM].reshape(b, B_HEADS, B_QK_DIM, B_V_DIM).transpose(0, 1, 3, 2)
    new_n = ct[:, :, B_V_DIM].reshape(b, B_HEADS, B_QK_DIM)
    new_m = m_out[:, 0, 0:B_HEADS]

    x2 = x.reshape(m, D_MODEL)
    x2 = _even_out(y, bonus, g, yb, x2, p, b, _pick(t, (256, 128, 64, 32, 16, 8)))
    x2 = _experts(x2, None, None, p['ffn_g'], p['ffn_u'], p['ffn_d'], p['ln2_g'], p['ln2_b'], tm, routed=False)
    return x2.reshape(b, t, D_MODEL), (new_shift, new_wkv, new_conv, new_c, new_n, new_m)


def _odd_layer(x, past_k, past_v, past_ki, p):
    b, t, _ = x.shape
    m = b * t
    tm = _pick(m, (512, 256, 128, 64, 32, 16, 8))
    u = _matmul(x.reshape(m, D_MODEL), p['w_in'], tm, OD_COLS_PAD).reshape(b, t, OD_COLS_PAD)
    k_new = u[:, :, OD_K:OD_K + 256]
    v_new = u[:, :, OD_V:OD_V + 256]
    ki_new = u[:, :, OD_KI:OD_KI + IDX_DIM]
    if past_k is None:
        keys_k, keys_v, keys_i = k_new, v_new, ki_new
        causal, limit, tq = True, 0, _pick(t, (2 * CHUNK, CHUNK))
        top_k = min(TOPK_MAX, t // 4)
    else:
        keys_k = jnp.concatenate([past_k.reshape(b, -1, 256), k_new], axis=1)
        keys_v = jnp.concatenate([past_v.reshape(b, -1, 256), v_new], axis=1)
        keys_i = jnp.concatenate([past_ki, ki_new], axis=1)
        limit = keys_k.shape[1]
        causal, tq = False, t
        top_k = min(TOPK_MAX, limit // 4)
    s_pad = -keys_k.shape[1] % LANES
    pad_s = lambda z: jnp.pad(z, ((0, 0), (0, s_pad), (0, 0))) if s_pad else z
    kt = pad_s(keys_k).transpose(0, 2, 1).astype(BF16)
    vv = pad_s(keys_v).astype(BF16).reshape(b, -1, C_KV_HEADS, C_HEAD_DIM)
    vv = jnp.concatenate([vv, jnp.ones_like(vv)], axis=-1).reshape(b, -1, 2 * C_KV_HEADS * C_HEAD_DIM)
    kit = jnp.pad(pad_s(keys_i), ((0, 0), (0, 0), (0, LANES - IDX_DIM))).transpose(0, 2, 1).astype(BF16)
    qi = u[:, :, OD_QI:OD_QI + IDX_HEADS * IDX_DIM].reshape(b, t, IDX_HEADS, IDX_DIM).transpose(0, 2, 1, 3)
    qi = jnp.pad(qi, ((0, 0), (0, 0), (0, 0), (0, LANES - IDX_DIM))).astype(BF16)
    att = _dsa(u, qi, kt, vv, kit, tq, top_k, causal, limit)
    x2 = _proj_ln(att.reshape(m, C_WIDTH), x.reshape(m, D_MODEL), p['w_out'], p['ln1_g'], p['ln1_b'],
                  _pick(m, (512, 256, 128, 64, 32, 16, 8)))
    x2 = _experts(x2, p['router_w'], p['router_b'], p['e_gate'], p['e_up'], p['e_down'],
                  p['ln2_g'], p['ln2_b'], _pick(m, (1024, 512, 256, 128)), routed=True)
    st = (k_new.reshape(b, t, C_KV_HEADS, C_HEAD_DIM), v_new.reshape(b, t, C_KV_HEADS, C_HEAD_DIM), ki_new)
    return x2.reshape(b, t, D_MODEL), st


def _trunk(x, shift0, wkv0, conv0, c0, n0, m0, past_k, past_v, past_ki, ev, od):
    even_out = [[] for _ in range(6)]
    odd_out = [[] for _ in range(3)]
    for layer in range(DEPTH):
        i = layer // 2
        if layer % 2 == 0:
            x, st = _even_layer(x, shift0[i], wkv0[i], conv0[i], c0[i], n0[i], m0[i], ev[i])
            for acc, val in zip(even_out, st):
                acc.append(val)
        else:
            if past_k is None:
                x, st = _odd_layer(x, None, None, None, od[i])
            else:
                x, st = _odd_layer(x, past_k[i], past_v[i], past_ki[i], od[i])
            for acc, val in zip(odd_out, st):
                acc.append(val)
    return x, [jnp.stack(v) for v in even_out], [jnp.stack(v) for v in odd_out]


def kernel(x_prompt, x_sample, state_shift, state_wkv, state_conv, state_c, state_n, state_m,
           cache_k, cache_v, cache_idx_k,
           ev_w_in, a_mu, a_w0, a_w2, a_a0, a_a2, a_g2, a_k_k, a_k_a, a_r_k, a_ln_g, a_ln_b,
           b_conv_w, b_conv_b, b_i_bias, b_f_bias, b_ln_g, b_ln_b, ev_w_out, ev_ln1_g, ev_ln1_b,
           ffn_w_gate, ffn_w_up, ffn_w_down, ev_ln2_g, ev_ln2_b,
           od_w_in, od_w_out, od_ln1_g, od_ln1_b, moe_w_router, moe_b_router,
           moe_w_gate, moe_w_up, moe_w_down, od_ln2_g, od_ln2_b):
    ew = {'w_in': ev_w_in, 'mu': a_mu, 'w0': a_w0, 'w2': a_w2, 'a0': a_a0, 'a2': a_a2, 'g2': a_g2,
          'k_k': a_k_k, 'k_a': a_k_a, 'r_k': a_r_k, 'a_ln_g': a_ln_g, 'a_ln_b': a_ln_b,
          'conv_w': b_conv_w, 'conv_b': b_conv_b, 'b_i': b_i_bias, 'b_f': b_f_bias,
          'b_ln_g': b_ln_g, 'b_ln_b': b_ln_b, 'w_out': ev_w_out, 'ln1_g': ev_ln1_g, 'ln1_b': ev_ln1_b,
          'ffn_gate': ffn_w_gate, 'ffn_up': ffn_w_up, 'ffn_down': ffn_w_down,
          'ln2_g': ev_ln2_g, 'ln2_b': ev_ln2_b}
    ow = {'w_in': od_w_in, 'w_out': od_w_out, 'ln1_g': od_ln1_g, 'ln1_b': od_ln1_b,
          'router': moe_w_router, 'router_b': moe_b_router, 'e_gate': moe_w_gate, 'e_up': moe_w_up,
          'e_down': moe_w_down, 'ln2_g': od_ln2_g, 'ln2_b': od_ln2_b}
    n_even, n_odd = ev_w_in.shape[0], od_w_in.shape[0]
    ev = [_prep_even({k: v[i] for k, v in ew.items()}) for i in range(n_even)]
    od = [_prep_odd({k: v[i] for k, v in ow.items()}) for i in range(n_odd)]

    bp = x_prompt.shape[0]
    z = functools.partial(jnp.zeros, dtype=F32)
    y_p, ep, op = _trunk(x_prompt,
                         z((n_even, bp, 1, A_COLS)), z((n_even, bp, A_HEADS, A_HEAD_DIM, A_HEAD_DIM)),
                         z((n_even, bp, 3, 2 * B_QK_WIDTH)), z((n_even, bp, B_HEADS, B_V_DIM, B_QK_DIM)),
                         z((n_even, bp, B_HEADS, B_QK_DIM)), z((n_even, bp, B_HEADS)),
                         None, None, None, ev, od)
    y_s, es, os_ = _trunk(x_sample, state_shift, state_wkv, state_conv, state_c, state_n, state_m,
                          cache_k, cache_v, cache_idx_k, ev, od)
    return (y_p, y_s,
            ep[0], es[0], ep[1], es[1], ep[2], es[2], ep[3], es[3], ep[4], es[4], ep[5], es[5],
            op[0], os_[0], op[1], os_[1], op[2], os_[2])
```

```python
import functools
import math

import jax
import jax.numpy as jnp
from jax import lax
from jax.experimental import pallas as pl
from jax.experimental.pallas import tpu as pltpu

F32 = jnp.float32
BF16 = jnp.bfloat16
I32 = jnp.int32

D_MODEL = 1024
DEPTH = 4
CHUNK = 64
A_HEADS, A_HEAD_DIM, A_WIDTH = 16, 64, 1024
A_LORA = 256
A_COLS = 3 * A_WIDTH + A_LORA
B_HEADS, B_QK_DIM, B_V_DIM = 8, 64, 128
B_QK_WIDTH, B_V_WIDTH = 512, 1024
C_HEADS, C_KV_HEADS, C_HEAD_DIM, C_GROUP = 8, 2, 128, 4
C_WIDTH = 1024
IDX_HEADS, IDX_DIM = 8, 64
TOPK_MAX = 256
N_EXPERTS = 8
D_FF_EXPERT = 1408
ALPHA = (2 * DEPTH) ** 0.25
LN_EPS = 1e-5

LANES = 128
SUBLANES = 8
VMEM_LIMIT = 56 * 1024 * 1024

EV_R, EV_K, EV_V, EV_QK, EV_VB, EV_O, EV_LORA, EV_GATE = 0, 1024, 2048, 3072, 4096, 5120, 6144, 6400
EV_COLS_PAD = 6528
OD_Q, OD_K, OD_V, OD_QI, OD_KI = 0, 1024, 1280, 1536, 2048
OD_COLS_PAD = 2176

INT_MIN = -2 ** 31


def _cparams(*sem):
    return pltpu.CompilerParams(dimension_semantics=sem, vmem_limit_bytes=VMEM_LIMIT)


def _layer_norm(z, g, b):
    mu = jnp.mean(z, axis=-1, keepdims=True)
    d = z - mu
    var = jnp.mean(d * d, axis=-1, keepdims=True)
    return d * lax.rsqrt(var + LN_EPS) * g + b


def _segsum(x, e):
    hi = x.astype(BF16)
    lo = (x - hi.astype(F32)).astype(BF16)
    return jnp.dot(hi, e, preferred_element_type=F32) + jnp.dot(lo, e, preferred_element_type=F32)


def _mm_kernel(x_ref, w_ref, o_ref):
    o_ref[...] = jnp.dot(x_ref[...].astype(BF16), w_ref[...], preferred_element_type=F32)


def _matmul(x, w, tm, tn):
    m, k = x.shape
    n = w.shape[1]
    return pl.pallas_call(
        _mm_kernel,
        grid=(n // tn, m // tm),
        in_specs=[pl.BlockSpec((tm, k), lambda j, i: (i, 0)),
                  pl.BlockSpec((k, tn), lambda j, i: (0, j))],
        out_specs=pl.BlockSpec((tm, tn), lambda j, i: (i, j)),
        out_shape=jax.ShapeDtypeStruct((m, n), F32),
        compiler_params=_cparams("parallel", "parallel"),
        name="proj_in",
    )(x, w)


def _rwkv_pre_kernel(ur, uk, uv, ul, s0m, s0l, mum, mul_, w0, a0, kk_, ka_, rk_, w2, a2, g2, e_ref,
                     o_seq, o_g, o_bonus, prev_m, prev_l):
    @pl.when(pl.program_id(1) == 0)
    def _():
        prev_m[...] = s0m[0]
        prev_l[...] = s0l[0]

    tm = ur.shape[1]
    row = lax.broadcasted_iota(I32, (tm, 1), 0)

    def lerp(x, prev, mu):
        shifted = jnp.where(row == 0, prev, pltpu.roll(x, 1, 0))
        return x + (shifted - x) * mu

    r_raw, k_raw, v_raw, l_raw = ur[0], uk[0], uv[0], ul[0]
    r = lerp(r_raw, prev_m[:, 0:1024], mum[:, 0:1024])
    k = lerp(k_raw, prev_m[:, 1024:2048], mum[:, 1024:2048])
    v = lerp(v_raw, prev_m[:, 2048:3072], mum[:, 2048:3072])
    lo = lerp(l_raw, prev_l[...], mul_[...])
    prev_m[:, 0:1024] = r_raw[tm - 1:tm, :]
    prev_m[:, 1024:2048] = k_raw[tm - 1:tm, :]
    prev_m[:, 2048:3072] = v_raw[tm - 1:tm, :]
    prev_l[...] = l_raw[tm - 1:tm, :]

    zw = w0[...] + jnp.dot(jnp.tanh(lo).astype(BF16), w2[...], preferred_element_type=F32)
    decay = jnp.exp(-(math.exp(-0.5) * jax.nn.sigmoid(zw)))
    a = jax.nn.sigmoid(a0[...] + jnp.dot(lo.astype(BF16), a2[...], preferred_element_type=F32))
    g = jnp.dot(jax.nn.sigmoid(lo).astype(BF16), g2[...], preferred_element_type=F32)

    e = e_ref[...]
    kk = k * kk_[...]
    kk = kk * lax.rsqrt(jnp.maximum(_segsum(kk * kk, e), 1e-24))
    k2 = k * (1.0 + (a - 1.0) * ka_[...])
    bonus = _segsum(r * k2 * rk_[...], e) * v

    o_seq[WKV_SRC_R] = r
    o_seq[WKV_SRC_W] = decay
    o_seq[WKV_SRC_K] = k2
    o_seq[WKV_SRC_V] = v
    o_seq[WKV_SRC_A] = -kk
    o_seq[WKV_SRC_B] = kk * a
    o_g[...] = g
    o_bonus[...] = bonus


def _rwkv_pre(u, s0_main, s0_lora, p, tm):
    b, t, _ = u.shape
    blk = lambda w, c: pl.BlockSpec((1, tm, w), lambda i, j, c=c: (i, j, c))
    full = lambda arr: pl.BlockSpec(arr.shape, lambda i, j: (0,) * arr.ndim)
    params = [p['mu_main'], p['mu_lora'], p['w0'], p['a0'], p['k_k'], p['k_a'], p['r_k'],
              p['w2p'], p['a2p'], p['g2p'], p['seg64']]
    out = jax.ShapeDtypeStruct((t, b * A_WIDTH), F32)
    return pl.pallas_call(
        _rwkv_pre_kernel,
        grid=(b, t // tm),
        in_specs=[blk(1024, EV_R // 1024), blk(1024, EV_K // 1024), blk(1024, EV_V // 1024),
                  blk(A_LORA, EV_LORA // A_LORA),
                  pl.BlockSpec((1, 1, 3072), lambda i, j: (i, 0, 0)),
                  pl.BlockSpec((1, 1, A_LORA), lambda i, j: (i, 0, 0))] + [full(a) for a in params],
        out_specs=[pl.BlockSpec((WKV_N_SRC, tm, A_WIDTH), lambda i, j: (0, j, i)),
                   pl.BlockSpec((tm, A_WIDTH), lambda i, j: (j, i)),
                   pl.BlockSpec((tm, A_WIDTH), lambda i, j: (j, i))],
        out_shape=[jax.ShapeDtypeStruct((WKV_N_SRC, t, b * A_WIDTH), F32), out, out],
        scratch_shapes=[pltpu.VMEM((1, 3072), F32), pltpu.VMEM((1, A_LORA), F32)],
        compiler_params=_cparams("parallel", "arbitrary"),
        name="rwkv_pre",
    )(u, u, u, u, s0_main, s0_lora, *params)


WKV_N_SRC = 6
WKV_SRC_R, WKV_SRC_W, WKV_SRC_K, WKV_SRC_V, WKV_SRC_A, WKV_SRC_B = range(WKV_N_SRC)
WKV_VGROUP = 4
WKV_KCHUNK = 32


def _wkv_kernel(seq_ref, s0_ref, y_ref, s_ref, tr, sa_buf, ybuf):
    @pl.when(pl.program_id(1) == 0)
    def _():
        s_ref[...] = s0_ref[...]

    n_t = seq_ref.shape[1]
    d = A_HEAD_DIM
    vspan = WKV_VGROUP * SUBLANES
    groups = [(p, g) for p in range(2) for g in range(d // vspan)]
    n_kc = d // WKV_KCHUNK

    def vrows(g, j):
        return slice(g * vspan + j * SUBLANES, g * vspan + (j + 1) * SUBLANES)

    def transpose_tile(i, tok):
        tr[i, tok] = jnp.transpose(seq_ref[i, tok])

    for i in range(WKV_N_SRC):
        transpose_tile(i, 0)
    transpose_tile(WKV_SRC_A, min(1, n_t - 1))

    for p, g in groups:
        acc = [jnp.zeros((SUBLANES, LANES), F32) for _ in range(WKV_VGROUP)]
        for kx in range(d):
            ab = tr[WKV_SRC_A, 0, p * d + kx:p * d + kx + 1, :]
            for j in range(WKV_VGROUP):
                acc[j] = acc[j] + s_ref[p, kx, vrows(g, j), :] * ab
        for j in range(WKV_VGROUP):
            sa_buf[p, vrows(g, j), :] = acc[j]

    def step(t, carry):
        tn = jnp.minimum(t + 1, n_t - 1)
        for gi, (p, g) in enumerate(groups):
            vv = [tr[WKV_SRC_V, t, p * d + g * vspan + j * SUBLANES:p * d + g * vspan + (j + 1) * SUBLANES, :]
                  for j in range(WKV_VGROUP)]
            sa = [sa_buf[p, vrows(g, j), :] for j in range(WKV_VGROUP)]
            zero = tuple(jnp.zeros((SUBLANES, LANES), F32) for _ in range(2 * WKV_VGROUP))

            def key_chunk(kc, accs, gi=gi, p=p, g=g, vv=vv, sa=sa):
                yacc, san = list(accs[:WKV_VGROUP]), list(accs[WKV_VGROUP:])
                for i in range(WKV_KCHUNK):
                    kx = kc * WKV_KCHUNK + i
                    row = pl.ds(p * d + kx, 1)
                    wb = tr[WKV_SRC_W, t, row, :]
                    bb = tr[WKV_SRC_B, t, row, :]
                    kb = tr[WKV_SRC_K, t, row, :]
                    rb = tr[WKV_SRC_R, t, row, :]
                    an = tr[WKV_SRC_A, tn, row, :]
                    for j in range(WKV_VGROUP):
                        n = s_ref[p, kx, vrows(g, j), :] * wb + sa[j] * bb + vv[j] * kb
                        s_ref[p, kx, vrows(g, j), :] = n
                        yacc[j] = yacc[j] + n * rb
                        san[j] = san[j] + n * an
                tile = jnp.minimum(gi * n_kc + kc, WKV_N_SRC - 1)
                ahead = jnp.where(tile == WKV_SRC_A, 2, 1)
                transpose_tile(tile, jnp.minimum(t + ahead, n_t - 1))
                return tuple(yacc) + tuple(san)

            accs = lax.fori_loop(0, n_kc, key_chunk, zero)
            yacc, san = accs[:WKV_VGROUP], accs[WKV_VGROUP:]
            for j in range(WKV_VGROUP):
                ybuf[t, p * d + g * vspan + j * SUBLANES:p * d + g * vspan + (j + 1) * SUBLANES, :] = yacc[j]
                sa_buf[p, vrows(g, j), :] = san[j]
        return carry

    lax.fori_loop(0, n_t, step, 0)

    def transpose_out(t, carry):
        y_ref[t] = jnp.transpose(ybuf[t])
        return carry

    lax.fori_loop(0, n_t, transpose_out, 0, unroll=4)


def _wkv_scan(seq, s0, tt):
    _, t, rows, _ = seq.shape
    d = A_HEAD_DIM
    st = pl.BlockSpec((2, d, d, LANES), lambda g, i: (0, 0, 0, g))
    return pl.pallas_call(
        _wkv_kernel,
        grid=(rows // LANES, t // tt),
        in_specs=[pl.BlockSpec((WKV_N_SRC, tt, LANES, LANES), lambda g, i: (0, i, g, 0)), st],
        out_specs=[pl.BlockSpec((tt, LANES, LANES), lambda g, i: (i, g, 0)), st],
        out_shape=[jax.ShapeDtypeStruct((t, rows, LANES), F32), jax.ShapeDtypeStruct((2, d, d, rows), F32)],
        scratch_shapes=[pltpu.VMEM((WKV_N_SRC, tt, LANES, LANES), F32), pltpu.VMEM((2, d, LANES), F32),
                        pltpu.VMEM((tt, LANES, LANES), F32)],
        compiler_params=_cparams("parallel", "arbitrary"),
        name="wkv_scan",
    )(seq, s0)


def _mlstm_kernel(uqk, uv, uo, ug, conv0, ct0, m0, cw, cb, gb, lng, lnb,
                  o_y, o_ct, o_m, ct, m_scr, carry, *, t_valid):
    tb = pl.program_id(1)

    @pl.when(tb == 0)
    def _():
        ct[...] = ct0[0]
        m_scr[...] = m0[0]
        carry[...] = conv0[0]

    L = uqk.shape[1]
    row = lax.broadcasted_iota(I32, (L, 1), 0)
    x = uqk[0]

    def shifted(j):
        out = pltpu.roll(x, j, 0)
        for i in range(j):
            out = jnp.where(row == i, carry[3 + i - j:4 + i - j, :], out)
        return out

    conv = x * cw[3:4, :] + shifted(1) * cw[2:3, :] + shifted(2) * cw[1:2, :] + shifted(3) * cw[0:1, :] + cb[...]
    carry[...] = x[L - 3:L, :]
    qk = conv * jax.nn.sigmoid(conv)
    q_all = qk[:, 0:B_QK_WIDTH]
    k_t = jnp.transpose(qk[:, B_QK_WIDTH:2 * B_QK_WIDTH] * (B_QK_DIM ** -0.5))

    lane = lax.broadcasted_iota(I32, (L, LANES), 1)
    gz = ug[0] + gb[...]
    lg = jnp.where(lane < B_HEADS, gz, jnp.minimum(gz, 0.0) - jnp.log(1.0 + jnp.exp(-jnp.abs(gz))))
    valid = (row + tb * L) < t_valid
    lg = jnp.where(valid, lg, jnp.where(lane < B_HEADS, -jnp.inf, 0.0))
    lg_t = jnp.transpose(lg)
    ti = lax.broadcasted_iota(I32, (L, L), 0)
    si = lax.broadcasted_iota(I32, (L, L), 1)
    tril = ti >= si
    tril_f = tril.astype(F32)
    triu_f = (ti <= si).astype(F32)
    lg_fin = jnp.where(lane < B_HEADS, 0.0, lg)
    bcum = jnp.dot(tril_f, lg_fin, preferred_element_type=F32, precision=lax.Precision.HIGHEST)
    bcum_t = jnp.dot(jnp.transpose(lg_fin), triu_f, preferred_element_type=F32,
                     precision=lax.Precision.HIGHEST)

    head_q = lax.shift_right_logical(lax.broadcasted_iota(I32, (L, B_QK_WIDTH), 1), 6)
    lane1 = lax.broadcasted_iota(I32, (1, LANES), 1)
    ones_col = (lax.broadcasted_iota(I32, (L, LANES), 1) == 0).astype(F32)
    m_vec = m_scr[...]
    m_new_vec = m_vec
    ct_old = ct[...]
    ct_old_bf = ct_old.astype(BF16)
    vv = uv[0]
    oo = uo[0]
    for h in range(B_HEADS):
        hs = slice(h * B_QK_DIM, (h + 1) * B_QK_DIM)
        vs = slice(h * B_V_DIM, (h + 1) * B_V_DIM)
        bc_col = bcum[:, B_HEADS + h:B_HEADS + h + 1]
        bc_row = bcum_t[B_HEADS + h:B_HEADS + h + 1, :]
        ic_row = lg_t[h:h + 1, :]
        gtot = bc_row[:, L - 1:L]
        m_prev = m_vec[:, h:h + 1]
        dmat = jnp.where(tril, bc_col - bc_row + ic_row, -jnp.inf)
        inter = bc_col + m_prev
        m_t = jnp.maximum(inter, jnp.max(dmat, axis=1, keepdims=True))
        w_intra = jnp.exp(dmat - m_t)
        w_inter = jnp.exp(inter - m_t)
        q_m = jnp.where(head_q == h, q_all, 0.0).astype(BF16)
        s = jnp.dot(q_m, k_t.astype(BF16), preferred_element_type=F32) * w_intra
        v_ext = jnp.concatenate([vv[:, vs], ones_col], axis=1).astype(BF16)
        intra = jnp.dot(s.astype(BF16), v_ext, preferred_element_type=F32)
        cross = jnp.dot(q_m, ct_old_bf, preferred_element_type=F32)
        num = intra[:, 0:B_V_DIM] + w_inter * cross[:, 0:B_V_DIM]
        den = jnp.sum(s, axis=1, keepdims=True) + w_inter * cross[:, B_V_DIM:B_V_DIM + 1]
        hh = num / jnp.maximum(jnp.abs(den), jnp.exp(-m_t))
        mu = jnp.mean(hh, axis=-1, keepdims=True)
        dd = hh - mu
        var = jnp.mean(dd * dd, axis=-1, keepdims=True)
        yn = dd * lax.rsqrt(var + LN_EPS) * lng[:, vs] + lnb[:, vs]
        o_y[0, :, vs] = jax.nn.sigmoid(oo[:, vs]) * yn
        lw = gtot - bc_row + ic_row
        m_new = jnp.maximum(gtot + m_prev, jnp.max(lw, axis=1, keepdims=True))
        w_s = jnp.exp(lw - m_new)
        dec = jnp.exp(gtot + m_prev - m_new)
        upd = jnp.dot((k_t[hs, :] * w_s).astype(BF16), v_ext, preferred_element_type=F32)
        ct[hs, :] = dec * ct_old[hs, :] + upd
        m_new_vec = jnp.where(lane1 == h, m_new, m_new_vec)
    m_scr[...] = m_new_vec

    @pl.when(tb == pl.num_programs(1) - 1)
    def _():
        o_ct[0] = ct[...]
        o_m[0] = m_scr[...]


def _mlstm(u, conv0, ct0, m0, p, L, t_valid):
    b, t, _ = u.shape
    blk = lambda w, c: pl.BlockSpec((1, L, w), lambda i, j, c=c: (i, j, c))
    full = lambda arr: pl.BlockSpec(arr.shape, lambda i, j: (0,) * arr.ndim)
    per_b = lambda arr: pl.BlockSpec((1,) + arr.shape[1:], lambda i, j: (i, 0, 0))
    params = [p['conv_w'], p['conv_b'], p['gate_b'], p['b_ln_g'], p['b_ln_b']]
    return pl.pallas_call(
        functools.partial(_mlstm_kernel, t_valid=t_valid),
        grid=(b, t // L),
        in_specs=[blk(1024, EV_QK // 1024), blk(1024, EV_VB // 1024), blk(1024, EV_O // 1024),
                  blk(LANES, EV_GATE // LANES), per_b(conv0), per_b(ct0), per_b(m0)] + [full(a) for a in params],
        out_specs=[pl.BlockSpec((1, L, B_V_WIDTH), lambda i, j: (i, j, 0)),
                   pl.BlockSpec((1, B_QK_WIDTH, 2 * LANES), lambda i, j: (i, 0, 0)),
                   pl.BlockSpec((1, 1, LANES), lambda i, j: (i, 0, 0))],
        out_shape=[jax.ShapeDtypeStruct((b, t, B_V_WIDTH), F32),
                   jax.ShapeDtypeStruct((b, B_QK_WIDTH, 2 * LANES), F32),
                   jax.ShapeDtypeStruct((b, 1, LANES), F32)],
        scratch_shapes=[pltpu.VMEM((B_QK_WIDTH, 2 * LANES), F32), pltpu.VMEM((1, LANES), F32),
                        pltpu.VMEM((3, 2 * B_QK_WIDTH), F32)],
        compiler_params=_cparams("parallel", "arbitrary"),
        name="mlstm",
    )(u, u, u, u, conv0, ct0, m0, *params)


def _even_out_kernel(y_ref, bonus_ref, g_ref, yb_ref, x_ref, lng, lnb, e_ref, woa, wob, g1, b1, o_ref):
    e = e_ref[...]
    y = y_ref[...]
    mu = _segsum(y, e) * (1.0 / A_HEAD_DIM)
    d = y - mu
    var = _segsum(d * d, e) * (1.0 / A_HEAD_DIM)
    yn = d * lax.rsqrt(var + LN_EPS) * lng[...] + lnb[...]
    ya = (yn + bonus_ref[...]) * g_ref[...]
    mix = (jnp.dot(ya.astype(BF16), woa[...], preferred_element_type=F32)
           + jnp.dot(yb_ref[...].astype(BF16), wob[...], preferred_element_type=F32))
    o_ref[...] = _layer_norm(ALPHA * x_ref[...] + mix, g1[...], b1[...])


def _even_out(y, bonus, g, yb, x, p, b, tm):
    m = x.shape[0]
    nt = m // b // tm
    tmaj = pl.BlockSpec((tm, D_MODEL), lambda i, j: (j, i))
    row = pl.BlockSpec((tm, D_MODEL), lambda i, j: (i * nt + j, 0))
    full = lambda arr: pl.BlockSpec(arr.shape, lambda i, j: (0,) * arr.ndim)
    params = [p['a_ln_g'], p['a_ln_b'], p['seg64'], p['w_out_a'], p['w_out_b'], p['ln1_g'], p['ln1_b']]
    return pl.pallas_call(
        _even_out_kernel,
        grid=(b, nt),
        in_specs=[tmaj] * 3 + [row] * 2 + [full(a) for a in params],
        out_specs=row,
        out_shape=jax.ShapeDtypeStruct((m, D_MODEL), F32),
        compiler_params=_cparams("parallel", "parallel"),
        name="even_out",
    )(y, bonus, g, yb, x, *params)


def _proj_ln_kernel(a_ref, x_ref, w_ref, g1, b1, o_ref):
    mix = jnp.dot(a_ref[...].astype(BF16), w_ref[...], preferred_element_type=F32)
    o_ref[...] = _layer_norm(ALPHA * x_ref[...] + mix, g1[...], b1[...])


def _proj_ln(a, x, w, g1, b1, tm):
    m = x.shape[0]
    row = pl.BlockSpec((tm, D_MODEL), lambda i: (i, 0))
    full = lambda arr: pl.BlockSpec(arr.shape, lambda i: (0,) * arr.ndim)
    return pl.pallas_call(
        _proj_ln_kernel,
        grid=(m // tm,),
        in_specs=[row, row, full(w), full(g1), full(b1)],
        out_specs=row,
        out_shape=jax.ShapeDtypeStruct((m, D_MODEL), F32),
        compiler_params=_cparams("parallel"),
        name="proj_ln",
    )(a, x, w, g1, b1)


def _swiglu(xb, wg, wu, wd):
    hg = jnp.dot(xb, wg, preferred_element_type=F32)
    hu = jnp.dot(xb, wu, preferred_element_type=F32)
    hid = (hg * jax.nn.sigmoid(hg)) * hu
    return jnp.dot(hid.astype(BF16), wd, preferred_element_type=F32)


def _ffn_kernel(x_ref, wg, wu, wd, g2, b2, o_ref, xb):
    e = pl.program_id(1)

    @pl.when(e == 0)
    def _():
        xb[...] = x_ref[...].astype(BF16)
        o_ref[...] = jnp.zeros_like(o_ref)

    o_ref[...] += _swiglu(xb[...], wg[0], wu[0], wd[0])

    @pl.when(e == pl.num_programs(1) - 1)
    def _():
        o_ref[...] = _layer_norm(ALPHA * x_ref[...] + o_ref[...], g2[...], b2[...])


MOE_CHUNK = 256


def _moe_kernel(x_ref, wr, br, wg, wu, wd, g2, b2, o_ref, xb, comb, pos_scr, post_scr):
    e = pl.program_id(1)
    tm = x_ref.shape[0]

    @pl.when(e == 0)
    def _():
        x = x_ref[...]
        xb[...] = x.astype(BF16)
        o_ref[...] = jnp.zeros_like(o_ref)
        logits = jnp.dot(x, wr[...], preferred_element_type=F32, precision=lax.Precision.HIGHEST) + br[...]
        lane = lax.broadcasted_iota(I32, logits.shape, 1).astype(F32)
        v1 = jnp.max(logits, axis=-1, keepdims=True)
        i1 = jnp.min(jnp.where(logits == v1, lane, float(LANES)), axis=-1, keepdims=True)
        rest = jnp.where(lane == i1, -jnp.inf, logits)
        v2 = jnp.max(rest, axis=-1, keepdims=True)
        i2 = jnp.min(jnp.where(rest == v2, lane, float(LANES)), axis=-1, keepdims=True)
        e2 = jnp.exp(v2 - v1)
        den = 1.0 + e2
        comb[...] = jnp.where(lane == i1, 1.0 / den, jnp.where(lane == i2, e2 / den, 0.0))
        member = jnp.where(lane == i1, 1.0, jnp.where(lane == i2, 1.0, 0.0))
        earlier = (lax.broadcasted_iota(I32, (tm, tm), 0) > lax.broadcasted_iota(I32, (tm, tm), 1))
        pos = jnp.dot(jnp.where(earlier, 1.0, 0.0).astype(BF16), member.astype(BF16), preferred_element_type=F32)
        posm = jnp.where(member > 0.0, pos, -1.0)
        pos_scr[...] = posm
        post_scr[...] = jnp.transpose(posm)

    sel = lax.broadcasted_iota(I32, (tm, LANES), 1) == e
    gate = jnp.sum(jnp.where(sel, comb[...], 0.0), axis=-1, keepdims=True)
    slot_col = jnp.sum(jnp.where(sel, pos_scr[...], 0.0), axis=-1, keepdims=True)
    slot_row = post_scr[pl.ds(e, 1), :]
    count = jnp.max(slot_row) + 1.0

    chunk = min(MOE_CHUNK, tm)
    for c in range(tm // chunk):
        @pl.when(count > c * chunk)
        def _(c=c):
            base = float(c * chunk)
            rows = lax.broadcasted_iota(I32, (chunk, 1), 0).astype(F32) + base
            gather = jnp.where(slot_row == rows, 1.0, 0.0).astype(BF16)
            xg = jnp.dot(gather, xb[...], preferred_element_type=F32).astype(BF16)
            y = _swiglu(xg, wg[0], wu[0], wd[0])
            cols = lax.broadcasted_iota(I32, (1, chunk), 1).astype(F32) + base
            scatter = jnp.where(slot_col == cols, 1.0, 0.0).astype(BF16)
            y_hi = y.astype(BF16)
            y_lo = (y - y_hi.astype(F32)).astype(BF16)
            back = (jnp.dot(scatter, y_hi, preferred_element_type=F32)
                    + jnp.dot(scatter, y_lo, preferred_element_type=F32))
            o_ref[...] += gate * back

    @pl.when(e == pl.num_programs(1) - 1)
    def _():
        o_ref[...] = _layer_norm(ALPHA * x_ref[...] + o_ref[...], g2[...], b2[...])


def _experts(x, wr, br, wg, wu, wd, g2, b2, tm, routed):
    m = x.shape[0]
    ne, _, dff = wg.shape
    row = pl.BlockSpec((tm, D_MODEL), lambda i, e: (i, 0))
    full = lambda arr: pl.BlockSpec(arr.shape, lambda i, e: (0,) * arr.ndim)
    wspecs = [pl.BlockSpec((1, D_MODEL, dff), lambda i, e: (e, 0, 0)),
              pl.BlockSpec((1, D_MODEL, dff), lambda i, e: (e, 0, 0)),
              pl.BlockSpec((1, dff, D_MODEL), lambda i, e: (e, 0, 0))]
    common = dict(grid=(m // tm, ne), out_specs=row, out_shape=jax.ShapeDtypeStruct((m, D_MODEL), F32),
                  compiler_params=_cparams("parallel", "arbitrary"))
    if not routed:
        return pl.pallas_call(
            _ffn_kernel, in_specs=[row] + wspecs + [full(g2), full(b2)],
            scratch_shapes=[pltpu.VMEM((tm, D_MODEL), BF16)], name="experts_dense", **common,
        )(x, wg, wu, wd, g2, b2)
    return pl.pallas_call(
        _moe_kernel, in_specs=[row, full(wr), full(br)] + wspecs + [full(g2), full(b2)],
        scratch_shapes=[pltpu.VMEM((tm, D_MODEL), BF16), pltpu.VMEM((tm, LANES), F32),
                        pltpu.VMEM((tm, LANES), F32), pltpu.VMEM((LANES, tm), F32)],
        name="experts_routed", **common,
    )(x, wr, br, wg, wu, wd, g2, b2)


DSA_COARSE_BITS = 28


def _dsa_kernel(q_ref, qi_ref, wi_ref, kt_ref, v_ref, kit_ref, prev_ref, o_ref, key_scr, jcut_scr, thr_scr,
                *, top_k, causal, limit_const, blk_off):
    del prev_ref
    tq = q_ref.shape[1]
    s_len = kt_ref.shape[2]
    if causal:
        row_chunk = lax.shift_right_logical(lax.broadcasted_iota(I32, (tq, 1), 0), CHUNK.bit_length() - 1)
        chunk = (pl.program_id(1) + blk_off) * (tq // CHUNK) + row_chunk
        limit = (chunk + 1) * CHUNK
        kth = jnp.minimum(top_k, limit).astype(F32)
    else:
        limit = limit_const
        kth = jnp.full((tq, 1), min(top_k, limit_const), F32)
    idx = lax.broadcasted_iota(I32, (tq, s_len), 1)

    kit = kit_ref[0]
    wi = wi_ref[0] * (IDX_HEADS ** -0.5)
    score = jnp.zeros((tq, s_len), F32)
    for h in range(IDX_HEADS):
        rel = jnp.dot(qi_ref[0, h], kit, preferred_element_type=F32)
        score = score + wi[:, IDX_DIM + h:IDX_DIM + h + 1] * jnp.maximum(rel, 0.0)
    score = jnp.where(score == 0.0, 0.0, score)
    bits = pltpu.bitcast(score, I32)
    key = jnp.where(bits < 0, bits ^ 0x7FFFFFFF, bits)
    key_scr[...] = jnp.where(idx < limit, key, INT_MIN)

    n_grp = 2 if tq % (4 * SUBLANES) == 0 else 1
    rg = tq // n_grp
    rows_of = lambda g: slice(g * rg, (g + 1) * rg)

    def count_ge(thrs):
        return [jnp.sum(jnp.where(key_scr[rows_of(g), :] >= thrs[g], 1.0, 0.0), axis=1, keepdims=True)
                for g in range(n_grp)]

    def bit_step(i, thrs):
        bit = lax.shift_left(jnp.int32(1), 31 - i)
        cands = [t + bit for t in thrs]
        cnts = count_ge(cands)
        return tuple(jnp.where(cnts[g] >= kth[rows_of(g)], cands[g], thrs[g]) for g in range(n_grp))

    thrs = lax.fori_loop(0, DSA_COARSE_BITS, bit_step,
                         tuple(jnp.full((rg, 1), INT_MIN, I32) for _ in range(n_grp)), unroll=4)
    thr_scr[...] = jnp.concatenate(thrs, axis=0)
    n_coarse = jnp.concatenate(count_ge(thrs), axis=0)

    @pl.when(jnp.max(n_coarse - kth) > 0.0)
    def _():
        fine = lax.fori_loop(DSA_COARSE_BITS, 32, bit_step, tuple(thr_scr[rows_of(g), :] for g in range(n_grp)),
                             unroll=4)
        thr_scr[...] = jnp.concatenate(fine, axis=0)

    thr = thr_scr[...]
    keyv = key_scr[...]
    n_ge = jnp.sum(jnp.where(keyv >= thr, 1.0, 0.0), axis=1, keepdims=True)
    jcut_scr[...] = jnp.full((tq, 1), s_len, I32)

    @pl.when(jnp.max(n_ge - kth) > 0.0)
    def _():
        need = kth - jnp.sum(jnp.where(key_scr[...] > thr, 1.0, 0.0), axis=1, keepdims=True)
        n_bits = max(1, (s_len - 1).bit_length())

        def idx_step(i, j):
            cand = j + lax.shift_left(jnp.int32(1), n_bits - 1 - i)
            cnt = jnp.sum(jnp.where(key_scr[...] == thr, jnp.where(idx < cand, 1.0, 0.0), 0.0),
                          axis=1, keepdims=True)
            return jnp.where(cnt < need, cand, j)

        jcut_scr[...] = lax.fori_loop(0, n_bits, idx_step, jnp.zeros((tq, 1), I32))

    jcut = jcut_scr[...]
    bias = jnp.where(keyv > thr, 0.0,
                     jnp.where(keyv == thr, jnp.where(idx <= jcut, 0.0, -jnp.inf), -jnp.inf)).astype(BF16)

    q = q_ref[0] * (C_HEAD_DIM ** -0.5)
    for grp in range(C_KV_HEADS):
        kt = kt_ref[0, grp * C_HEAD_DIM:(grp + 1) * C_HEAD_DIM, :]
        v_ext = v_ref[0, :, grp * 2 * C_HEAD_DIM:(grp + 1) * 2 * C_HEAD_DIM]
        hsl = [slice((grp * C_GROUP + hg) * C_HEAD_DIM, (grp * C_GROUP + hg + 1) * C_HEAD_DIM)
               for hg in range(C_GROUP)]
        qg = jnp.concatenate([q[:, sl] for sl in hsl], axis=0).astype(BF16)
        lg = jnp.dot(qg, kt, preferred_element_type=F32).astype(BF16)
        ps = []
        for hg in range(C_GROUP):
            logits = lg[hg * tq:(hg + 1) * tq, :] + bias
            m_tile = functools.reduce(jnp.maximum, [logits[:, i * LANES:(i + 1) * LANES]
                                                    for i in range(s_len // LANES)])
            mx = jnp.max(m_tile.astype(F32), axis=1, keepdims=True).astype(BF16)
            ps.append(jnp.exp(logits - mx))
        out = jnp.dot(jnp.concatenate(ps, axis=0), v_ext, preferred_element_type=F32)
        for hg in range(C_GROUP):
            rows = slice(hg * tq, (hg + 1) * tq)
            o_ref[0, :, hsl[hg]] = out[rows, 0:C_HEAD_DIM] / out[rows, C_HEAD_DIM:C_HEAD_DIM + 1]


DSA_KEY_STEP = 512


def _dsa_call(u, qi_heads, kt, v, kit, prev, tq, top_k, causal, limit_const, blk_off, n_blk, s_len):
    b, t, _ = u.shape
    return pl.pallas_call(
        functools.partial(_dsa_kernel, top_k=top_k, causal=causal, limit_const=limit_const, blk_off=blk_off),
        grid=(b, n_blk),
        in_specs=[pl.BlockSpec((1, tq, C_WIDTH), lambda i, j: (i, j + blk_off, OD_Q // C_WIDTH)),
                  pl.BlockSpec((1, IDX_HEADS, tq, LANES), lambda i, j: (i, 0, j + blk_off, 0)),
                  pl.BlockSpec((1, tq, LANES), lambda i, j: (i, j + blk_off, OD_KI // LANES)),
                  pl.BlockSpec((1, 2 * C_HEAD_DIM, s_len), lambda i, j: (i, 0, 0)),
                  pl.BlockSpec((1, s_len, 2 * C_KV_HEADS * C_HEAD_DIM), lambda i, j: (i, 0, 0)),
                  pl.BlockSpec((1, LANES, s_len), lambda i, j: (i, 0, 0)),
                  pl.BlockSpec(memory_space=pl.ANY)],
        out_specs=pl.BlockSpec((1, tq, C_WIDTH), lambda i, j: (i, j + blk_off, 0)),
        out_shape=jax.ShapeDtypeStruct((b, t, C_WIDTH), F32),
        scratch_shapes=[pltpu.VMEM((tq, s_len), I32), pltpu.VMEM((tq, 1), I32), pltpu.VMEM((tq, 1), I32)],
        input_output_aliases={6: 0},
        compiler_params=_cparams("parallel", "arbitrary"),
        name="dsa",
    )(u, qi_heads, u, kt, v, kit, prev)


def _dsa(u, qi_heads, kt, v, kit, tq, top_k, causal, limit_const):
    b, t, _ = u.shape
    s_full = kt.shape[2]
    att = jnp.zeros((b, t, C_WIDTH), F32)
    if not causal or s_full % DSA_KEY_STEP:
        return _dsa_call(u, qi_heads, kt, v, kit, att, tq, top_k, causal, limit_const, 0, t // tq, s_full)
    per = DSA_KEY_STEP // tq
    for cls in range(s_full // DSA_KEY_STEP):
        att = _dsa_call(u, qi_heads, kt, v, kit, att, tq, top_k, causal, limit_const,
                        cls * per, per, (cls + 1) * DSA_KEY_STEP)
    return att


def _prep_even(w):
    win = w['w_in']
    a, bq = win[:, :A_COLS], win[:, A_COLS:]
    cols = [a[:, 0:3072], bq[:, 0:3072], a[:, 3072:A_COLS], bq[:, 3072:3088],
            jnp.zeros((D_MODEL, EV_COLS_PAD - EV_GATE - 2 * B_HEADS), F32)]
    p = {'w_in': jnp.concatenate(cols, axis=1).astype(BF16)}
    mu = w['mu']
    p['mu_main'] = mu[None, 0:3072]
    p['mu_lora'] = mu[None, 3072:A_COLS]
    row = lambda v: v.reshape(1, -1)
    p['w0'], p['a0'], p['k_k'], p['k_a'] = row(w['w0']), row(w['a0']), row(w['k_k']), row(w['k_a'])
    p['r_k'] = row(w['r_k'])
    z = lambda n: jnp.zeros((n, A_WIDTH), F32)
    p['w2p'] = jnp.concatenate([w['w2'], z(192)], axis=0).astype(BF16)
    p['a2p'] = jnp.concatenate([z(64), w['a2'], z(128)], axis=0).astype(BF16)
    p['g2p'] = jnp.concatenate([z(128), w['g2']], axis=0).astype(BF16)
    seg = jnp.arange(A_WIDTH) // A_HEAD_DIM
    p['seg64'] = (seg[:, None] == seg[None, :]).astype(BF16)
    p['a_ln_g'], p['a_ln_b'] = row(w['a_ln_g']), row(w['a_ln_b'])
    p['conv_w'], p['conv_b'] = w['conv_w'], row(w['conv_b'])
    p['gate_b'] = jnp.concatenate([w['b_i'], w['b_f'], jnp.zeros((LANES - 2 * B_HEADS,), F32)])[None]
    p['b_ln_g'], p['b_ln_b'] = row(w['b_ln_g']), row(w['b_ln_b'])
    p['w_out_a'] = w['w_out'][:A_WIDTH].astype(BF16)
    p['w_out_b'] = w['w_out'][A_WIDTH:].astype(BF16)
    p['ln1_g'], p['ln1_b'], p['ln2_g'], p['ln2_b'] = row(w['ln1_g']), row(w['ln1_b']), row(w['ln2_g']), row(w['ln2_b'])
    half = w['ffn_gate'].shape[1] // 2
    p['ffn_g'] = jnp.stack([w['ffn_gate'][:, :half], w['ffn_gate'][:, half:]]).astype(BF16)
    p['ffn_u'] = jnp.stack([w['ffn_up'][:, :half], w['ffn_up'][:, half:]]).astype(BF16)
    p['ffn_d'] = jnp.stack([w['ffn_down'][:half], w['ffn_down'][half:]]).astype(BF16)
    return p


def _prep_odd(w):
    row = lambda v: v.reshape(1, -1)
    p = {'w_in': jnp.pad(w['w_in'], ((0, 0), (0, OD_COLS_PAD - w['w_in'].shape[1]))).astype(BF16)}
    p['w_out'] = w['w_out'].astype(BF16)
    p['ln1_g'], p['ln1_b'], p['ln2_g'], p['ln2_b'] = row(w['ln1_g']), row(w['ln1_b']), row(w['ln2_g']), row(w['ln2_b'])
    p['router_w'] = jnp.pad(w['router'], ((0, 0), (0, LANES - N_EXPERTS)))
    p['router_b'] = jnp.concatenate([w['router_b'], jnp.full((LANES - N_EXPERTS,), -1e30, F32)])[None]
    p['e_gate'], p['e_up'], p['e_down'] = w['e_gate'].astype(BF16), w['e_up'].astype(BF16), w['e_down'].astype(BF16)
    return p


def _pick(n, pref):
    for c in pref:
        if n % c == 0:
            return c
    return n


def _even_layer(x, shift0, wkv0, conv0, c0, n0, m0, p):
    b, t, _ = x.shape
    m = b * t
    tm = _pick(m, (512, 256, 128, 64, 32, 16, 8))
    u = _matmul(x.reshape(m, D_MODEL), p['w_in'], tm, EV_COLS_PAD // 3).reshape(b, t, EV_COLS_PAD)

    new_shift = jnp.concatenate([u[:, t - 1:, 0:3072], u[:, t - 1:, EV_LORA:EV_LORA + A_LORA]], axis=-1)
    new_conv = jnp.concatenate([conv0, u[:, :, EV_QK:EV_QK + 1024]], axis=1)[:, t:]

    s0_main, s0_lora = shift0[:, :, 0:3072], shift0[:, :, 3072:A_COLS]
    tp = _pick(t, (256, 128, 64, 32, 16, 8))
    seq, g, bonus = _rwkv_pre(u, s0_main, s0_lora, p, tp)
    rows = b * A_HEADS // 2
    assert rows % LANES == 0, "batch must be a multiple of 16"
    s0 = wkv0.reshape(b, A_HEADS // 2, 2, A_HEAD_DIM, A_HEAD_DIM).transpose(2, 4, 3, 0, 1)
    s0 = s0.reshape(2, A_HEAD_DIM, A_HEAD_DIM, rows)
    tt = _pick(t, (16, 8))
    y, s_fin = _wkv_scan(seq.reshape(WKV_N_SRC, t, rows, LANES), s0, tt)
    y = y.reshape(t, b * A_WIDTH)
    new_wkv = s_fin.reshape(2, A_HEAD_DIM, A_HEAD_DIM, b, A_HEADS // 2).transpose(3, 4, 0, 2, 1)
    new_wkv = new_wkv.reshape(b, A_HEADS, A_HEAD_DIM, A_HEAD_DIM)

    ct0 = jnp.concatenate([c0.transpose(0, 1, 3, 2).reshape(b, B_QK_WIDTH, B_V_DIM),
                           n0.reshape(b, B_QK_WIDTH, 1),
                           jnp.zeros((b, B_QK_WIDTH, LANES - 1), F32)], axis=-1)
    m0p = jnp.pad(m0, ((0, 0), (0, LANES - B_HEADS)))[:, None, :]
    lm = _pick(t, (256, 128))
    if t % LANES:
        t_pad = -t % LANES
        u_b = jnp.pad(u, ((0, 0), (0, t_pad), (0, 0)))
        lm = LANES
    else:
        u_b = u
    yb, ct, m_out = _mlstm(u_b, conv0, ct0, m0p, p, lm, t)
    yb = yb[:, :t].reshape(m, B_V_WIDTH)
    new_c = ct[:, :, 0:B_V_DIM].reshape(b, B_HEADS, B_QK_DIM, B_V_DIM).transpose(0, 1, 3, 2)
    new_n = ct[:, :, B_V_DIM].reshape(b, B_HEADS, B_QK_DIM)
    new_m = m_out[:, 0, 0:B_HEADS]

    x2 = x.reshape(m, D_MODEL)
    x2 = _even_out(y, bonus, g, yb, x2, p, b, _pick(t, (256, 128, 64, 32, 16, 8)))
    x2 = _experts(x2, None, None, p['ffn_g'], p['ffn_u'], p['ffn_d'], p['ln2_g'], p['ln2_b'], tm, routed=False)
    return x2.reshape(b, t, D_MODEL), (new_shift, new_wkv, new_conv, new_c, new_n, new_m)


def _odd_layer(x, past_k, past_v, past_ki, p):
    b, t, _ = x.shape
    m = b * t
    tm = _pick(m, (512, 256, 128, 64, 32, 16, 8))
    u = _matmul(x.reshape(m, D_MODEL), p['w_in'], tm, OD_COLS_PAD).reshape(b, t, OD_COLS_PAD)
    k_new = u[:, :, OD_K:OD_K + 256]
    v_new = u[:, :, OD_V:OD_V + 256]
    ki_new = u[:, :, OD_KI:OD_KI + IDX_DIM]
    if past_k is None:
        keys_k, keys_v, keys_i = k_new, v_new, ki_new
        causal, limit, tq = True, 0, _pick(t, (2 * CHUNK, CHUNK))
        top_k = min(TOPK_MAX, t // 4)
    else:
        keys_k = jnp.concatenate([past_k.reshape(b, -1, 256), k_new], axis=1)
        keys_v = jnp.concatenate([past_v.reshape(b, -1, 256), v_new], axis=1)
        keys_i = jnp.concatenate([past_ki, ki_new], axis=1)
        limit = keys_k.shape[1]
        causal, tq = False, t
        top_k = min(TOPK_MAX, limit // 4)
    s_pad = -keys_k.shape[1] % LANES
    pad_s = lambda z: jnp.pad(z, ((0, 0), (0, s_pad), (0, 0))) if s_pad else z
    kt = pad_s(keys_k).transpose(0, 2, 1).astype(BF16)
    vv = pad_s(keys_v).astype(BF16).reshape(b, -1, C_KV_HEADS, C_HEAD_DIM)
    vv = jnp.concatenate([vv, jnp.ones_like(vv)], axis=-1).reshape(b, -1, 2 * C_KV_HEADS * C_HEAD_DIM)
    kit = jnp.pad(pad_s(keys_i), ((0, 0), (0, 0), (0, LANES - IDX_DIM))).transpose(0, 2, 1).astype(BF16)
    qi = u[:, :, OD_QI:OD_QI + IDX_HEADS * IDX_DIM].reshape(b, t, IDX_HEADS, IDX_DIM).transpose(0, 2, 1, 3)
    qi = jnp.pad(qi, ((0, 0), (0, 0), (0, 0), (0, LANES - IDX_DIM))).astype(BF16)
    att = _dsa(u, qi, kt, vv, kit, tq, top_k, causal, limit)
    x2 = _proj_ln(att.reshape(m, C_WIDTH), x.reshape(m, D_MODEL), p['w_out'], p['ln1_g'], p['ln1_b'],
                  _pick(m, (512, 256, 128, 64, 32, 16, 8)))
    x2 = _experts(x2, p['router_w'], p['router_b'], p['e_gate'], p['e_up'], p['e_down'],
                  p['ln2_g'], p['ln2_b'], _pick(m, (1024, 512, 256, 128)), routed=True)
    st = (k_new.reshape(b, t, C_KV_HEADS, C_HEAD_DIM), v_new.reshape(b, t, C_KV_HEADS, C_HEAD_DIM), ki_new)
    return x2.reshape(b, t, D_MODEL), st


def _trunk(x, shift0, wkv0, conv0, c0, n0, m0, past_k, past_v, past_ki, ev, od):
    even_out = [[] for _ in range(6)]
    odd_out = [[] for _ in range(3)]
    for layer in range(DEPTH):
        i = layer // 2
        if layer % 2 == 0:
            x, st = _even_layer(x, shift0[i], wkv0[i], conv0[i], c0[i], n0[i], m0[i], ev[i])
            for acc, val in zip(even_out, st):
                acc.append(val)
        else:
            if past_k is None:
                x, st = _odd_layer(x, None, None, None, od[i])
            else:
                x, st = _odd_layer(x, past_k[i], past_v[i], past_ki[i], od[i])
            for acc, val in zip(odd_out, st):
                acc.append(val)
    return x, [jnp.stack(v) for v in even_out], [jnp.stack(v) for v in odd_out]


def kernel(x_prompt, x_sample, state_shift, state_wkv, state_conv, state_c, state_n, state_m,
           cache_k, cache_v, cache_idx_k,
           ev_w_in, a_mu, a_w0, a_w2, a_a0, a_a2, a_g2, a_k_k, a_k_a, a_r_k, a_ln_g, a_ln_b,
           b_conv_w, b_conv_b, b_i_bias, b_f_bias, b_ln_g, b_ln_b, ev_w_out, ev_ln1_g, ev_ln1_b,
           ffn_w_gate, ffn_w_up, ffn_w_down, ev_ln2_g, ev_ln2_b,
           od_w_in, od_w_out, od_ln1_g, od_ln1_b, moe_w_router, moe_b_router,
           moe_w_gate, moe_w_up, moe_w_down, od_ln2_g, od_ln2_b):
    ew = {'w_in': ev_w_in, 'mu': a_mu, 'w0': a_w0, 'w2': a_w2, 'a0': a_a0, 'a2': a_a2, 'g2': a_g2,
          'k_k': a_k_k, 'k_a': a_k_a, 'r_k': a_r_k, 'a_ln_g': a_ln_g, 'a_ln_b': a_ln_b,
          'conv_w': b_conv_w, 'conv_b': b_conv_b, 'b_i': b_i_bias, 'b_f': b_f_bias,
          'b_ln_g': b_ln_g, 'b_ln_b': b_ln_b, 'w_out': ev_w_out, 'ln1_g': ev_ln1_g, 'ln1_b': ev_ln1_b,
          'ffn_gate': ffn_w_gate, 'ffn_up': ffn_w_up, 'ffn_down': ffn_w_down,
          'ln2_g': ev_ln2_g, 'ln2_b': ev_ln2_b}
    ow = {'w_in': od_w_in, 'w_out': od_w_out, 'ln1_g': od_ln1_g, 'ln1_b': od_ln1_b,
          'router': moe_w_router, 'router_b': moe_b_router, 'e_gate': moe_w_gate, 'e_up': moe_w_up,
          'e_down': moe_w_down, 'ln2_g': od_ln2_g, 'ln2_b': od_ln2_b}
    n_even, n_odd = ev_w_in.shape[0], od_w_in.shape[0]
    ev = [_prep_even({k: v[i] for k, v in ew.items()}) for i in range(n_even)]
    od = [_prep_odd({k: v[i] for k, v in ow.items()}) for i in range(n_odd)]

    bp = x_prompt.shape[0]
    z = functools.partial(jnp.zeros, dtype=F32)
    y_p, ep, op = _trunk(x_prompt,
                         z((n_even, bp, 1, A_COLS)), z((n_even, bp, A_HEADS, A_HEAD_DIM, A_HEAD_DIM)),
                         z((n_even, bp, 3, 2 * B_QK_WIDTH)), z((n_even, bp, B_HEADS, B_V_DIM, B_QK_DIM)),
                         z((n_even, bp, B_HEADS, B_QK_DIM)), z((n_even, bp, B_HEADS)),
                         None, None, None, ev, od)
    y_s, es, os_ = _trunk(x_sample, state_shift, state_wkv, state_conv, state_c, state_n, state_m,
                          cache_k, cache_v, cache_idx_k, ev, od)
    return (y_p, y_s,
            ep[0], es[0], ep[1], es[1], ep[2], es[2], ep[3], es[3], ep[4], es[4], ep[5], es[5],
            op[0], os_[0], op[1], os_[1], op[2], os_[2])
```

```python
import functools
import math

import jax
import jax.numpy as jnp
from jax import lax
from jax.experimental import pallas as pl
from jax.experimental.pallas import tpu as pltpu

F32 = jnp.float32
BF16 = jnp.bfloat16
I32 = jnp.int32

D_MODEL = 1024
DEPTH = 4
CHUNK = 64
A_HEADS, A_HEAD_DIM, A_WIDTH = 16, 64, 1024
A_LORA = 256
A_COLS = 3 * A_WIDTH + A_LORA
B_HEADS, B_QK_DIM, B_V_DIM = 8, 64, 128
B_QK_WIDTH, B_V_WIDTH = 512, 1024
C_HEADS, C_KV_HEADS, C_HEAD_DIM, C_GROUP = 8, 2, 128, 4
C_WIDTH = 1024
IDX_HEADS, IDX_DIM = 8, 64
TOPK_MAX = 256
N_EXPERTS = 8
D_FF_EXPERT = 1408
ALPHA = (2 * DEPTH) ** 0.25
LN_EPS = 1e-5

LANES = 128
SUBLANES = 8
VMEM_LIMIT = 56 * 1024 * 1024

EV_R, EV_K, EV_V, EV_QK, EV_VB, EV_O, EV_LORA, EV_GATE = 0, 1024, 2048, 3072, 4096, 5120, 6144, 6400
EV_COLS_PAD = 6528
OD_Q, OD_K, OD_V, OD_QI, OD_KI = 0, 1024, 1280, 1536, 2048
OD_COLS_PAD = 2176

INT_MIN = -2 ** 31


def _cparams(*sem):
    return pltpu.CompilerParams(dimension_semantics=sem, vmem_limit_bytes=VMEM_LIMIT)


def _layer_norm(z, g, b):
    mu = jnp.mean(z, axis=-1, keepdims=True)
    d = z - mu
    var = jnp.mean(d * d, axis=-1, keepdims=True)
    return d * lax.rsqrt(var + LN_EPS) * g + b


def _segsum(x, e):
    hi = x.astype(BF16)
    lo = (x - hi.astype(F32)).astype(BF16)
    return jnp.dot(hi, e, preferred_element_type=F32) + jnp.dot(lo, e, preferred_element_type=F32)


def _mm_kernel(x_ref, w_ref, o_ref):
    o_ref[...] = jnp.dot(x_ref[...].astype(BF16), w_ref[...], preferred_element_type=F32)


def _matmul(x, w, tm, tn):
    m, k = x.shape
    n = w.shape[1]
    return pl.pallas_call(
        _mm_kernel,
        grid=(n // tn, m // tm),
        in_specs=[pl.BlockSpec((tm, k), lambda j, i: (i, 0)),
                  pl.BlockSpec((k, tn), lambda j, i: (0, j))],
        out_specs=pl.BlockSpec((tm, tn), lambda j, i: (i, j)),
        out_shape=jax.ShapeDtypeStruct((m, n), F32),
        compiler_params=_cparams("parallel", "parallel"),
        name="proj_in",
    )(x, w)


def _rwkv_pre_kernel(ur, uk, uv, ul, s0m, s0l, mum, mul_, w0, a0, kk_, ka_, rk_, w2, a2, g2, e_ref,
                     o_seq, o_g, o_bonus, prev_m, prev_l):
    @pl.when(pl.program_id(1) == 0)
    def _():
        prev_m[...] = s0m[0]
        prev_l[...] = s0l[0]

    tm = ur.shape[1]
    row = lax.broadcasted_iota(I32, (tm, 1), 0)

    def lerp(x, prev, mu):
        shifted = jnp.where(row == 0, prev, pltpu.roll(x, 1, 0))
        return x + (shifted - x) * mu

    r_raw, k_raw, v_raw, l_raw = ur[0], uk[0], uv[0], ul[0]
    r = lerp(r_raw, prev_m[:, 0:1024], mum[:, 0:1024])
    k = lerp(k_raw, prev_m[:, 1024:2048], mum[:, 1024:2048])
    v = lerp(v_raw, prev_m[:, 2048:3072], mum[:, 2048:3072])
    lo = lerp(l_raw, prev_l[...], mul_[...])
    prev_m[:, 0:1024] = r_raw[tm - 1:tm, :]
    prev_m[:, 1024:2048] = k_raw[tm - 1:tm, :]
    prev_m[:, 2048:3072] = v_raw[tm - 1:tm, :]
    prev_l[...] = l_raw[tm - 1:tm, :]

    zw = w0[...] + jnp.dot(jnp.tanh(lo).astype(BF16), w2[...], preferred_element_type=F32)
    decay = jnp.exp(-(math.exp(-0.5) * jax.nn.sigmoid(zw)))
    a = jax.nn.sigmoid(a0[...] + jnp.dot(lo.astype(BF16), a2[...], preferred_element_type=F32))
    g = jnp.dot(jax.nn.sigmoid(lo).astype(BF16), g2[...], preferred_element_type=F32)

    e = e_ref[...]
    kk = k * kk_[...]
    kk = kk * lax.rsqrt(jnp.maximum(_segsum(kk * kk, e), 1e-24))
    k2 = k * (1.0 + (a - 1.0) * ka_[...])
    bonus = _segsum(r * k2 * rk_[...], e) * v

    o_seq[WKV_SRC_R] = r
    o_seq[WKV_SRC_W] = decay
    o_seq[WKV_SRC_K] = k2
    o_seq[WKV_SRC_V] = v
    o_seq[WKV_SRC_A] = -kk
    o_seq[WKV_SRC_B] = kk * a
    o_g[...] = g
    o_bonus[...] = bonus


def _rwkv_pre(u, s0_main, s0_lora, p, tm):
    b, t, _ = u.shape
    blk = lambda w, c: pl.BlockSpec((1, tm, w), lambda i, j, c=c: (i, j, c))
    full = lambda arr: pl.BlockSpec(arr.shape, lambda i, j: (0,) * arr.ndim)
    params = [p['mu_main'], p['mu_lora'], p['w0'], p['a0'], p['k_k'], p['k_a'], p['r_k'],
              p['w2p'], p['a2p'], p['g2p'], p['seg64']]
    out = jax.ShapeDtypeStruct((t, b * A_WIDTH), F32)
    return pl.pallas_call(
        _rwkv_pre_kernel,
        grid=(b, t // tm),
        in_specs=[blk(1024, EV_R // 1024), blk(1024, EV_K // 1024), blk(1024, EV_V // 1024),
                  blk(A_LORA, EV_LORA // A_LORA),
                  pl.BlockSpec((1, 1, 3072), lambda i, j: (i, 0, 0)),
                  pl.BlockSpec((1, 1, A_LORA), lambda i, j: (i, 0, 0))] + [full(a) for a in params],
        out_specs=[pl.BlockSpec((WKV_N_SRC, tm, A_WIDTH), lambda i, j: (0, j, i)),
                   pl.BlockSpec((tm, A_WIDTH), lambda i, j: (j, i)),
                   pl.BlockSpec((tm, A_WIDTH), lambda i, j: (j, i))],
        out_shape=[jax.ShapeDtypeStruct((WKV_N_SRC, t, b * A_WIDTH), F32), out, out],
        scratch_shapes=[pltpu.VMEM((1, 3072), F32), pltpu.VMEM((1, A_LORA), F32)],
        compiler_params=_cparams("parallel", "arbitrary"),
        name="rwkv_pre",
    )(u, u, u, u, s0_main, s0_lora, *params)


WKV_N_SRC = 6
WKV_SRC_R, WKV_SRC_W, WKV_SRC_K, WKV_SRC_V, WKV_SRC_A, WKV_SRC_B = range(WKV_N_SRC)
WKV_VGROUP = 4
WKV_KCHUNK = 32


def _wkv_kernel(seq_ref, s0_ref, y_ref, s_ref, tr, sa_buf, ybuf):
    @pl.when(pl.program_id(1) == 0)
    def _():
        s_ref[...] = s0_ref[...]

    n_t = seq_ref.shape[1]
    d = A_HEAD_DIM
    vspan = WKV_VGROUP * SUBLANES
    groups = [(p, g) for p in range(2) for g in range(d // vspan)]
    n_kc = d // WKV_KCHUNK

    def vrows(g, j):
        return slice(g * vspan + j * SUBLANES, g * vspan + (j + 1) * SUBLANES)

    def transpose_tile(i, tok):
        tr[i, tok] = jnp.transpose(seq_ref[i, tok])

    for i in range(WKV_N_SRC):
        transpose_tile(i, 0)
    transpose_tile(WKV_SRC_A, min(1, n_t - 1))

    for p, g in groups:
        acc = [jnp.zeros((SUBLANES, LANES), F32) for _ in range(WKV_VGROUP)]
        for kx in range(d):
            ab = tr[WKV_SRC_A, 0, p * d + kx:p * d + kx + 1, :]
            for j in range(WKV_VGROUP):
                acc[j] = acc[j] + s_ref[p, kx, vrows(g, j), :] * ab
        for j in range(WKV_VGROUP):
            sa_buf[p, vrows(g, j), :] = acc[j]

    def step(t, carry):
        tn = jnp.minimum(t + 1, n_t - 1)
        for gi, (p, g) in enumerate(groups):
            vv = [tr[WKV_SRC_V, t, p * d + g * vspan + j * SUBLANES:p * d + g * vspan + (j + 1) * SUBLANES, :]
                  for j in range(WKV_VGROUP)]
            sa = [sa_buf[p, vrows(g, j), :] for j in range(WKV_VGROUP)]
            zero = tuple(jnp.zeros((SUBLANES, LANES), F32) for _ in range(2 * WKV_VGROUP))

            def key_chunk(kc, accs, gi=gi, p=p, g=g, vv=vv, sa=sa):
                yacc, san = list(accs[:WKV_VGROUP]), list(accs[WKV_VGROUP:])
                for i in range(WKV_KCHUNK):
                    kx = kc * WKV_KCHUNK + i
                    row = pl.ds(p * d + kx, 1)
                    wb = tr[WKV_SRC_W, t, row, :]
                    bb = tr[WKV_SRC_B, t, row, :]
                    kb = tr[WKV_SRC_K, t, row, :]
                    rb = tr[WKV_SRC_R, t, row, :]
                    an = tr[WKV_SRC_A, tn, row, :]
                    for j in range(WKV_VGROUP):
                        n = s_ref[p, kx, vrows(g, j), :] * wb + sa[j] * bb + vv[j] * kb
                        s_ref[p, kx, vrows(g, j), :] = n
                        yacc[j] = yacc[j] + n * rb
                        san[j] = san[j] + n * an
                tile = jnp.minimum(gi * n_kc + kc, WKV_N_SRC - 1)
                ahead = jnp.where(tile == WKV_SRC_A, 2, 1)
                transpose_tile(tile, jnp.minimum(t + ahead, n_t - 1))
                return tuple(yacc) + tuple(san)

            accs = lax.fori_loop(0, n_kc, key_chunk, zero)
            yacc, san = accs[:WKV_VGROUP], accs[WKV_VGROUP:]
            for j in range(WKV_VGROUP):
                ybuf[t, p * d + g * vspan + j * SUBLANES:p * d + g * vspan + (j + 1) * SUBLANES, :] = yacc[j]
                sa_buf[p, vrows(g, j), :] = san[j]
        return carry

    lax.fori_loop(0, n_t, step, 0)

    def transpose_out(t, carry):
        y_ref[t] = jnp.transpose(ybuf[t])
        return carry

    lax.fori_loop(0, n_t, transpose_out, 0, unroll=4)


def _wkv_scan(seq, s0, tt):
    _, t, rows, _ = seq.shape
    d = A_HEAD_DIM
    st = pl.BlockSpec((2, d, d, LANES), lambda g, i: (0, 0, 0, g))
    return pl.pallas_call(
        _wkv_kernel,
        grid=(rows // LANES, t // tt),
        in_specs=[pl.BlockSpec((WKV_N_SRC, tt, LANES, LANES), lambda g, i: (0, i, g, 0)), st],
        out_specs=[pl.BlockSpec((tt, LANES, LANES), lambda g, i: (i, g, 0)), st],
        out_shape=[jax.ShapeDtypeStruct((t, rows, LANES), F32), jax.ShapeDtypeStruct((2, d, d, rows), F32)],
        scratch_shapes=[pltpu.VMEM((WKV_N_SRC, tt, LANES, LANES), F32), pltpu.VMEM((2, d, LANES), F32),
                        pltpu.VMEM((tt, LANES, LANES), F32)],
        compiler_params=_cparams("parallel", "arbitrary"),
        name="wkv_scan",
    )(seq, s0)


def _mlstm_kernel(uqk, uv, uo, ug, conv0, ct0, m0, cw, cb, gb, lng, lnb,
                  o_y, o_ct, o_m, ct, m_scr, carry, *, t_valid):
    tb = pl.program_id(1)

    @pl.when(tb == 0)
    def _():
        ct[...] = ct0[0]
        m_scr[...] = m0[0]
        carry[...] = conv0[0]

    L = uqk.shape[1]
    row = lax.broadcasted_iota(I32, (L, 1), 0)
    x = uqk[0]

    def shifted(j):
        out = pltpu.roll(x, j, 0)
        for i in range(j):
            out = jnp.where(row == i, carry[3 + i - j:4 + i - j, :], out)
        return out

    conv = x * cw[3:4, :] + shifted(1) * cw[2:3, :] + shifted(2) * cw[1:2, :] + shifted(3) * cw[0:1, :] + cb[...]
    carry[...] = x[L - 3:L, :]
    qk = conv * jax.nn.sigmoid(conv)
    q_all = qk[:, 0:B_QK_WIDTH]
    k_t = jnp.transpose(qk[:, B_QK_WIDTH:2 * B_QK_WIDTH] * (B_QK_DIM ** -0.5))

    lane = lax.broadcasted_iota(I32, (L, LANES), 1)
    gz = ug[0] + gb[...]
    lg = jnp.where(lane < B_HEADS, gz, jnp.minimum(gz, 0.0) - jnp.log(1.0 + jnp.exp(-jnp.abs(gz))))
    valid = (row + tb * L) < t_valid
    lg = jnp.where(valid, lg, jnp.where(lane < B_HEADS, -jnp.inf, 0.0))
    lg_t = jnp.transpose(lg)
    ti = lax.broadcasted_iota(I32, (L, L), 0)
    si = lax.broadcasted_iota(I32, (L, L), 1)
    tril = ti >= si
    tril_f = tril.astype(F32)
    triu_f = (ti <= si).astype(F32)
    lg_fin = jnp.where(lane < B_HEADS, 0.0, lg)
    bcum = jnp.dot(tril_f, lg_fin, preferred_element_type=F32, precision=lax.Precision.HIGHEST)
    bcum_t = jnp.dot(jnp.transpose(lg_fin), triu_f, preferred_element_type=F32,
                     precision=lax.Precision.HIGHEST)

    head_q = lax.shift_right_logical(lax.broadcasted_iota(I32, (L, B_QK_WIDTH), 1), 6)
    lane1 = lax.broadcasted_iota(I32, (1, LANES), 1)
    ones_col = (lax.broadcasted_iota(I32, (L, LANES), 1) == 0).astype(F32)
    m_vec = m_scr[...]
    m_new_vec = m_vec
    ct_old = ct[...]
    ct_old_bf = ct_old.astype(BF16)
    vv = uv[0]
    oo = uo[0]
    for h in range(B_HEADS):
        hs = slice(h * B_QK_DIM, (h + 1) * B_QK_DIM)
        vs = slice(h * B_V_DIM, (h + 1) * B_V_DIM)
        bc_col = bcum[:, B_HEADS + h:B_HEADS + h + 1]
        bc_row = bcum_t[B_HEADS + h:B_HEADS + h + 1, :]
        ic_row = lg_t[h:h + 1, :]
        gtot = bc_row[:, L - 1:L]
        m_prev = m_vec[:, h:h + 1]
        dmat = jnp.where(tril, bc_col - bc_row + ic_row, -jnp.inf)
        inter = bc_col + m_prev
        m_t = jnp.maximum(inter, jnp.max(dmat, axis=1, keepdims=True))
        w_intra = jnp.exp(dmat - m_t)
        w_inter = jnp.exp(inter - m_t)
        q_m = jnp.where(head_q == h, q_all, 0.0).astype(BF16)
        s = jnp.dot(q_m, k_t.astype(BF16), preferred_element_type=F32) * w_intra
        v_ext = jnp.concatenate([vv[:, vs], ones_col], axis=1).astype(BF16)
        intra = jnp.dot(s.astype(BF16), v_ext, preferred_element_type=F32)
        cross = jnp.dot(q_m, ct_old_bf, preferred_element_type=F32)
        num = intra[:, 0:B_V_DIM] + w_inter * cross[:, 0:B_V_DIM]
        den = jnp.sum(s, axis=1, keepdims=True) + w_inter * cross[:, B_V_DIM:B_V_DIM + 1]
        hh = num / jnp.maximum(jnp.abs(den), jnp.exp(-m_t))
        mu = jnp.mean(hh, axis=-1, keepdims=True)
        dd = hh - mu
        var = jnp.mean(dd * dd, axis=-1, keepdims=True)
        yn = dd * lax.rsqrt(var + LN_EPS) * lng[:, vs] + lnb[:, vs]
        o_y[0, :, vs] = jax.nn.sigmoid(oo[:, vs]) * yn
        lw = gtot - bc_row + ic_row
        m_new = jnp.maximum(gtot + m_prev, jnp.max(lw, axis=1, keepdims=True))
        w_s = jnp.exp(lw - m_new)
        dec = jnp.exp(gtot + m_prev - m_new)
        kw = k_t[hs, :] * w_s
        kw_hi = kw.astype(BF16)
        kw_lo = (kw - kw_hi.astype(F32)).astype(BF16)
        upd = (jnp.dot(kw_hi, v_ext, preferred_element_type=F32)
               + jnp.dot(kw_lo, v_ext, preferred_element_type=F32))
        ct[hs, :] = dec * ct_old[hs, :] + upd
        m_new_vec = jnp.where(lane1 == h, m_new, m_new_vec)
    m_scr[...] = m_new_vec

    @pl.when(tb == pl.num_programs(1) - 1)
    def _():
        o_ct[0] = ct[...]
        o_m[0] = m_scr[...]


def _mlstm(u, conv0, ct0, m0, p, L, t_valid):
    b, t, _ = u.shape
    blk = lambda w, c: pl.BlockSpec((1, L, w), lambda i, j, c=c: (i, j, c))
    full = lambda arr: pl.BlockSpec(arr.shape, lambda i, j: (0,) * arr.ndim)
    per_b = lambda arr: pl.BlockSpec((1,) + arr.shape[1:], lambda i, j: (i, 0, 0))
    params = [p['conv_w'], p['conv_b'], p['gate_b'], p['b_ln_g'], p['b_ln_b']]
    return pl.pallas_call(
        functools.partial(_mlstm_kernel, t_valid=t_valid),
        grid=(b, t // L),
        in_specs=[blk(1024, EV_QK // 1024), blk(1024, EV_VB // 1024), blk(1024, EV_O // 1024),
                  blk(LANES, EV_GATE // LANES), per_b(conv0), per_b(ct0), per_b(m0)] + [full(a) for a in params],
        out_specs=[pl.BlockSpec((1, L, B_V_WIDTH), lambda i, j: (i, j, 0)),
                   pl.BlockSpec((1, B_QK_WIDTH, 2 * LANES), lambda i, j: (i, 0, 0)),
                   pl.BlockSpec((1, 1, LANES), lambda i, j: (i, 0, 0))],
        out_shape=[jax.ShapeDtypeStruct((b, t, B_V_WIDTH), F32),
                   jax.ShapeDtypeStruct((b, B_QK_WIDTH, 2 * LANES), F32),
                   jax.ShapeDtypeStruct((b, 1, LANES), F32)],
        scratch_shapes=[pltpu.VMEM((B_QK_WIDTH, 2 * LANES), F32), pltpu.VMEM((1, LANES), F32),
                        pltpu.VMEM((3, 2 * B_QK_WIDTH), F32)],
        compiler_params=_cparams("parallel", "arbitrary"),
        name="mlstm",
    )(u, u, u, u, conv0, ct0, m0, *params)


def _even_out_kernel(y_ref, bonus_ref, g_ref, yb_ref, x_ref, lng, lnb, e_ref, woa, wob, g1, b1, o_ref):
    e = e_ref[...]
    y = y_ref[...]
    mu = _segsum(y, e) * (1.0 / A_HEAD_DIM)
    d = y - mu
    var = _segsum(d * d, e) * (1.0 / A_HEAD_DIM)
    yn = d * lax.rsqrt(var + LN_EPS) * lng[...] + lnb[...]
    ya = (yn + bonus_ref[...]) * g_ref[...]
    mix = (jnp.dot(ya.astype(BF16), woa[...], preferred_element_type=F32)
           + jnp.dot(yb_ref[...].astype(BF16), wob[...], preferred_element_type=F32))
    o_ref[...] = _layer_norm(ALPHA * x_ref[...] + mix, g1[...], b1[...])


def _even_out(y, bonus, g, yb, x, p, b, tm):
    m = x.shape[0]
    nt = m // b // tm
    tmaj = pl.BlockSpec((tm, D_MODEL), lambda i, j: (j, i))
    row = pl.BlockSpec((tm, D_MODEL), lambda i, j: (i * nt + j, 0))
    full = lambda arr: pl.BlockSpec(arr.shape, lambda i, j: (0,) * arr.ndim)
    params = [p['a_ln_g'], p['a_ln_b'], p['seg64'], p['w_out_a'], p['w_out_b'], p['ln1_g'], p['ln1_b']]
    return pl.pallas_call(
        _even_out_kernel,
        grid=(b, nt),
        in_specs=[tmaj] * 3 + [row] * 2 + [full(a) for a in params],
        out_specs=row,
        out_shape=jax.ShapeDtypeStruct((m, D_MODEL), F32),
        compiler_params=_cparams("parallel", "parallel"),
        name="even_out",
    )(y, bonus, g, yb, x, *params)


def _proj_ln_kernel(a_ref, x_ref, w_ref, g1, b1, o_ref):
    mix = jnp.dot(a_ref[...].astype(BF16), w_ref[...], preferred_element_type=F32)
    o_ref[...] = _layer_norm(ALPHA * x_ref[...] + mix, g1[...], b1[...])


def _proj_ln(a, x, w, g1, b1, tm):
    m = x.shape[0]
    row = pl.BlockSpec((tm, D_MODEL), lambda i: (i, 0))
    full = lambda arr: pl.BlockSpec(arr.shape, lambda i: (0,) * arr.ndim)
    return pl.pallas_call(
        _proj_ln_kernel,
        grid=(m // tm,),
        in_specs=[row, row, full(w), full(g1), full(b1)],
        out_specs=row,
        out_shape=jax.ShapeDtypeStruct((m, D_MODEL), F32),
        compiler_params=_cparams("parallel"),
        name="proj_ln",
    )(a, x, w, g1, b1)


def _swiglu(xb, wg, wu, wd):
    hg = jnp.dot(xb, wg, preferred_element_type=F32)
    hu = jnp.dot(xb, wu, preferred_element_type=F32)
    hid = (hg * jax.nn.sigmoid(hg)) * hu
    return jnp.dot(hid.astype(BF16), wd, preferred_element_type=F32)


def _ffn_kernel(x_ref, wg, wu, wd, g2, b2, o_ref, xb):
    e = pl.program_id(1)

    @pl.when(e == 0)
    def _():
        xb[...] = x_ref[...].astype(BF16)
        o_ref[...] = jnp.zeros_like(o_ref)

    o_ref[...] += _swiglu(xb[...], wg[0], wu[0], wd[0])

    @pl.when(e == pl.num_programs(1) - 1)
    def _():
        o_ref[...] = _layer_norm(ALPHA * x_ref[...] + o_ref[...], g2[...], b2[...])


MOE_CHUNK = 256


def _moe_kernel(x_ref, wr, br, wg, wu, wd, g2, b2, o_ref, xb, comb, pos_scr, post_scr):
    e = pl.program_id(1)
    tm = x_ref.shape[0]

    @pl.when(e == 0)
    def _():
        x = x_ref[...]
        xb[...] = x.astype(BF16)
        o_ref[...] = jnp.zeros_like(o_ref)
        logits = jnp.dot(x, wr[...], preferred_element_type=F32, precision=lax.Precision.HIGHEST) + br[...]
        lane = lax.broadcasted_iota(I32, logits.shape, 1).astype(F32)
        v1 = jnp.max(logits, axis=-1, keepdims=True)
        i1 = jnp.min(jnp.where(logits == v1, lane, float(LANES)), axis=-1, keepdims=True)
        rest = jnp.where(lane == i1, -jnp.inf, logits)
        v2 = jnp.max(rest, axis=-1, keepdims=True)
        i2 = jnp.min(jnp.where(rest == v2, lane, float(LANES)), axis=-1, keepdims=True)
        e2 = jnp.exp(v2 - v1)
        den = 1.0 + e2
        comb[...] = jnp.where(lane == i1, 1.0 / den, jnp.where(lane == i2, e2 / den, 0.0))
        member = jnp.where(lane == i1, 1.0, jnp.where(lane == i2, 1.0, 0.0))
        earlier = (lax.broadcasted_iota(I32, (tm, tm), 0) > lax.broadcasted_iota(I32, (tm, tm), 1))
        pos = jnp.dot(jnp.where(earlier, 1.0, 0.0).astype(BF16), member.astype(BF16), preferred_element_type=F32)
        posm = jnp.where(member > 0.0, pos, -1.0)
        pos_scr[...] = posm
        post_scr[...] = jnp.transpose(posm)

    sel = lax.broadcasted_iota(I32, (tm, LANES), 1) == e
    gate = jnp.sum(jnp.where(sel, comb[...], 0.0), axis=-1, keepdims=True)
    slot_col = jnp.sum(jnp.where(sel, pos_scr[...], 0.0), axis=-1, keepdims=True)
    slot_row = post_scr[pl.ds(e, 1), :]
    count = jnp.max(slot_row) + 1.0

    def run_slots(base, n):
        rows = lax.broadcasted_iota(I32, (n, 1), 0).astype(F32) + float(base)
        gather = jnp.where(slot_row == rows, 1.0, 0.0).astype(BF16)
        xg = jnp.dot(gather, xb[...], preferred_element_type=F32).astype(BF16)
        y = _swiglu(xg, wg[0], wu[0], wd[0])
        cols = lax.broadcasted_iota(I32, (1, n), 1).astype(F32) + float(base)
        scatter = jnp.where(slot_col == cols, 1.0, 0.0).astype(BF16)
        o_ref[...] += gate * jnp.dot(scatter, y.astype(BF16), preferred_element_type=F32)

    chunk = min(MOE_CHUNK, tm)
    half = chunk // 2
    for c in range(tm // chunk):
        lo = c * chunk

        @pl.when(count > lo + half)
        def _(lo=lo):
            run_slots(lo, chunk)

        @pl.when(jnp.logical_and(count > lo, count <= lo + half))
        def _(lo=lo):
            run_slots(lo, half)

    @pl.when(e == pl.num_programs(1) - 1)
    def _():
        o_ref[...] = _layer_norm(ALPHA * x_ref[...] + o_ref[...], g2[...], b2[...])


def _experts(x, wr, br, wg, wu, wd, g2, b2, tm, routed):
    m = x.shape[0]
    ne, _, dff = wg.shape
    row = pl.BlockSpec((tm, D_MODEL), lambda i, e: (i, 0))
    full = lambda arr: pl.BlockSpec(arr.shape, lambda i, e: (0,) * arr.ndim)
    wspecs = [pl.BlockSpec((1, D_MODEL, dff), lambda i, e: (e, 0, 0)),
              pl.BlockSpec((1, D_MODEL, dff), lambda i, e: (e, 0, 0)),
              pl.BlockSpec((1, dff, D_MODEL), lambda i, e: (e, 0, 0))]
    common = dict(grid=(m // tm, ne), out_specs=row, out_shape=jax.ShapeDtypeStruct((m, D_MODEL), F32),
                  compiler_params=_cparams("parallel", "arbitrary"))
    if not routed:
        return pl.pallas_call(
            _ffn_kernel, in_specs=[row] + wspecs + [full(g2), full(b2)],
            scratch_shapes=[pltpu.VMEM((tm, D_MODEL), BF16)], name="experts_dense", **common,
        )(x, wg, wu, wd, g2, b2)
    return pl.pallas_call(
        _moe_kernel, in_specs=[row, full(wr), full(br)] + wspecs + [full(g2), full(b2)],
        scratch_shapes=[pltpu.VMEM((tm, D_MODEL), BF16), pltpu.VMEM((tm, LANES), F32),
                        pltpu.VMEM((tm, LANES), F32), pltpu.VMEM((LANES, tm), F32)],
        name="experts_routed", **common,
    )(x, wr, br, wg, wu, wd, g2, b2)


DSA_COARSE_BITS = 28


def _dsa_kernel(q_ref, qi_ref, wi_ref, kt_ref, v_ref, kit_ref, *rest, top_k, causal, limit_const, blk_off, has_prev):
    o_ref, key_scr, jcut_scr, thr_scr = rest[1:] if has_prev else rest
    tq = q_ref.shape[1]
    s_len = kt_ref.shape[2]
    if causal:
        row_chunk = lax.shift_right_logical(lax.broadcasted_iota(I32, (tq, 1), 0), CHUNK.bit_length() - 1)
        chunk = (pl.program_id(1) + blk_off) * (tq // CHUNK) + row_chunk
        limit = (chunk + 1) * CHUNK
        kth = jnp.minimum(top_k, limit).astype(F32)
    else:
        limit = limit_const
        kth = jnp.full((tq, 1), min(top_k, limit_const), F32)
    idx = lax.broadcasted_iota(I32, (tq, s_len), 1)

    kit = kit_ref[0]
    wi = wi_ref[0] * (IDX_HEADS ** -0.5)
    score = jnp.zeros((tq, s_len), F32)
    for h in range(IDX_HEADS):
        rel = jnp.dot(qi_ref[0, h], kit, preferred_element_type=F32)
        score = score + wi[:, IDX_DIM + h:IDX_DIM + h + 1] * jnp.maximum(rel, 0.0)
    score = jnp.where(score == 0.0, 0.0, score)
    bits = pltpu.bitcast(score, I32)
    key = jnp.where(bits < 0, bits ^ 0x7FFFFFFF, bits)
    key_scr[...] = jnp.where(idx < limit, key, INT_MIN)

    n_grp = 2 if tq % (4 * SUBLANES) == 0 else 1
    rg = tq // n_grp
    rows_of = lambda g: slice(g * rg, (g + 1) * rg)

    def count_ge(thrs):
        return [jnp.sum(jnp.where(key_scr[rows_of(g), :] >= thrs[g], 1.0, 0.0), axis=1, keepdims=True)
                for g in range(n_grp)]

    def bit_step(i, thrs):
        bit = lax.shift_left(jnp.int32(1), 31 - i)
        cands = [t + bit for t in thrs]
        cnts = count_ge(cands)
        return tuple(jnp.where(cnts[g] >= kth[rows_of(g)], cands[g], thrs[g]) for g in range(n_grp))

    thrs = lax.fori_loop(0, DSA_COARSE_BITS, bit_step,
                         tuple(jnp.full((rg, 1), INT_MIN, I32) for _ in range(n_grp)), unroll=4)
    thr_scr[...] = jnp.concatenate(thrs, axis=0)
    n_coarse = jnp.concatenate(count_ge(thrs), axis=0)

    @pl.when(jnp.max(n_coarse - kth) > 0.0)
    def _():
        fine = lax.fori_loop(DSA_COARSE_BITS, 32, bit_step, tuple(thr_scr[rows_of(g), :] for g in range(n_grp)),
                             unroll=4)
        thr_scr[...] = jnp.concatenate(fine, axis=0)

    thr = thr_scr[...]
    keyv = key_scr[...]
    n_ge = jnp.sum(jnp.where(keyv >= thr, 1.0, 0.0), axis=1, keepdims=True)
    jcut_scr[...] = jnp.full((tq, 1), s_len, I32)

    @pl.when(jnp.max(n_ge - kth) > 0.0)
    def _():
        need = kth - jnp.sum(jnp.where(key_scr[...] > thr, 1.0, 0.0), axis=1, keepdims=True)
        n_bits = max(1, (s_len - 1).bit_length())

        def idx_step(i, j):
            cand = j + lax.shift_left(jnp.int32(1), n_bits - 1 - i)
            cnt = jnp.sum(jnp.where(key_scr[...] == thr, jnp.where(idx < cand, 1.0, 0.0), 0.0),
                          axis=1, keepdims=True)
            return jnp.where(cnt < need, cand, j)

        jcut_scr[...] = lax.fori_loop(0, n_bits, idx_step, jnp.zeros((tq, 1), I32))

    jcut = jcut_scr[...]
    bias = jnp.where(keyv > thr, 0.0,
                     jnp.where(keyv == thr, jnp.where(idx <= jcut, 0.0, -jnp.inf), -jnp.inf)).astype(BF16)

    q = q_ref[0] * (C_HEAD_DIM ** -0.5)
    for grp in range(C_KV_HEADS):
        kt = kt_ref[0, grp * C_HEAD_DIM:(grp + 1) * C_HEAD_DIM, :]
        v_ext = v_ref[0, :, grp * 2 * C_HEAD_DIM:(grp + 1) * 2 * C_HEAD_DIM]
        hsl = [slice((grp * C_GROUP + hg) * C_HEAD_DIM, (grp * C_GROUP + hg + 1) * C_HEAD_DIM)
               for hg in range(C_GROUP)]
        qg = jnp.concatenate([q[:, sl] for sl in hsl], axis=0).astype(BF16)
        lg = jnp.dot(qg, kt, preferred_element_type=F32).astype(BF16)
        ps = []
        for hg in range(C_GROUP):
            logits = lg[hg * tq:(hg + 1) * tq, :] + bias
            m_tile = functools.reduce(jnp.maximum, [logits[:, i * LANES:(i + 1) * LANES]
                                                    for i in range(s_len // LANES)])
            mx = jnp.max(m_tile.astype(F32), axis=1, keepdims=True).astype(BF16)
            ps.append(jnp.exp(logits - mx))
        out = jnp.dot(jnp.concatenate(ps, axis=0), v_ext, preferred_element_type=F32)
        for hg in range(C_GROUP):
            rows = slice(hg * tq, (hg + 1) * tq)
            o_ref[0, :, hsl[hg]] = out[rows, 0:C_HEAD_DIM] / out[rows, C_HEAD_DIM:C_HEAD_DIM + 1]


DSA_KEY_STEP = 512


def _dsa_call(u, qi_heads, kt, v, kit, prev, tq, top_k, causal, limit_const, blk_off, n_blk, s_len):
    b, t, _ = u.shape
    in_specs = [pl.BlockSpec((1, tq, C_WIDTH), lambda i, j: (i, j + blk_off, OD_Q // C_WIDTH)),
                pl.BlockSpec((1, IDX_HEADS, tq, LANES), lambda i, j: (i, 0, j + blk_off, 0)),
                pl.BlockSpec((1, tq, LANES), lambda i, j: (i, j + blk_off, OD_KI // LANES)),
                pl.BlockSpec((1, 2 * C_HEAD_DIM, s_len), lambda i, j: (i, 0, 0)),
                pl.BlockSpec((1, s_len, 2 * C_KV_HEADS * C_HEAD_DIM), lambda i, j: (i, 0, 0)),
                pl.BlockSpec((1, LANES, s_len), lambda i, j: (i, 0, 0))]
    args = [u, qi_heads, u, kt, v, kit]
    aliases = {}
    if prev is not None:
        in_specs.append(pl.BlockSpec(memory_space=pl.ANY))
        args.append(prev)
        aliases = {len(args) - 1: 0}
    return pl.pallas_call(
        functools.partial(_dsa_kernel, top_k=top_k, causal=causal, limit_const=limit_const, blk_off=blk_off,
                          has_prev=prev is not None),
        grid=(b, n_blk),
        in_specs=in_specs,
        out_specs=pl.BlockSpec((1, tq, C_WIDTH), lambda i, j: (i, j + blk_off, 0)),
        out_shape=jax.ShapeDtypeStruct((b, t, C_WIDTH), F32),
        scratch_shapes=[pltpu.VMEM((tq, s_len), I32), pltpu.VMEM((tq, 1), I32), pltpu.VMEM((tq, 1), I32)],
        input_output_aliases=aliases,
        compiler_params=_cparams("parallel", "arbitrary"),
        name="dsa",
    )(*args)


def _dsa(u, qi_heads, kt, v, kit, tq, top_k, causal, limit_const):
    b, t, _ = u.shape
    s_full = kt.shape[2]
    if not causal or s_full % DSA_KEY_STEP:
        return _dsa_call(u, qi_heads, kt, v, kit, None, tq, top_k, causal, limit_const, 0, t // tq, s_full)
    per = DSA_KEY_STEP // tq
    att = None
    for cls in range(s_full // DSA_KEY_STEP):
        att = _dsa_call(u, qi_heads, kt, v, kit, att, tq, top_k, causal, limit_const,
                        cls * per, per, (cls + 1) * DSA_KEY_STEP)
    return att


def _prep_even(w):
    win = w['w_in']
    a, bq = win[:, :A_COLS], win[:, A_COLS:]
    cols = [a[:, 0:3072], bq[:, 0:3072], a[:, 3072:A_COLS], bq[:, 3072:3088],
            jnp.zeros((D_MODEL, EV_COLS_PAD - EV_GATE - 2 * B_HEADS), F32)]
    p = {'w_in': jnp.concatenate(cols, axis=1).astype(BF16)}
    mu = w['mu']
    p['mu_main'] = mu[None, 0:3072]
    p['mu_lora'] = mu[None, 3072:A_COLS]
    row = lambda v: v.reshape(1, -1)
    p['w0'], p['a0'], p['k_k'], p['k_a'] = row(w['w0']), row(w['a0']), row(w['k_k']), row(w['k_a'])
    p['r_k'] = row(w['r_k'])
    z = lambda n: jnp.zeros((n, A_WIDTH), F32)
    p['w2p'] = jnp.concatenate([w['w2'], z(192)], axis=0).astype(BF16)
    p['a2p'] = jnp.concatenate([z(64), w['a2'], z(128)], axis=0).astype(BF16)
    p['g2p'] = jnp.concatenate([z(128), w['g2']], axis=0).astype(BF16)
    seg = jnp.arange(A_WIDTH) // A_HEAD_DIM
    p['seg64'] = (seg[:, None] == seg[None, :]).astype(BF16)
    p['a_ln_g'], p['a_ln_b'] = row(w['a_ln_g']), row(w['a_ln_b'])
    p['conv_w'], p['conv_b'] = w['conv_w'], row(w['conv_b'])
    p['gate_b'] = jnp.concatenate([w['b_i'], w['b_f'], jnp.zeros((LANES - 2 * B_HEADS,), F32)])[None]
    p['b_ln_g'], p['b_ln_b'] = row(w['b_ln_g']), row(w['b_ln_b'])
    p['w_out_a'] = w['w_out'][:A_WIDTH].astype(BF16)
    p['w_out_b'] = w['w_out'][A_WIDTH:].astype(BF16)
    p['ln1_g'], p['ln1_b'], p['ln2_g'], p['ln2_b'] = row(w['ln1_g']), row(w['ln1_b']), row(w['ln2_g']), row(w['ln2_b'])
    half = w['ffn_gate'].shape[1] // 2
    p['ffn_g'] = jnp.stack([w['ffn_gate'][:, :half], w['ffn_gate'][:, half:]]).astype(BF16)
    p['ffn_u'] = jnp.stack([w['ffn_up'][:, :half], w['ffn_up'][:, half:]]).astype(BF16)
    p['ffn_d'] = jnp.stack([w['ffn_down'][:half], w['ffn_down'][half:]]).astype(BF16)
    return p


def _prep_odd(w):
    row = lambda v: v.reshape(1, -1)
    p = {'w_in': jnp.pad(w['w_in'], ((0, 0), (0, OD_COLS_PAD - w['w_in'].shape[1]))).astype(BF16)}
    p['w_out'] = w['w_out'].astype(BF16)
    p['ln1_g'], p['ln1_b'], p['ln2_g'], p['ln2_b'] = row(w['ln1_g']), row(w['ln1_b']), row(w['ln2_g']), row(w['ln2_b'])
    p['router_w'] = jnp.pad(w['router'], ((0, 0), (0, LANES - N_EXPERTS)))
    p['router_b'] = jnp.concatenate([w['router_b'], jnp.full((LANES - N_EXPERTS,), -1e30, F32)])[None]
    p['e_gate'], p['e_up'], p['e_down'] = w['e_gate'].astype(BF16), w['e_up'].astype(BF16), w['e_down'].astype(BF16)
    return p


def _pick(n, pref):
    for c in pref:
        if n % c == 0:
            return c
    return n


def _even_layer(x, shift0, wkv0, conv0, c0, n0, m0, p):
    b, t, _ = x.shape
    m = b * t
    tm = _pick(m, (512, 256, 128, 64, 32, 16, 8))
    u = _matmul(x.reshape(m, D_MODEL), p['w_in'], tm, EV_COLS_PAD // 3).reshape(b, t, EV_COLS_PAD)

    new_shift = jnp.concatenate([u[:, t - 1:, 0:3072], u[:, t - 1:, EV_LORA:EV_LORA + A_LORA]], axis=-1)
    new_conv = jnp.concatenate([conv0, u[:, :, EV_QK:EV_QK + 1024]], axis=1)[:, t:]

    s0_main, s0_lora = shift0[:, :, 0:3072], shift0[:, :, 3072:A_COLS]
    tp = _pick(t, (256, 128, 64, 32, 16, 8))
    seq, g, bonus = _rwkv_pre(u, s0_main, s0_lora, p, tp)
    rows = b * A_HEADS // 2
    assert rows % LANES == 0, "batch must be a multiple of 16"
    s0 = wkv0.reshape(b, A_HEADS // 2, 2, A_HEAD_DIM, A_HEAD_DIM).transpose(2, 4, 3, 0, 1)
    s0 = s0.reshape(2, A_HEAD_DIM, A_HEAD_DIM, rows)
    tt = _pick(t, (16, 8))
    y, s_fin = _wkv_scan(seq.reshape(WKV_N_SRC, t, rows, LANES), s0, tt)
    y = y.reshape(t, b * A_WIDTH)
    new_wkv = s_fin.reshape(2, A_HEAD_DIM, A_HEAD_DIM, b, A_HEADS // 2).transpose(3, 4, 0, 2, 1)
    new_wkv = new_wkv.reshape(b, A_HEADS, A_HEAD_DIM, A_HEAD_DIM)

    ct0 = jnp.concatenate([c0.transpose(0, 1, 3, 2).reshape(b, B_QK_WIDTH, B_V_DIM),
                           n0.reshape(b, B_QK_WIDTH, 1),
                           jnp.zeros((b, B_QK_WIDTH, LANES - 1), F32)], axis=-1)
    m0p = jnp.pad(m0, ((0, 0), (0, LANES - B_HEADS)))[:, None, :]
    lm = _pick(t, (256, 128))
    if t % LANES:
        t_pad = -t % LANES
        u_b = jnp.pad(u, ((0, 0), (0, t_pad), (0, 0)))
        lm = LANES
    else:
        u_b = u
    yb, ct, m_out = _mlstm(u_b, conv0, ct0, m0p, p, lm, t)
    yb = yb[:, :t].reshape(m, B_V_WIDTH)
    new_c = ct[:, :, 0:B_V_DIM].reshape(b, B_HEADS, B_QK_DIM, B_V_DIM).transpose(0, 1, 3, 2)
    new_n = ct[:, :, B_V_DIM].reshape(b, B_HEADS, B_QK_DIM)
    new_m = m_out[:, 0, 0:B_HEADS]

    x2 = x.reshape(m, D_MODEL)
    x2 = _even_out(y, bonus, g, yb, x2, p, b, _pick(t, (256, 128, 64, 32, 16, 8)))
    x2 = _experts(x2, None, None, p['ffn_g'], p['ffn_u'], p['ffn_d'], p['ln2_g'], p['ln2_b'], tm, routed=False)
    return x2.reshape(b, t, D_MODEL), (new_shift, new_wkv, new_conv, new_c, new_n, new_m)


def _odd_layer(x, past_k, past_v, past_ki, p):
    b, t, _ = x.shape
    m = b * t
    tm = _pick(m, (512, 256, 128, 64, 32, 16, 8))
    u = _matmul(x.reshape(m, D_MODEL), p['w_in'], tm, OD_COLS_PAD).reshape(b, t, OD_COLS_PAD)
    k_new = u[:, :, OD_K:OD_K + 256]
    v_new = u[:, :, OD_V:OD_V + 256]
    ki_new = u[:, :, OD_KI:OD_KI + IDX_DIM]
    if past_k is None:
        keys_k, keys_v, keys_i = k_new, v_new, ki_new
        causal, limit, tq = True, 0, _pick(t, (2 * CHUNK, CHUNK))
        top_k = min(TOPK_MAX, t // 4)
    else:
        keys_k = jnp.concatenate([past_k.reshape(b, -1, 256), k_new], axis=1)
        keys_v = jnp.concatenate([past_v.reshape(b, -1, 256), v_new], axis=1)
        keys_i = jnp.concatenate([past_ki, ki_new], axis=1)
        limit = keys_k.shape[1]
        causal, tq = False, t
        top_k = min(TOPK_MAX, limit // 4)
    s_pad = -keys_k.shape[1] % LANES
    pad_s = lambda z: jnp.pad(z, ((0, 0), (0, s_pad), (0, 0))) if s_pad else z
    kt = pad_s(keys_k).transpose(0, 2, 1).astype(BF16)
    vv = pad_s(keys_v).astype(BF16).reshape(b, -1, C_KV_HEADS, C_HEAD_DIM)
    vv = jnp.concatenate([vv, jnp.ones_like(vv)], axis=-1).reshape(b, -1, 2 * C_KV_HEADS * C_HEAD_DIM)
    kit = jnp.pad(pad_s(keys_i), ((0, 0), (0, 0), (0, LANES - IDX_DIM))).transpose(0, 2, 1).astype(BF16)
    qi = u[:, :, OD_QI:OD_QI + IDX_HEADS * IDX_DIM].reshape(b, t, IDX_HEADS, IDX_DIM).transpose(0, 2, 1, 3)
    qi = jnp.pad(qi, ((0, 0), (0, 0), (0, 0), (0, LANES - IDX_DIM))).astype(BF16)
    att = _dsa(u, qi, kt, vv, kit, tq, top_k, causal, limit)
    x2 = _proj_ln(att.reshape(m, C_WIDTH), x.reshape(m, D_MODEL), p['w_out'], p['ln1_g'], p['ln1_b'],
                  _pick(m, (512, 256, 128, 64, 32, 16, 8)))
    x2 = _experts(x2, p['router_w'], p['router_b'], p['e_gate'], p['e_up'], p['e_down'],
                  p['ln2_g'], p['ln2_b'], _pick(m, (1024, 512, 256, 128)), routed=True)
    st = (k_new.reshape(b, t, C_KV_HEADS, C_HEAD_DIM), v_new.reshape(b, t, C_KV_HEADS, C_HEAD_DIM), ki_new)
    return x2.reshape(b, t, D_MODEL), st


def _trunk(x, shift0, wkv0, conv0, c0, n0, m0, past_k, past_v, past_ki, ev, od):
    even_out = [[] for _ in range(6)]
    odd_out = [[] for _ in range(3)]
    for layer in range(DEPTH):
        i = layer // 2
        if layer % 2 == 0:
            x, st = _even_layer(x, shift0[i], wkv0[i], conv0[i], c0[i], n0[i], m0[i], ev[i])
            for acc, val in zip(even_out, st):
                acc.append(val)
        else:
            if past_k is None:
                x, st = _odd_layer(x, None, None, None, od[i])
            else:
                x, st = _odd_layer(x, past_k[i], past_v[i], past_ki[i], od[i])
            for acc, val in zip(odd_out, st):
                acc.append(val)
    return x, [jnp.stack(v) for v in even_out], [jnp.stack(v) for v in odd_out]


def kernel(x_prompt, x_sample, state_shift, state_wkv, state_conv, state_c, state_n, state_m,
           cache_k, cache_v, cache_idx_k,
           ev_w_in, a_mu, a_w0, a_w2, a_a0, a_a2, a_g2, a_k_k, a_k_a, a_r_k, a_ln_g, a_ln_b,
           b_conv_w, b_conv_b, b_i_bias, b_f_bias, b_ln_g, b_ln_b, ev_w_out, ev_ln1_g, ev_ln1_b,
           ffn_w_gate, ffn_w_up, ffn_w_down, ev_ln2_g, ev_ln2_b,
           od_w_in, od_w_out, od_ln1_g, od_ln1_b, moe_w_router, moe_b_router,
           moe_w_gate, moe_w_up, moe_w_down, od_ln2_g, od_ln2_b):
    ew = {'w_in': ev_w_in, 'mu': a_mu, 'w0': a_w0, 'w2': a_w2, 'a0': a_a0, 'a2': a_a2, 'g2': a_g2,
          'k_k': a_k_k, 'k_a': a_k_a, 'r_k': a_r_k, 'a_ln_g': a_ln_g, 'a_ln_b': a_ln_b,
          'conv_w': b_conv_w, 'conv_b': b_conv_b, 'b_i': b_i_bias, 'b_f': b_f_bias,
          'b_ln_g': b_ln_g, 'b_ln_b': b_ln_b, 'w_out': ev_w_out, 'ln1_g': ev_ln1_g, 'ln1_b': ev_ln1_b,
          'ffn_gate': ffn_w_gate, 'ffn_up': ffn_w_up, 'ffn_down': ffn_w_down,
          'ln2_g': ev_ln2_g, 'ln2_b': ev_ln2_b}
    ow = {'w_in': od_w_in, 'w_out': od_w_out, 'ln1_g': od_ln1_g, 'ln1_b': od_ln1_b,
          'router': moe_w_router, 'router_b': moe_b_router, 'e_gate': moe_w_gate, 'e_up': moe_w_up,
          'e_down': moe_w_down, 'ln2_g': od_ln2_g, 'ln2_b': od_ln2_b}
    n_even, n_odd = ev_w_in.shape[0], od_w_in.shape[0]
    ev = [_prep_even({k: v[i] for k, v in ew.items()}) for i in range(n_even)]
    od = [_prep_odd({k: v[i] for k, v in ow.items()}) for i in range(n_odd)]

    bp = x_prompt.shape[0]
    z = functools.partial(jnp.zeros, dtype=F32)
    y_p, ep, op = _trunk(x_prompt,
                         z((n_even, bp, 1, A_COLS)), z((n_even, bp, A_HEADS, A_HEAD_DIM, A_HEAD_DIM)),
                         z((n_even, bp, 3, 2 * B_QK_WIDTH)), z((n_even, bp, B_HEADS, B_V_DIM, B_QK_DIM)),
                         z((n_even, bp, B_HEADS, B_QK_DIM)), z((n_even, bp, B_HEADS)),
                         None, None, None, ev, od)
    y_s, es, os_ = _trunk(x_sample, state_shift, state_wkv, state_conv, state_c, state_n, state_m,
                          cache_k, cache_v, cache_idx_k, ev, od)
    return (y_p, y_s,
            ep[0], es[0], ep[1], es[1], ep[2], es[2], ep[3], es[3], ep[4], es[4], ep[5], es[5],
            op[0], os_[0], op[1], os_[1], op[2], os_[2])
```

```python
import functools
import math

import jax
import jax.numpy as jnp
from jax import lax
from jax.experimental import pallas as pl
from jax.experimental.pallas import tpu as pltpu

F32 = jnp.float32
BF16 = jnp.bfloat16
I32 = jnp.int32

D_MODEL = 1024
DEPTH = 4
CHUNK = 64
A_HEADS, A_HEAD_DIM, A_WIDTH = 16, 64, 1024
A_LORA = 256
A_COLS = 3 * A_WIDTH + A_LORA
B_HEADS, B_QK_DIM, B_V_DIM = 8, 64, 128
B_QK_WIDTH, B_V_WIDTH = 512, 1024
C_HEADS, C_KV_HEADS, C_HEAD_DIM, C_GROUP = 8, 2, 128, 4
C_WIDTH = 1024
IDX_HEADS, IDX_DIM = 8, 64
TOPK_MAX = 256
N_EXPERTS = 8
D_FF_EXPERT = 1408
ALPHA = (2 * DEPTH) ** 0.25
LN_EPS = 1e-5

LANES = 128
SUBLANES = 8
VMEM_LIMIT = 56 * 1024 * 1024

EV_R, EV_K, EV_V, EV_QK, EV_VB, EV_O, EV_LORA, EV_GATE = 0, 1024, 2048, 3072, 4096, 5120, 6144, 6400
EV_COLS_PAD = 6528
OD_Q, OD_K, OD_V, OD_QI, OD_KI = 0, 1024, 1280, 1536, 2048
OD_COLS_PAD = 2176

INT_MIN = -2 ** 31


def _cparams(*sem):
    return pltpu.CompilerParams(dimension_semantics=sem, vmem_limit_bytes=VMEM_LIMIT)


def _layer_norm(z, g, b):
    mu = jnp.mean(z, axis=-1, keepdims=True)
    d = z - mu
    var = jnp.mean(d * d, axis=-1, keepdims=True)
    return d * lax.rsqrt(var + LN_EPS) * g + b


def _segsum(x, e):
    hi = x.astype(BF16)
    lo = (x - hi.astype(F32)).astype(BF16)
    return jnp.dot(hi, e, preferred_element_type=F32) + jnp.dot(lo, e, preferred_element_type=F32)


def _mm_kernel(x_ref, w_ref, o_ref):
    o_ref[...] = jnp.dot(x_ref[...].astype(BF16), w_ref[...], preferred_element_type=F32)


def _matmul(x, w, tm, tn):
    m, k = x.shape
    n = w.shape[1]
    return pl.pallas_call(
        _mm_kernel,
        grid=(n // tn, m // tm),
        in_specs=[pl.BlockSpec((tm, k), lambda j, i: (i, 0)),
                  pl.BlockSpec((k, tn), lambda j, i: (0, j))],
        out_specs=pl.BlockSpec((tm, tn), lambda j, i: (i, j)),
        out_shape=jax.ShapeDtypeStruct((m, n), F32),
        compiler_params=_cparams("parallel", "parallel"),
        name="proj_in",
    )(x, w)


def _rwkv_pre_kernel(ur, uk, uv, ul, s0m, s0l, mum, mul_, w0, a0, kk_, ka_, rk_, w2, a2, g2, e_ref,
                     o_seq, o_g, o_bonus, prev_m, prev_l):
    @pl.when(pl.program_id(1) == 0)
    def _():
        prev_m[...] = s0m[0]
        prev_l[...] = s0l[0]

    tm = ur.shape[1]
    row = lax.broadcasted_iota(I32, (tm, 1), 0)

    def lerp(x, prev, mu):
        shifted = jnp.where(row == 0, prev, pltpu.roll(x, 1, 0))
        return x + (shifted - x) * mu

    r_raw, k_raw, v_raw, l_raw = ur[0], uk[0], uv[0], ul[0]
    r = lerp(r_raw, prev_m[:, 0:1024], mum[:, 0:1024])
    k = lerp(k_raw, prev_m[:, 1024:2048], mum[:, 1024:2048])
    v = lerp(v_raw, prev_m[:, 2048:3072], mum[:, 2048:3072])
    lo = lerp(l_raw, prev_l[...], mul_[...])
    prev_m[:, 0:1024] = r_raw[tm - 1:tm, :]
    prev_m[:, 1024:2048] = k_raw[tm - 1:tm, :]
    prev_m[:, 2048:3072] = v_raw[tm - 1:tm, :]
    prev_l[...] = l_raw[tm - 1:tm, :]

    zw = w0[...] + jnp.dot(jnp.tanh(lo).astype(BF16), w2[...], preferred_element_type=F32)
    decay = jnp.exp(-(math.exp(-0.5) * jax.nn.sigmoid(zw)))
    a = jax.nn.sigmoid(a0[...] + jnp.dot(lo.astype(BF16), a2[...], preferred_element_type=F32))
    g = jnp.dot(jax.nn.sigmoid(lo).astype(BF16), g2[...], preferred_element_type=F32)

    e = e_ref[...]
    kk = k * kk_[...]
    kk = kk * lax.rsqrt(jnp.maximum(_segsum(kk * kk, e), 1e-24))
    k2 = k * (1.0 + (a - 1.0) * ka_[...])
    bonus = _segsum(r * k2 * rk_[...], e) * v

    o_seq[WKV_SRC_R] = r
    o_seq[WKV_SRC_W] = decay
    o_seq[WKV_SRC_K] = k2
    o_seq[WKV_SRC_V] = v
    o_seq[WKV_SRC_A] = -kk
    o_seq[WKV_SRC_B] = kk * a
    o_g[...] = g
    o_bonus[...] = bonus


def _rwkv_pre(u, s0_main, s0_lora, p, tm):
    b, t, _ = u.shape
    blk = lambda w, c: pl.BlockSpec((1, tm, w), lambda i, j, c=c: (i, j, c))
    full = lambda arr: pl.BlockSpec(arr.shape, lambda i, j: (0,) * arr.ndim)
    params = [p['mu_main'], p['mu_lora'], p['w0'], p['a0'], p['k_k'], p['k_a'], p['r_k'],
              p['w2p'], p['a2p'], p['g2p'], p['seg64']]
    out = jax.ShapeDtypeStruct((t, b * A_WIDTH), F32)
    return pl.pallas_call(
        _rwkv_pre_kernel,
        grid=(b, t // tm),
        in_specs=[blk(1024, EV_R // 1024), blk(1024, EV_K // 1024), blk(1024, EV_V // 1024),
                  blk(A_LORA, EV_LORA // A_LORA),
                  pl.BlockSpec((1, 1, 3072), lambda i, j: (i, 0, 0)),
                  pl.BlockSpec((1, 1, A_LORA), lambda i, j: (i, 0, 0))] + [full(a) for a in params],
        out_specs=[pl.BlockSpec((WKV_N_SRC, tm, A_WIDTH), lambda i, j: (0, j, i)),
                   pl.BlockSpec((tm, A_WIDTH), lambda i, j: (j, i)),
                   pl.BlockSpec((tm, A_WIDTH), lambda i, j: (j, i))],
        out_shape=[jax.ShapeDtypeStruct((WKV_N_SRC, t, b * A_WIDTH), F32), out, out],
        scratch_shapes=[pltpu.VMEM((1, 3072), F32), pltpu.VMEM((1, A_LORA), F32)],
        compiler_params=_cparams("parallel", "arbitrary"),
        name="rwkv_pre",
    )(u, u, u, u, s0_main, s0_lora, *params)


WKV_N_SRC = 6
WKV_SRC_R, WKV_SRC_W, WKV_SRC_K, WKV_SRC_V, WKV_SRC_A, WKV_SRC_B = range(WKV_N_SRC)
WKV_VGROUP = 4
WKV_KCHUNK = 32


def _wkv_kernel(seq_ref, s0_ref, y_ref, s_ref, tr, sa_buf, ybuf):
    @pl.when(pl.program_id(1) == 0)
    def _():
        s_ref[...] = s0_ref[...]

    n_t = seq_ref.shape[1]
    d = A_HEAD_DIM
    vspan = WKV_VGROUP * SUBLANES
    groups = [(p, g) for p in range(2) for g in range(d // vspan)]
    n_kc = d // WKV_KCHUNK

    def vrows(g, j):
        return slice(g * vspan + j * SUBLANES, g * vspan + (j + 1) * SUBLANES)

    def transpose_tile(i, tok):
        tr[i, tok] = jnp.transpose(seq_ref[i, tok])

    for i in range(WKV_N_SRC):
        transpose_tile(i, 0)
    transpose_tile(WKV_SRC_A, min(1, n_t - 1))

    for p, g in groups:
        acc = [jnp.zeros((SUBLANES, LANES), F32) for _ in range(WKV_VGROUP)]
        for kx in range(d):
            ab = tr[WKV_SRC_A, 0, p * d + kx:p * d + kx + 1, :]
            for j in range(WKV_VGROUP):
                acc[j] = acc[j] + s_ref[p, kx, vrows(g, j), :] * ab
        for j in range(WKV_VGROUP):
            sa_buf[p, vrows(g, j), :] = acc[j]

    def step(t, carry):
        tn = jnp.minimum(t + 1, n_t - 1)
        for gi, (p, g) in enumerate(groups):
            vv = [tr[WKV_SRC_V, t, p * d + g * vspan + j * SUBLANES:p * d + g * vspan + (j + 1) * SUBLANES, :]
                  for j in range(WKV_VGROUP)]
            sa = [sa_buf[p, vrows(g, j), :] for j in range(WKV_VGROUP)]
            zero = tuple(jnp.zeros((SUBLANES, LANES), F32) for _ in range(2 * WKV_VGROUP))

            def key_chunk(kc, accs, gi=gi, p=p, g=g, vv=vv, sa=sa):
                yacc, san = list(accs[:WKV_VGROUP]), list(accs[WKV_VGROUP:])
                for i in range(WKV_KCHUNK):
                    kx = kc * WKV_KCHUNK + i
                    row = pl.ds(p * d + kx, 1)
                    wb = tr[WKV_SRC_W, t, row, :]
                    bb = tr[WKV_SRC_B, t, row, :]
                    kb = tr[WKV_SRC_K, t, row, :]
                    rb = tr[WKV_SRC_R, t, row, :]
                    an = tr[WKV_SRC_A, tn, row, :]
                    for j in range(WKV_VGROUP):
                        n = s_ref[p, kx, vrows(g, j), :] * wb + sa[j] * bb + vv[j] * kb
                        s_ref[p, kx, vrows(g, j), :] = n
                        yacc[j] = yacc[j] + n * rb
                        san[j] = san[j] + n * an
                tile = jnp.minimum(gi * n_kc + kc, WKV_N_SRC - 1)
                ahead = jnp.where(tile == WKV_SRC_A, 2, 1)
                transpose_tile(tile, jnp.minimum(t + ahead, n_t - 1))
                return tuple(yacc) + tuple(san)

            accs = lax.fori_loop(0, n_kc, key_chunk, zero)
            yacc, san = accs[:WKV_VGROUP], accs[WKV_VGROUP:]
            for j in range(WKV_VGROUP):
                ybuf[t, p * d + g * vspan + j * SUBLANES:p * d + g * vspan + (j + 1) * SUBLANES, :] = yacc[j]
                sa_buf[p, vrows(g, j), :] = san[j]
        return carry

    lax.fori_loop(0, n_t, step, 0)

    def transpose_out(t, carry):
        y_ref[t] = jnp.transpose(ybuf[t])
        return carry

    lax.fori_loop(0, n_t, transpose_out, 0, unroll=4)


def _wkv_scan(seq, s0, tt):
    _, t, rows, _ = seq.shape
    d = A_HEAD_DIM
    st = pl.BlockSpec((2, d, d, LANES), lambda g, i: (0, 0, 0, g))
    return pl.pallas_call(
        _wkv_kernel,
        grid=(rows // LANES, t // tt),
        in_specs=[pl.BlockSpec((WKV_N_SRC, tt, LANES, LANES), lambda g, i: (0, i, g, 0)), st],
        out_specs=[pl.BlockSpec((tt, LANES, LANES), lambda g, i: (i, g, 0)), st],
        out_shape=[jax.ShapeDtypeStruct((t, rows, LANES), F32), jax.ShapeDtypeStruct((2, d, d, rows), F32)],
        scratch_shapes=[pltpu.VMEM((WKV_N_SRC, tt, LANES, LANES), F32), pltpu.VMEM((2, d, LANES), F32),
                        pltpu.VMEM((tt, LANES, LANES), F32)],
        compiler_params=_cparams("parallel", "arbitrary"),
        name="wkv_scan",
    )(seq, s0)


def _mlstm_kernel(uqk, uv, uo, ug, conv0, ct0, m0, cw, cb, gb, lng, lnb,
                  o_y, o_ct, o_m, ct, m_scr, carry, *, t_valid):
    tb = pl.program_id(1)

    @pl.when(tb == 0)
    def _():
        ct[...] = ct0[...]
        m_scr[...] = m0[...]
        carry[...] = conv0[...]

    for bi in range(uqk.shape[0]):
        one = pl.ds(bi, 1)
        _mlstm_chunk(tb, uqk.at[one], uv.at[one], uo.at[one], ug.at[one], cw, cb, gb, lng, lnb,
                     o_y.at[one], ct.at[bi], m_scr.at[bi], carry.at[bi], t_valid=t_valid)

    @pl.when(tb == pl.num_programs(1) - 1)
    def _():
        o_ct[...] = ct[...]
        o_m[...] = m_scr[...]


def _mlstm_chunk(tb, uqk, uv, uo, ug, cw, cb, gb, lng, lnb, o_y, ct, m_scr, carry, *, t_valid):
    L = uqk.shape[1]
    row = lax.broadcasted_iota(I32, (L, 1), 0)
    x = uqk[0]

    def shifted(j):
        out = pltpu.roll(x, j, 0)
        for i in range(j):
            out = jnp.where(row == i, carry[3 + i - j:4 + i - j, :], out)
        return out

    conv = x * cw[3:4, :] + shifted(1) * cw[2:3, :] + shifted(2) * cw[1:2, :] + shifted(3) * cw[0:1, :] + cb[...]
    carry[...] = x[L - 3:L, :]
    qk = conv * jax.nn.sigmoid(conv)
    q_all = qk[:, 0:B_QK_WIDTH]
    k_t = jnp.transpose(qk[:, B_QK_WIDTH:2 * B_QK_WIDTH] * (B_QK_DIM ** -0.5))

    lane = lax.broadcasted_iota(I32, (L, LANES), 1)
    gz = ug[0] + gb[...]
    lg = jnp.where(lane < B_HEADS, gz, jnp.minimum(gz, 0.0) - jnp.log(1.0 + jnp.exp(-jnp.abs(gz))))
    valid = (row + tb * L) < t_valid
    lg = jnp.where(valid, lg, jnp.where(lane < B_HEADS, -jnp.inf, 0.0))
    lg_t = jnp.transpose(lg)
    ti = lax.broadcasted_iota(I32, (L, L), 0)
    si = lax.broadcasted_iota(I32, (L, L), 1)
    tril = ti >= si
    tril_f = tril.astype(F32)
    triu_f = (ti <= si).astype(F32)
    lg_fin = jnp.where(lane < B_HEADS, 0.0, lg)
    bcum = jnp.dot(tril_f, lg_fin, preferred_element_type=F32, precision=lax.Precision.HIGHEST)
    bcum_t = jnp.dot(jnp.transpose(lg_fin), triu_f, preferred_element_type=F32,
                     precision=lax.Precision.HIGHEST)

    head_q = lax.shift_right_logical(lax.broadcasted_iota(I32, (L, B_QK_WIDTH), 1), 6)
    lane1 = lax.broadcasted_iota(I32, (1, LANES), 1)
    ones_col = (lax.broadcasted_iota(I32, (L, LANES), 1) == 0).astype(F32)
    m_vec = m_scr[...]
    m_new_vec = m_vec
    lane_ct = lax.broadcasted_iota(I32, (B_QK_DIM, 2 * LANES), 1)
    ct_old = ct[...]
    ct_old_bf = ct_old.astype(BF16)
    k_bf = k_t.astype(BF16)
    vv = uv[0]
    oo = uo[0]
    for h in range(B_HEADS):
        hs = slice(h * B_QK_DIM, (h + 1) * B_QK_DIM)
        vs = slice(h * B_V_DIM, (h + 1) * B_V_DIM)
        bc_col = bcum[:, B_HEADS + h:B_HEADS + h + 1]
        bc_row = bcum_t[B_HEADS + h:B_HEADS + h + 1, :]
        ic_row = lg_t[h:h + 1, :]
        gtot = bc_row[:, L - 1:L]
        m_prev = m_vec[:, h:h + 1]
        dmat = jnp.where(tril, bc_col - bc_row + ic_row, -jnp.inf)
        inter = bc_col + m_prev
        m_t = jnp.maximum(inter, jnp.max(dmat, axis=1, keepdims=True))
        w_intra = jnp.exp(dmat - m_t)
        w_inter = jnp.exp(inter - m_t)
        q_m = jnp.where(head_q == h, q_all, 0.0).astype(BF16)
        s = jnp.dot(q_m, k_bf, preferred_element_type=F32) * w_intra
        v_ext = jnp.concatenate([vv[:, vs], ones_col], axis=1).astype(BF16)
        intra = jnp.dot(s.astype(BF16), v_ext, preferred_element_type=F32)
        cross = jnp.dot(q_m, ct_old_bf, preferred_element_type=F32)
        num = intra[:, 0:B_V_DIM] + w_inter * cross[:, 0:B_V_DIM]
        den = jnp.sum(s, axis=1, keepdims=True) + w_inter * cross[:, B_V_DIM:B_V_DIM + 1]
        hh = num / jnp.maximum(jnp.abs(den), jnp.exp(-m_t))
        mu = jnp.mean(hh, axis=-1, keepdims=True)
        dd = hh - mu
        var = jnp.mean(dd * dd, axis=-1, keepdims=True)
        yn = dd * lax.rsqrt(var + LN_EPS) * lng[:, vs] + lnb[:, vs]
        o_y[0, :, vs] = jax.nn.sigmoid(oo[:, vs]) * yn
        lw = gtot - bc_row + ic_row
        m_new = jnp.maximum(gtot + m_prev, jnp.max(lw, axis=1, keepdims=True))
        w_s = jnp.exp(lw - m_new)
        dec = jnp.exp(gtot + m_prev - m_new)
        kw = k_t[hs, :] * w_s
        upd = jnp.dot(kw.astype(BF16), v_ext, preferred_element_type=F32)
        upd = jnp.where(lane_ct == B_V_DIM, jnp.sum(kw, axis=1, keepdims=True), upd)
        ct[hs, :] = dec * ct_old[hs, :] + upd
        m_new_vec = jnp.where(lane1 == h, m_new, m_new_vec)
    m_scr[...] = m_new_vec


MLSTM_ROWS = 1


def _mlstm(u, conv0, ct0, m0, p, L, t_valid):
    b, t, _ = u.shape
    nb = MLSTM_ROWS if b % MLSTM_ROWS == 0 else 1
    blk = lambda w, c: pl.BlockSpec((nb, L, w), lambda i, j, c=c: (i, j, c))
    full = lambda arr: pl.BlockSpec(arr.shape, lambda i, j: (0,) * arr.ndim)
    per_b = lambda arr: pl.BlockSpec((nb,) + arr.shape[1:], lambda i, j: (i, 0, 0))
    params = [p['conv_w'], p['conv_b'], p['gate_b'], p['b_ln_g'], p['b_ln_b']]
    return pl.pallas_call(
        functools.partial(_mlstm_kernel, t_valid=t_valid),
        grid=(b // nb, t // L),
        in_specs=[blk(1024, EV_QK // 1024), blk(1024, EV_VB // 1024), blk(1024, EV_O // 1024),
                  blk(LANES, EV_GATE // LANES), per_b(conv0), per_b(ct0), per_b(m0)] + [full(a) for a in params],
        out_specs=[pl.BlockSpec((nb, L, B_V_WIDTH), lambda i, j: (i, j, 0)),
                   pl.BlockSpec((nb, B_QK_WIDTH, 2 * LANES), lambda i, j: (i, 0, 0)),
                   pl.BlockSpec((nb, 1, LANES), lambda i, j: (i, 0, 0))],
        out_shape=[jax.ShapeDtypeStruct((b, t, B_V_WIDTH), F32),
                   jax.ShapeDtypeStruct((b, B_QK_WIDTH, 2 * LANES), F32),
                   jax.ShapeDtypeStruct((b, 1, LANES), F32)],
        scratch_shapes=[pltpu.VMEM((nb, B_QK_WIDTH, 2 * LANES), F32), pltpu.VMEM((nb, 1, LANES), F32),
                        pltpu.VMEM((nb, 3, 2 * B_QK_WIDTH), F32)],
        compiler_params=_cparams("parallel", "arbitrary"),
        name="mlstm",
    )(u, u, u, u, conv0, ct0, m0, *params)


def _even_out_kernel(y_ref, bonus_ref, g_ref, yb_ref, x_ref, lng, lnb, e_ref, woa, wob, g1, b1, o_ref):
    e = e_ref[...]
    y = y_ref[...]
    mu = _segsum(y, e) * (1.0 / A_HEAD_DIM)
    d = y - mu
    var = _segsum(d * d, e) * (1.0 / A_HEAD_DIM)
    yn = d * lax.rsqrt(var + LN_EPS) * lng[...] + lnb[...]
    ya = (yn + bonus_ref[...]) * g_ref[...]
    mix = (jnp.dot(ya.astype(BF16), woa[...], preferred_element_type=F32)
           + jnp.dot(yb_ref[...].astype(BF16), wob[...], preferred_element_type=F32))
    o_ref[...] = _layer_norm(ALPHA * x_ref[...] + mix, g1[...], b1[...])


def _even_out(y, bonus, g, yb, x, p, b, tm):
    m = x.shape[0]
    nt = m // b // tm
    tmaj = pl.BlockSpec((tm, D_MODEL), lambda i, j: (j, i))
    row = pl.BlockSpec((tm, D_MODEL), lambda i, j: (i * nt + j, 0))
    full = lambda arr: pl.BlockSpec(arr.shape, lambda i, j: (0,) * arr.ndim)
    params = [p['a_ln_g'], p['a_ln_b'], p['seg64'], p['w_out_a'], p['w_out_b'], p['ln1_g'], p['ln1_b']]
    return pl.pallas_call(
        _even_out_kernel,
        grid=(b, nt),
        in_specs=[tmaj] * 3 + [row] * 2 + [full(a) for a in params],
        out_specs=row,
        out_shape=jax.ShapeDtypeStruct((m, D_MODEL), F32),
        compiler_params=_cparams("parallel", "parallel"),
        name="even_out",
    )(y, bonus, g, yb, x, *params)


def _proj_ln_kernel(a_ref, x_ref, w_ref, g1, b1, o_ref):
    mix = jnp.dot(a_ref[...].astype(BF16), w_ref[...], preferred_element_type=F32)
    o_ref[...] = _layer_norm(ALPHA * x_ref[...] + mix, g1[...], b1[...])


def _proj_ln(a, x, w, g1, b1, tm):
    m = x.shape[0]
    row = pl.BlockSpec((tm, D_MODEL), lambda i: (i, 0))
    full = lambda arr: pl.BlockSpec(arr.shape, lambda i: (0,) * arr.ndim)
    return pl.pallas_call(
        _proj_ln_kernel,
        grid=(m // tm,),
        in_specs=[row, row, full(w), full(g1), full(b1)],
        out_specs=row,
        out_shape=jax.ShapeDtypeStruct((m, D_MODEL), F32),
        compiler_params=_cparams("parallel"),
        name="proj_ln",
    )(a, x, w, g1, b1)


def _swiglu(xb, wg, wu, wd):
    hg = jnp.dot(xb, wg, preferred_element_type=F32)
    hu = jnp.dot(xb, wu, preferred_element_type=F32)
    hid = (hg * jax.nn.sigmoid(hg)) * hu
    return jnp.dot(hid.astype(BF16), wd, preferred_element_type=F32)


def _ffn_kernel(x_ref, wg, wu, wd, g2, b2, o_ref, xb):
    e = pl.program_id(1)

    @pl.when(e == 0)
    def _():
        xb[...] = x_ref[...].astype(BF16)
        o_ref[...] = jnp.zeros_like(o_ref)

    o_ref[...] += _swiglu(xb[...], wg[0], wu[0], wd[0])

    @pl.when(e == pl.num_programs(1) - 1)
    def _():
        o_ref[...] = _layer_norm(ALPHA * x_ref[...] + o_ref[...], g2[...], b2[...])


MOE_CHUNK = 256


def _moe_kernel(x_ref, wr, br, wg, wu, wd, g2, b2, o_ref, xb, comb, pos_scr, post_scr):
    e = pl.program_id(1)
    tm = x_ref.shape[0]

    @pl.when(e == 0)
    def _():
        x = x_ref[...]
        xb[...] = x.astype(BF16)
        o_ref[...] = jnp.zeros_like(o_ref)
        logits = jnp.dot(x, wr[...], preferred_element_type=F32, precision=lax.Precision.HIGHEST) + br[...]
        lane = lax.broadcasted_iota(I32, logits.shape, 1).astype(F32)
        v1 = jnp.max(logits, axis=-1, keepdims=True)
        i1 = jnp.min(jnp.where(logits == v1, lane, float(LANES)), axis=-1, keepdims=True)
        rest = jnp.where(lane == i1, -jnp.inf, logits)
        v2 = jnp.max(rest, axis=-1, keepdims=True)
        i2 = jnp.min(jnp.where(rest == v2, lane, float(LANES)), axis=-1, keepdims=True)
        e2 = jnp.exp(v2 - v1)
        den = 1.0 + e2
        comb[...] = jnp.where(lane == i1, 1.0 / den, jnp.where(lane == i2, e2 / den, 0.0))
        member = jnp.where(lane == i1, 1.0, jnp.where(lane == i2, 1.0, 0.0))
        earlier = (lax.broadcasted_iota(I32, (tm, tm), 0) > lax.broadcasted_iota(I32, (tm, tm), 1))
        pos = jnp.dot(jnp.where(earlier, 1.0, 0.0).astype(BF16), member.astype(BF16), preferred_element_type=F32)
        posm = jnp.where(member > 0.0, pos, -1.0)
        pos_scr[...] = posm
        post_scr[...] = jnp.transpose(posm)

    sel = lax.broadcasted_iota(I32, (tm, LANES), 1) == e
    gate = jnp.sum(jnp.where(sel, comb[...], 0.0), axis=-1, keepdims=True)
    slot_col = jnp.sum(jnp.where(sel, pos_scr[...], 0.0), axis=-1, keepdims=True)
    slot_row = post_scr[pl.ds(e, 1), :]
    count = jnp.max(slot_row) + 1.0

    def run_slots(base, n):
        rows = lax.broadcasted_iota(I32, (n, 1), 0).astype(F32) + float(base)
        gather = jnp.where(slot_row == rows, 1.0, 0.0).astype(BF16)
        xg = jnp.dot(gather, xb[...], preferred_element_type=F32).astype(BF16)
        y = _swiglu(xg, wg[0], wu[0], wd[0])
        cols = lax.broadcasted_iota(I32, (1, n), 1).astype(F32) + float(base)
        scatter = jnp.where(slot_col == cols, 1.0, 0.0).astype(BF16)
        o_ref[...] += gate * jnp.dot(scatter, y.astype(BF16), preferred_element_type=F32)

    chunk = min(MOE_CHUNK, tm)
    half = chunk // 2
    for c in range(tm // chunk):
        lo = c * chunk

        @pl.when(count > lo + half)
        def _(lo=lo):
            run_slots(lo, chunk)

        @pl.when(jnp.logical_and(count > lo, count <= lo + half))
        def _(lo=lo):
            run_slots(lo, half)

    @pl.when(e == pl.num_programs(1) - 1)
    def _():
        o_ref[...] = _layer_norm(ALPHA * x_ref[...] + o_ref[...], g2[...], b2[...])


def _experts(x, wr, br, wg, wu, wd, g2, b2, tm, routed):
    m = x.shape[0]
    ne, _, dff = wg.shape
    row = pl.BlockSpec((tm, D_MODEL), lambda i, e: (i, 0))
    full = lambda arr: pl.BlockSpec(arr.shape, lambda i, e: (0,) * arr.ndim)
    wspecs = [pl.BlockSpec((1, D_MODEL, dff), lambda i, e: (e, 0, 0)),
              pl.BlockSpec((1, D_MODEL, dff), lambda i, e: (e, 0, 0)),
              pl.BlockSpec((1, dff, D_MODEL), lambda i, e: (e, 0, 0))]
    common = dict(grid=(m // tm, ne), out_specs=row, out_shape=jax.ShapeDtypeStruct((m, D_MODEL), F32),
                  compiler_params=_cparams("parallel", "arbitrary"))
    if not routed:
        return pl.pallas_call(
            _ffn_kernel, in_specs=[row] + wspecs + [full(g2), full(b2)],
            scratch_shapes=[pltpu.VMEM((tm, D_MODEL), BF16)], name="experts_dense", **common,
        )(x, wg, wu, wd, g2, b2)
    return pl.pallas_call(
        _moe_kernel, in_specs=[row, full(wr), full(br)] + wspecs + [full(g2), full(b2)],
        scratch_shapes=[pltpu.VMEM((tm, D_MODEL), BF16), pltpu.VMEM((tm, LANES), F32),
                        pltpu.VMEM((tm, LANES), F32), pltpu.VMEM((LANES, tm), F32)],
        name="experts_routed", **common,
    )(x, wr, br, wg, wu, wd, g2, b2)


DSA_COARSE_BITS = 28


def _dsa_kernel(q_ref, qi_ref, wi_ref, kt_ref, v_ref, kit_ref, *rest, top_k, causal, limit_const, blk_off, has_prev):
    o_ref, key_scr, jcut_scr, thr_scr = rest[1:] if has_prev else rest
    tq = q_ref.shape[1]
    s_len = kt_ref.shape[2]
    if causal:
        row_chunk = lax.shift_right_logical(lax.broadcasted_iota(I32, (tq, 1), 0), CHUNK.bit_length() - 1)
        chunk = (pl.program_id(1) + blk_off) * (tq // CHUNK) + row_chunk
        limit = (chunk + 1) * CHUNK
        kth = jnp.minimum(top_k, limit).astype(F32)
    else:
        limit = limit_const
        kth = jnp.full((tq, 1), min(top_k, limit_const), F32)
    idx = lax.broadcasted_iota(I32, (tq, s_len), 1)

    kit = kit_ref[0]
    wi = wi_ref[0] * (IDX_HEADS ** -0.5)
    score = jnp.zeros((tq, s_len), F32)
    for h in range(IDX_HEADS):
        rel = jnp.dot(qi_ref[0, h], kit, preferred_element_type=F32)
        score = score + wi[:, IDX_DIM + h:IDX_DIM + h + 1] * jnp.maximum(rel, 0.0)
    score = jnp.where(score == 0.0, 0.0, score)
    bits = pltpu.bitcast(score, I32)
    key = jnp.where(bits < 0, bits ^ 0x7FFFFFFF, bits)
    key_scr[...] = jnp.where(idx < limit, key, INT_MIN)

    n_grp = 2 if tq % (4 * SUBLANES) == 0 else 1
    rg = tq // n_grp
    rows_of = lambda g: slice(g * rg, (g + 1) * rg)

    def count_ge(thrs):
        return [jnp.sum(jnp.where(key_scr[rows_of(g), :] >= thrs[g], 1.0, 0.0), axis=1, keepdims=True)
                for g in range(n_grp)]

    def bit_step(i, thrs):
        bit = lax.shift_left(jnp.int32(1), 31 - i)
        cands = [t + bit for t in thrs]
        cnts = count_ge(cands)
        return tuple(jnp.where(cnts[g] >= kth[rows_of(g)], cands[g], thrs[g]) for g in range(n_grp))

    thrs = lax.fori_loop(0, DSA_COARSE_BITS, bit_step,
                         tuple(jnp.full((rg, 1), INT_MIN, I32) for _ in range(n_grp)), unroll=4)
    thr_scr[...] = jnp.concatenate(thrs, axis=0)
    n_coarse = jnp.concatenate(count_ge(thrs), axis=0)

    @pl.when(jnp.max(n_coarse - kth) > 0.0)
    def _():
        fine = lax.fori_loop(DSA_COARSE_BITS, 32, bit_step, tuple(thr_scr[rows_of(g), :] for g in range(n_grp)),
                             unroll=4)
        thr_scr[...] = jnp.concatenate(fine, axis=0)

    thr = thr_scr[...]
    keyv = key_scr[...]
    n_ge = jnp.sum(jnp.where(keyv >= thr, 1.0, 0.0), axis=1, keepdims=True)
    jcut_scr[...] = jnp.full((tq, 1), s_len, I32)

    @pl.when(jnp.max(n_ge - kth) > 0.0)
    def _():
        need = kth - jnp.sum(jnp.where(key_scr[...] > thr, 1.0, 0.0), axis=1, keepdims=True)
        n_bits = max(1, (s_len - 1).bit_length())

        def idx_step(i, j):
            cand = j + lax.shift_left(jnp.int32(1), n_bits - 1 - i)
            cnt = jnp.sum(jnp.where(key_scr[...] == thr, jnp.where(idx < cand, 1.0, 0.0), 0.0),
                          axis=1, keepdims=True)
            return jnp.where(cnt < need, cand, j)

        jcut_scr[...] = lax.fori_loop(0, n_bits, idx_step, jnp.zeros((tq, 1), I32))

    jcut = jcut_scr[...]
    bias = jnp.where(keyv > thr, 0.0,
                     jnp.where(keyv == thr, jnp.where(idx <= jcut, 0.0, -jnp.inf), -jnp.inf)).astype(BF16)

    q = q_ref[0] * (C_HEAD_DIM ** -0.5)
    for grp in range(C_KV_HEADS):
        kt = kt_ref[0, grp * C_HEAD_DIM:(grp + 1) * C_HEAD_DIM, :]
        v_ext = v_ref[0, :, grp * 2 * C_HEAD_DIM:(grp + 1) * 2 * C_HEAD_DIM]
        hsl = [slice((grp * C_GROUP + hg) * C_HEAD_DIM, (grp * C_GROUP + hg + 1) * C_HEAD_DIM)
               for hg in range(C_GROUP)]
        qg = jnp.concatenate([q[:, sl] for sl in hsl], axis=0).astype(BF16)
        lg = jnp.dot(qg, kt, preferred_element_type=F32).astype(BF16)
        ps = []
        for hg in range(C_GROUP):
            logits = lg[hg * tq:(hg + 1) * tq, :] + bias
            m_tile = functools.reduce(jnp.maximum, [logits[:, i * LANES:(i + 1) * LANES]
                                                    for i in range(s_len // LANES)])
            mx = jnp.max(m_tile.astype(F32), axis=1, keepdims=True).astype(BF16)
            ps.append(jnp.exp(logits - mx))
        out = jnp.dot(jnp.concatenate(ps, axis=0), v_ext, preferred_element_type=F32)
        for hg in range(C_GROUP):
            rows = slice(hg * tq, (hg + 1) * tq)
            o_ref[0, :, hsl[hg]] = out[rows, 0:C_HEAD_DIM] / out[rows, C_HEAD_DIM:C_HEAD_DIM + 1]


DSA_KEY_STEP = 512


def _dsa_call(u, qi_heads, kt, v, kit, prev, tq, top_k, causal, limit_const, blk_off, n_blk, s_len):
    b, t, _ = u.shape
    in_specs = [pl.BlockSpec((1, tq, C_WIDTH), lambda i, j: (i, j + blk_off, OD_Q // C_WIDTH)),
                pl.BlockSpec((1, IDX_HEADS, tq, LANES), lambda i, j: (i, 0, j + blk_off, 0)),
                pl.BlockSpec((1, tq, LANES), lambda i, j: (i, j + blk_off, OD_KI // LANES)),
                pl.BlockSpec((1, 2 * C_HEAD_DIM, s_len), lambda i, j: (i, 0, 0)),
                pl.BlockSpec((1, s_len, 2 * C_KV_HEADS * C_HEAD_DIM), lambda i, j: (i, 0, 0)),
                pl.BlockSpec((1, LANES, s_len), lambda i, j: (i, 0, 0))]
    args = [u, qi_heads, u, kt, v, kit]
    aliases = {}
    if prev is not None:
        in_specs.append(pl.BlockSpec(memory_space=pl.ANY))
        args.append(prev)
        aliases = {len(args) - 1: 0}
    return pl.pallas_call(
        functools.partial(_dsa_kernel, top_k=top_k, causal=causal, limit_const=limit_const, blk_off=blk_off,
                          has_prev=prev is not None),
        grid=(b, n_blk),
        in_specs=in_specs,
        out_specs=pl.BlockSpec((1, tq, C_WIDTH), lambda i, j: (i, j + blk_off, 0)),
        out_shape=jax.ShapeDtypeStruct((b, t, C_WIDTH), F32),
        scratch_shapes=[pltpu.VMEM((tq, s_len), I32), pltpu.VMEM((tq, 1), I32), pltpu.VMEM((tq, 1), I32)],
        input_output_aliases=aliases,
        compiler_params=_cparams("parallel", "arbitrary"),
        name="dsa",
    )(*args)


def _dsa(u, qi_heads, kt, v, kit, tq, top_k, causal, limit_const):
    b, t, _ = u.shape
    s_full = kt.shape[2]
    if not causal or s_full % DSA_KEY_STEP:
        return _dsa_call(u, qi_heads, kt, v, kit, None, tq, top_k, causal, limit_const, 0, t // tq, s_full)
    per = DSA_KEY_STEP // tq
    att = None
    for cls in range(s_full // DSA_KEY_STEP):
        att = _dsa_call(u, qi_heads, kt, v, kit, att, tq, top_k, causal, limit_const,
                        cls * per, per, (cls + 1) * DSA_KEY_STEP)
    return att


def _prep_even(w):
    win = w['w_in']
    a, bq = win[:, :A_COLS], win[:, A_COLS:]
    cols = [a[:, 0:3072], bq[:, 0:3072], a[:, 3072:A_COLS], bq[:, 3072:3088],
            jnp.zeros((D_MODEL, EV_COLS_PAD - EV_GATE - 2 * B_HEADS), F32)]
    p = {'w_in': jnp.concatenate(cols, axis=1).astype(BF16)}
    mu = w['mu']
    p['mu_main'] = mu[None, 0:3072]
    p['mu_lora'] = mu[None, 3072:A_COLS]
    row = lambda v: v.reshape(1, -1)
    p['w0'], p['a0'], p['k_k'], p['k_a'] = row(w['w0']), row(w['a0']), row(w['k_k']), row(w['k_a'])
    p['r_k'] = row(w['r_k'])
    z = lambda n: jnp.zeros((n, A_WIDTH), F32)
    p['w2p'] = jnp.concatenate([w['w2'], z(192)], axis=0).astype(BF16)
    p['a2p'] = jnp.concatenate([z(64), w['a2'], z(128)], axis=0).astype(BF16)
    p['g2p'] = jnp.concatenate([z(128), w['g2']], axis=0).astype(BF16)
    seg = jnp.arange(A_WIDTH) // A_HEAD_DIM
    p['seg64'] = (seg[:, None] == seg[None, :]).astype(BF16)
    p['a_ln_g'], p['a_ln_b'] = row(w['a_ln_g']), row(w['a_ln_b'])
    p['conv_w'], p['conv_b'] = w['conv_w'], row(w['conv_b'])
    p['gate_b'] = jnp.concatenate([w['b_i'], w['b_f'], jnp.zeros((LANES - 2 * B_HEADS,), F32)])[None]
    p['b_ln_g'], p['b_ln_b'] = row(w['b_ln_g']), row(w['b_ln_b'])
    p['w_out_a'] = w['w_out'][:A_WIDTH].astype(BF16)
    p['w_out_b'] = w['w_out'][A_WIDTH:].astype(BF16)
    p['ln1_g'], p['ln1_b'], p['ln2_g'], p['ln2_b'] = row(w['ln1_g']), row(w['ln1_b']), row(w['ln2_g']), row(w['ln2_b'])
    half = w['ffn_gate'].shape[1] // 2
    p['ffn_g'] = jnp.stack([w['ffn_gate'][:, :half], w['ffn_gate'][:, half:]]).astype(BF16)
    p['ffn_u'] = jnp.stack([w['ffn_up'][:, :half], w['ffn_up'][:, half:]]).astype(BF16)
    p['ffn_d'] = jnp.stack([w['ffn_down'][:half], w['ffn_down'][half:]]).astype(BF16)
    return p


def _prep_odd(w):
    row = lambda v: v.reshape(1, -1)
    p = {'w_in': jnp.pad(w['w_in'], ((0, 0), (0, OD_COLS_PAD - w['w_in'].shape[1]))).astype(BF16)}
    p['w_out'] = w['w_out'].astype(BF16)
    p['ln1_g'], p['ln1_b'], p['ln2_g'], p['ln2_b'] = row(w['ln1_g']), row(w['ln1_b']), row(w['ln2_g']), row(w['ln2_b'])
    p['router_w'] = jnp.pad(w['router'], ((0, 0), (0, LANES - N_EXPERTS)))
    p['router_b'] = jnp.concatenate([w['router_b'], jnp.full((LANES - N_EXPERTS,), -1e30, F32)])[None]
    p['e_gate'], p['e_up'], p['e_down'] = w['e_gate'].astype(BF16), w['e_up'].astype(BF16), w['e_down'].astype(BF16)
    return p


def _pick(n, pref):
    for c in pref:
        if n % c == 0:
            return c
    return n


def _even_layer(x, shift0, wkv0, conv0, c0, n0, m0, p):
    b, t, _ = x.shape
    m = b * t
    tm = _pick(m, (512, 256, 128, 64, 32, 16, 8))
    u = _matmul(x.reshape(m, D_MODEL), p['w_in'], tm, EV_COLS_PAD // 3).reshape(b, t, EV_COLS_PAD)

    new_shift = jnp.concatenate([u[:, t - 1:, 0:3072], u[:, t - 1:, EV_LORA:EV_LORA + A_LORA]], axis=-1)
    new_conv = jnp.concatenate([conv0, u[:, :, EV_QK:EV_QK + 1024]], axis=1)[:, t:]

    s0_main, s0_lora = shift0[:, :, 0:3072], shift0[:, :, 3072:A_COLS]
    tp = _pick(t, (256, 128, 64, 32, 16, 8))
    seq, g, bonus = _rwkv_pre(u, s0_main, s0_lora, p, tp)
    rows = b * A_HEADS // 2
    assert rows % LANES == 0, "batch must be a multiple of 16"
    s0 = wkv0.reshape(b, A_HEADS // 2, 2, A_HEAD_DIM, A_HEAD_DIM).transpose(2, 4, 3, 0, 1)
    s0 = s0.reshape(2, A_HEAD_DIM, A_HEAD_DIM, rows)
    tt = _pick(t, (16, 8))
    y, s_fin = _wkv_scan(seq.reshape(WKV_N_SRC, t, rows, LANES), s0, tt)
    y = y.reshape(t, b * A_WIDTH)
    new_wkv = s_fin.reshape(2, A_HEAD_DIM, A_HEAD_DIM, b, A_HEADS // 2).transpose(3, 4, 0, 2, 1)
    new_wkv = new_wkv.reshape(b, A_HEADS, A_HEAD_DIM, A_HEAD_DIM)

    ct0 = jnp.concatenate([c0.transpose(0, 1, 3, 2).reshape(b, B_QK_WIDTH, B_V_DIM),
                           n0.reshape(b, B_QK_WIDTH, 1),
                           jnp.zeros((b, B_QK_WIDTH, LANES - 1), F32)], axis=-1)
    m0p = jnp.pad(m0, ((0, 0), (0, LANES - B_HEADS)))[:, None, :]
    lm = _pick(t, (256, 128))
    if t % LANES:
        t_pad = -t % LANES
        u_b = jnp.pad(u, ((0, 0), (0, t_pad), (0, 0)))
        lm = LANES
    else:
        u_b = u
    yb, ct, m_out = _mlstm(u_b, conv0, ct0, m0p, p, lm, t)
    yb = yb[:, :t].reshape(m, B_V_WIDTH)
    new_c = ct[:, :, 0:B_V_DIM].reshape(b, B_HEADS, B_QK_DIM, B_V_DIM).transpose(0, 1, 3, 2)
    new_n = ct[:, :, B_V_DIM].reshape(b, B_HEADS, B_QK_DIM)
    new_m = m_out[:, 0, 0:B_HEADS]

    x2 = x.reshape(m, D_MODEL)
    x2 = _even_out(y, bonus, g, yb, x2, p, b, _pick(t, (256, 128, 64, 32, 16, 8)))
    x2 = _experts(x2, None, None, p['ffn_g'], p['ffn_u'], p['ffn_d'], p['ln2_g'], p['ln2_b'], tm, routed=False)
    return x2.reshape(b, t, D_MODEL), (new_shift, new_wkv, new_conv, new_c, new_n, new_m)


def _odd_layer(x, past_k, past_v, past_ki, p):
    b, t, _ = x.shape
    m = b * t
    tm = _pick(m, (512, 256, 128, 64, 32, 16, 8))
    u = _matmul(x.reshape(m, D_MODEL), p['w_in'], tm, OD_COLS_PAD).reshape(b, t, OD_COLS_PAD)
    k_new = u[:, :, OD_K:OD_K + 256]
    v_new = u[:, :, OD_V:OD_V + 256]
    ki_new = u[:, :, OD_KI:OD_KI + IDX_DIM]
    if past_k is None:
        keys_k, keys_v, keys_i = k_new, v_new, ki_new
        causal, limit, tq = True, 0, _pick(t, (2 * CHUNK, CHUNK))
        top_k = min(TOPK_MAX, t // 4)
    else:
        keys_k = jnp.concatenate([past_k.reshape(b, -1, 256), k_new], axis=1)
        keys_v = jnp.concatenate([past_v.reshape(b, -1, 256), v_new], axis=1)
        keys_i = jnp.concatenate([past_ki, ki_new], axis=1)
        limit = keys_k.shape[1]
        causal, tq = False, t
        top_k = min(TOPK_MAX, limit // 4)
    s_pad = -keys_k.shape[1] % LANES
    pad_s = lambda z: jnp.pad(z, ((0, 0), (0, s_pad), (0, 0))) if s_pad else z
    kt = pad_s(keys_k).transpose(0, 2, 1).astype(BF16)
    vv = pad_s(keys_v).astype(BF16).reshape(b, -1, C_KV_HEADS, C_HEAD_DIM)
    vv = jnp.concatenate([vv, jnp.ones_like(vv)], axis=-1).reshape(b, -1, 2 * C_KV_HEADS * C_HEAD_DIM)
    kit = jnp.pad(pad_s(keys_i), ((0, 0), (0, 0), (0, LANES - IDX_DIM))).transpose(0, 2, 1).astype(BF16)
    qi = u[:, :, OD_QI:OD_QI + IDX_HEADS * IDX_DIM].reshape(b, t, IDX_HEADS, IDX_DIM).transpose(0, 2, 1, 3)
    qi = jnp.pad(qi, ((0, 0), (0, 0), (0, 0), (0, LANES - IDX_DIM))).astype(BF16)
    att = _dsa(u, qi, kt, vv, kit, tq, top_k, causal, limit)
    x2 = _proj_ln(att.reshape(m, C_WIDTH), x.reshape(m, D_MODEL), p['w_out'], p['ln1_g'], p['ln1_b'],
                  _pick(m, (512, 256, 128, 64, 32, 16, 8)))
    x2 = _experts(x2, p['router_w'], p['router_b'], p['e_gate'], p['e_up'], p['e_down'],
                  p['ln2_g'], p['ln2_b'], _pick(m, (1024, 512, 256, 128)), routed=True)
    st = (k_new.reshape(b, t, C_KV_HEADS, C_HEAD_DIM), v_new.reshape(b, t, C_KV_HEADS, C_HEAD_DIM), ki_new)
    return x2.reshape(b, t, D_MODEL), st


def _trunk(x, shift0, wkv0, conv0, c0, n0, m0, past_k, past_v, past_ki, ev, od):
    even_out = [[] for _ in range(6)]
    odd_out = [[] for _ in range(3)]
    for layer in range(DEPTH):
        i = layer // 2
        if layer % 2 == 0:
            x, st = _even_layer(x, shift0[i], wkv0[i], conv0[i], c0[i], n0[i], m0[i], ev[i])
            for acc, val in zip(even_out, st):
                acc.append(val)
        else:
            if past_k is None:
                x, st = _odd_layer(x, None, None, None, od[i])
            else:
                x, st = _odd_layer(x, past_k[i], past_v[i], past_ki[i], od[i])
            for acc, val in zip(odd_out, st):
                acc.append(val)
    return x, [jnp.stack(v) for v in even_out], [jnp.stack(v) for v in odd_out]


def kernel(x_prompt, x_sample, state_shift, state_wkv, state_conv, state_c, state_n, state_m,
           cache_k, cache_v, cache_idx_k,
           ev_w_in, a_mu, a_w0, a_w2, a_a0, a_a2, a_g2, a_k_k, a_k_a, a_r_k, a_ln_g, a_ln_b,
           b_conv_w, b_conv_b, b_i_bias, b_f_bias, b_ln_g, b_ln_b, ev_w_out, ev_ln1_g, ev_ln1_b,
           ffn_w_gate, ffn_w_up, ffn_w_down, ev_ln2_g, ev_ln2_b,
           od_w_in, od_w_out, od_ln1_g, od_ln1_b, moe_w_router, moe_b_router,
           moe_w_gate, moe_w_up, moe_w_down, od_ln2_g, od_ln2_b):
    ew = {'w_in': ev_w_in, 'mu': a_mu, 'w0': a_w0, 'w2': a_w2, 'a0': a_a0, 'a2': a_a2, 'g2': a_g2,
          'k_k': a_k_k, 'k_a': a_k_a, 'r_k': a_r_k, 'a_ln_g': a_ln_g, 'a_ln_b': a_ln_b,
          'conv_w': b_conv_w, 'conv_b': b_conv_b, 'b_i': b_i_bias, 'b_f': b_f_bias,
          'b_ln_g': b_ln_g, 'b_ln_b': b_ln_b, 'w_out': ev_w_out, 'ln1_g': ev_ln1_g, 'ln1_b': ev_ln1_b,
          'ffn_gate': ffn_w_gate, 'ffn_up': ffn_w_up, 'ffn_down': ffn_w_down,
          'ln2_g': ev_ln2_g, 'ln2_b': ev_ln2_b}
    ow = {'w_in': od_w_in, 'w_out': od_w_out, 'ln1_g': od_ln1_g, 'ln1_b': od_ln1_b,
          'router': moe_w_router, 'router_b': moe_b_router, 'e_gate': moe_w_gate, 'e_up': moe_w_up,
          'e_down': moe_w_down, 'ln2_g': od_ln2_g, 'ln2_b': od_ln2_b}
    n_even, n_odd = ev_w_in.shape[0], od_w_in.shape[0]
    ev = [_prep_even({k: v[i] for k, v in ew.items()}) for i in range(n_even)]
    od = [_prep_odd({k: v[i] for k, v in ow.items()}) for i in range(n_odd)]

    bp = x_prompt.shape[0]
    z = functools.partial(jnp.zeros, dtype=F32)
    y_p, ep, op = _trunk(x_prompt,
                         z((n_even, bp, 1, A_COLS)), z((n_even, bp, A_HEADS, A_HEAD_DIM, A_HEAD_DIM)),
                         z((n_even, bp, 3, 2 * B_QK_WIDTH)), z((n_even, bp, B_HEADS, B_V_DIM, B_QK_DIM)),
                         z((n_even, bp, B_HEADS, B_QK_DIM)), z((n_even, bp, B_HEADS)),
                         None, None, None, ev, od)
    y_s, es, os_ = _trunk(x_sample, state_shift, state_wkv, state_conv, state_c, state_n, state_m,
                          cache_k, cache_v, cache_idx_k, ev, od)
    return (y_p, y_s,
            ep[0], es[0], ep[1], es[1], ep[2], es[2], ep[3], es[3], ep[4], es[4], ep[5], es[5],
            op[0], os_[0], op[1], os_[1], op[2], os_[2])
```

```python
import functools
import math

import jax
import jax.numpy as jnp
from jax import lax
from jax.experimental import pallas as pl
from jax.experimental.pallas import tpu as pltpu

F32 = jnp.float32
BF16 = jnp.bfloat16
I32 = jnp.int32

D_MODEL = 1024
DEPTH = 4
CHUNK = 64
A_HEADS, A_HEAD_DIM, A_WIDTH = 16, 64, 1024
A_LORA = 256
A_COLS = 3 * A_WIDTH + A_LORA
B_HEADS, B_QK_DIM, B_V_DIM = 8, 64, 128
B_QK_WIDTH, B_V_WIDTH = 512, 1024
C_HEADS, C_KV_HEADS, C_HEAD_DIM, C_GROUP = 8, 2, 128, 4
C_WIDTH = 1024
IDX_HEADS, IDX_DIM = 8, 64
TOPK_MAX = 256
N_EXPERTS = 8
D_FF_EXPERT = 1408
ALPHA = (2 * DEPTH) ** 0.25
LN_EPS = 1e-5

LANES = 128
SUBLANES = 8
VMEM_LIMIT = 56 * 1024 * 1024

EV_R, EV_K, EV_V, EV_QK, EV_VB, EV_O, EV_LORA, EV_GATE = 0, 1024, 2048, 3072, 4096, 5120, 6144, 6400
EV_COLS_PAD = 6528
OD_Q, OD_K, OD_V, OD_QI, OD_KI = 0, 1024, 1280, 1536, 2048
OD_COLS_PAD = 2176

INT_MIN = -2 ** 31


def _cparams(*sem):
    return pltpu.CompilerParams(dimension_semantics=sem, vmem_limit_bytes=VMEM_LIMIT)


def _layer_norm(z, g, b):
    mu = jnp.mean(z, axis=-1, keepdims=True)
    d = z - mu
    var = jnp.mean(d * d, axis=-1, keepdims=True)
    return d * lax.rsqrt(var + LN_EPS) * g + b


def _hilo_dot(x, e):
    hi = x.astype(BF16)
    lo = (x - hi.astype(F32)).astype(BF16)
    return jnp.dot(hi, e, preferred_element_type=F32) + jnp.dot(lo, e, preferred_element_type=F32)


def _segsum(x, e):
    return _hilo_dot(_hilo_dot(x, e[0]), e[1])


def _mm_kernel(x_ref, w_ref, o_ref):
    o_ref[...] = jnp.dot(x_ref[...].astype(BF16), w_ref[...], preferred_element_type=F32)


def _matmul(x, w, tm, tn):
    m, k = x.shape
    n = w.shape[1]
    return pl.pallas_call(
        _mm_kernel,
        grid=(n // tn, m // tm),
        in_specs=[pl.BlockSpec((tm, k), lambda j, i: (i, 0)),
                  pl.BlockSpec((k, tn), lambda j, i: (0, j))],
        out_specs=pl.BlockSpec((tm, tn), lambda j, i: (i, j)),
        out_shape=jax.ShapeDtypeStruct((m, n), F32),
        compiler_params=_cparams("parallel", "parallel"),
        name="proj_in",
    )(x, w)


def _rwkv_pre_kernel(ur, uk, uv, ul, s0m, s0l, mum, mul_, w0, a0, kk_, ka_, rk_, w2, a2, g2, er_ref, eb_ref,
                     o_seq, o_g, o_bonus, prev_m, prev_l):
    @pl.when(pl.program_id(1) == 0)
    def _():
        prev_m[...] = s0m[0]
        prev_l[...] = s0l[0]

    tm = ur.shape[1]
    row = lax.broadcasted_iota(I32, (tm, 1), 0)

    def lerp(x, prev, mu):
        shifted = jnp.where(row == 0, prev, pltpu.roll(x, 1, 0))
        return x + (shifted - x) * mu

    r_raw, k_raw, v_raw, l_raw = ur[0], uk[0], uv[0], ul[0]
    r = lerp(r_raw, prev_m[:, 0:1024], mum[:, 0:1024])
    k = lerp(k_raw, prev_m[:, 1024:2048], mum[:, 1024:2048])
    v = lerp(v_raw, prev_m[:, 2048:3072], mum[:, 2048:3072])
    lo = lerp(l_raw, prev_l[...], mul_[...])
    prev_m[:, 0:1024] = r_raw[tm - 1:tm, :]
    prev_m[:, 1024:2048] = k_raw[tm - 1:tm, :]
    prev_m[:, 2048:3072] = v_raw[tm - 1:tm, :]
    prev_l[...] = l_raw[tm - 1:tm, :]

    zw = w0[...] + jnp.dot(jnp.tanh(lo).astype(BF16), w2[...], preferred_element_type=F32)
    decay = jnp.exp(-(math.exp(-0.5) * jax.nn.sigmoid(zw)))
    a = jax.nn.sigmoid(a0[...] + jnp.dot(lo.astype(BF16), a2[...], preferred_element_type=F32))
    g = jnp.dot(jax.nn.sigmoid(lo).astype(BF16), g2[...], preferred_element_type=F32)

    e = (er_ref[...], eb_ref[...])
    kk = k * kk_[...]
    kk = kk * lax.rsqrt(jnp.maximum(_segsum(kk * kk, e), 1e-24))
    k2 = k * (1.0 + (a - 1.0) * ka_[...])
    bonus = _segsum(r * k2 * rk_[...], e) * v

    o_seq[WKV_SRC_R] = r
    o_seq[WKV_SRC_W] = decay
    o_seq[WKV_SRC_K] = k2
    o_seq[WKV_SRC_V] = v
    o_seq[WKV_SRC_A] = -kk
    o_seq[WKV_SRC_B] = kk * a
    o_g[...] = g
    o_bonus[...] = bonus


def _rwkv_pre(u, s0_main, s0_lora, p, tm):
    b, t, _ = u.shape
    blk = lambda w, c: pl.BlockSpec((1, tm, w), lambda i, j, c=c: (i, j, c))
    full = lambda arr: pl.BlockSpec(arr.shape, lambda i, j: (0,) * arr.ndim)
    params = [p['mu_main'], p['mu_lora'], p['w0'], p['a0'], p['k_k'], p['k_a'], p['r_k'],
              p['w2p'], p['a2p'], p['g2p'], p['seg_red'], p['seg_bc']]
    out = jax.ShapeDtypeStruct((t, b * A_WIDTH), F32)
    return pl.pallas_call(
        _rwkv_pre_kernel,
        grid=(b, t // tm),
        in_specs=[blk(1024, EV_R // 1024), blk(1024, EV_K // 1024), blk(1024, EV_V // 1024),
                  blk(A_LORA, EV_LORA // A_LORA),
                  pl.BlockSpec((1, 1, 3072), lambda i, j: (i, 0, 0)),
                  pl.BlockSpec((1, 1, A_LORA), lambda i, j: (i, 0, 0))] + [full(a) for a in params],
        out_specs=[pl.BlockSpec((WKV_N_SRC, tm, A_WIDTH), lambda i, j: (0, j, i)),
                   pl.BlockSpec((tm, A_WIDTH), lambda i, j: (j, i)),
                   pl.BlockSpec((tm, A_WIDTH), lambda i, j: (j, i))],
        out_shape=[jax.ShapeDtypeStruct((WKV_N_SRC, t, b * A_WIDTH), F32), out, out],
        scratch_shapes=[pltpu.VMEM((1, 3072), F32), pltpu.VMEM((1, A_LORA), F32)],
        compiler_params=_cparams("parallel", "arbitrary"),
        name="rwkv_pre",
    )(u, u, u, u, s0_main, s0_lora, *params)


WKV_N_SRC = 6
WKV_SRC_R, WKV_SRC_W, WKV_SRC_K, WKV_SRC_V, WKV_SRC_A, WKV_SRC_B = range(WKV_N_SRC)
WKV_VGROUP = 4
WKV_KCHUNK = 32


def _wkv_kernel(seq_ref, s0_ref, y_ref, s_ref, tr, sa_buf, ybuf):
    @pl.when(pl.program_id(1) == 0)
    def _():
        s_ref[...] = s0_ref[...]

    n_t = seq_ref.shape[1]
    d = A_HEAD_DIM
    vspan = WKV_VGROUP * SUBLANES
    groups = [(p, g) for p in range(2) for g in range(d // vspan)]
    n_kc = d // WKV_KCHUNK

    def vrows(g, j):
        return slice(g * vspan + j * SUBLANES, g * vspan + (j + 1) * SUBLANES)

    def transpose_tile(i, tok):
        tr[i, tok] = jnp.transpose(seq_ref[i, tok])

    for i in range(WKV_N_SRC):
        transpose_tile(i, 0)
    transpose_tile(WKV_SRC_A, min(1, n_t - 1))

    for p, g in groups:
        acc = [jnp.zeros((SUBLANES, LANES), F32) for _ in range(WKV_VGROUP)]
        for kx in range(d):
            ab = tr[WKV_SRC_A, 0, p * d + kx:p * d + kx + 1, :]
            for j in range(WKV_VGROUP):
                acc[j] = acc[j] + s_ref[p, kx, vrows(g, j), :] * ab
        for j in range(WKV_VGROUP):
            sa_buf[p, vrows(g, j), :] = acc[j]

    def step(t, carry):
        tn = jnp.minimum(t + 1, n_t - 1)
        for gi, (p, g) in enumerate(groups):
            vv = [tr[WKV_SRC_V, t, p * d + g * vspan + j * SUBLANES:p * d + g * vspan + (j + 1) * SUBLANES, :]
                  for j in range(WKV_VGROUP)]
            sa = [sa_buf[p, vrows(g, j), :] for j in range(WKV_VGROUP)]
            zero = tuple(jnp.zeros((SUBLANES, LANES), F32) for _ in range(2 * WKV_VGROUP))

            def key_chunk(kc, accs, gi=gi, p=p, g=g, vv=vv, sa=sa):
                yacc, san = list(accs[:WKV_VGROUP]), list(accs[WKV_VGROUP:])
                for i in range(WKV_KCHUNK):
                    kx = kc * WKV_KCHUNK + i
                    row = pl.ds(p * d + kx, 1)
                    wb = tr[WKV_SRC_W, t, row, :]
                    bb = tr[WKV_SRC_B, t, row, :]
                    kb = tr[WKV_SRC_K, t, row, :]
                    rb = tr[WKV_SRC_R, t, row, :]
                    an = tr[WKV_SRC_A, tn, row, :]
                    for j in range(WKV_VGROUP):
                        n = s_ref[p, kx, vrows(g, j), :] * wb + sa[j] * bb + vv[j] * kb
                        s_ref[p, kx, vrows(g, j), :] = n
                        yacc[j] = yacc[j] + n * rb
                        san[j] = san[j] + n * an
                tile = jnp.minimum(gi * n_kc + kc, WKV_N_SRC - 1)
                ahead = jnp.where(tile == WKV_SRC_A, 2, 1)
                transpose_tile(tile, jnp.minimum(t + ahead, n_t - 1))
                return tuple(yacc) + tuple(san)

            accs = lax.fori_loop(0, n_kc, key_chunk, zero)
            yacc, san = accs[:WKV_VGROUP], accs[WKV_VGROUP:]
            for j in range(WKV_VGROUP):
                ybuf[t, p * d + g * vspan + j * SUBLANES:p * d + g * vspan + (j + 1) * SUBLANES, :] = yacc[j]
                sa_buf[p, vrows(g, j), :] = san[j]
        return carry

    lax.fori_loop(0, n_t, step, 0)

    def transpose_out(t, carry):
        y_ref[t] = jnp.transpose(ybuf[t])
        return carry

    lax.fori_loop(0, n_t, transpose_out, 0, unroll=4)


def _wkv_scan(seq, s0, tt):
    _, t, rows, _ = seq.shape
    d = A_HEAD_DIM
    st = pl.BlockSpec((2, d, d, LANES), lambda g, i: (0, 0, 0, g))
    return pl.pallas_call(
        _wkv_kernel,
        grid=(rows // LANES, t // tt),
        in_specs=[pl.BlockSpec((WKV_N_SRC, tt, LANES, LANES), lambda g, i: (0, i, g, 0)), st],
        out_specs=[pl.BlockSpec((tt, LANES, LANES), lambda g, i: (i, g, 0)), st],
        out_shape=[jax.ShapeDtypeStruct((t, rows, LANES), F32), jax.ShapeDtypeStruct((2, d, d, rows), F32)],
        scratch_shapes=[pltpu.VMEM((WKV_N_SRC, tt, LANES, LANES), F32), pltpu.VMEM((2, d, LANES), F32),
                        pltpu.VMEM((tt, LANES, LANES), F32)],
        compiler_params=_cparams("parallel", "arbitrary"),
        name="wkv_scan",
    )(seq, s0)


def _mlstm_kernel(uqk, uv, uo, ug, conv0, ct0, m0, cw, cb, gb, lng, lnb,
                  o_y, o_ct, o_m, ct, m_scr, carry, *, t_valid):
    tb = pl.program_id(1)

    @pl.when(tb == 0)
    def _():
        ct[...] = ct0[...]
        m_scr[...] = m0[...]
        carry[...] = conv0[...]

    for bi in range(uqk.shape[0]):
        one = pl.ds(bi, 1)
        _mlstm_chunk(tb, uqk.at[one], uv.at[one], uo.at[one], ug.at[one], cw, cb, gb, lng, lnb,
                     o_y.at[one], ct.at[bi], m_scr.at[bi], carry.at[bi], t_valid=t_valid)

    @pl.when(tb == pl.num_programs(1) - 1)
    def _():
        o_ct[...] = ct[...]
        o_m[...] = m_scr[...]


def _mlstm_chunk(tb, uqk, uv, uo, ug, cw, cb, gb, lng, lnb, o_y, ct, m_scr, carry, *, t_valid):
    L = uqk.shape[1]
    row = lax.broadcasted_iota(I32, (L, 1), 0)
    x = uqk[0]

    def shifted(j):
        out = pltpu.roll(x, j, 0)
        for i in range(j):
            out = jnp.where(row == i, carry[3 + i - j:4 + i - j, :], out)
        return out

    conv = x * cw[3:4, :] + shifted(1) * cw[2:3, :] + shifted(2) * cw[1:2, :] + shifted(3) * cw[0:1, :] + cb[...]
    carry[...] = x[L - 3:L, :]
    qk = conv * jax.nn.sigmoid(conv)
    q_all = qk[:, 0:B_QK_WIDTH]
    k_t = jnp.transpose(qk[:, B_QK_WIDTH:2 * B_QK_WIDTH] * (B_QK_DIM ** -0.5))

    lane = lax.broadcasted_iota(I32, (L, LANES), 1)
    gz = ug[0] + gb[...]
    lg = jnp.where(lane < B_HEADS, gz, jnp.minimum(gz, 0.0) - jnp.log(1.0 + jnp.exp(-jnp.abs(gz))))
    valid = (row + tb * L) < t_valid
    lg = jnp.where(valid, lg, jnp.where(lane < B_HEADS, -jnp.inf, 0.0))
    lg_t = jnp.transpose(lg)
    ti = lax.broadcasted_iota(I32, (L, L), 0)
    si = lax.broadcasted_iota(I32, (L, L), 1)
    tril = ti >= si
    tril_f = tril.astype(F32)
    triu_f = (ti <= si).astype(F32)
    lg_fin = jnp.where(lane < B_HEADS, 0.0, lg)
    bcum = jnp.dot(tril_f, lg_fin, preferred_element_type=F32, precision=lax.Precision.HIGHEST)
    bcum_t = jnp.dot(jnp.transpose(lg_fin), triu_f, preferred_element_type=F32,
                     precision=lax.Precision.HIGHEST)

    head_q = lax.shift_right_logical(lax.broadcasted_iota(I32, (L, B_QK_WIDTH), 1), 6)
    lane1 = lax.broadcasted_iota(I32, (1, LANES), 1)
    ones_col = (lax.broadcasted_iota(I32, (L, LANES), 1) == 0).astype(F32)
    m_vec = m_scr[...]
    m_new_vec = m_vec
    lane_ct = lax.broadcasted_iota(I32, (B_QK_DIM, 2 * LANES), 1)
    ct_old = ct[...]
    ct_old_bf = ct_old.astype(BF16)
    k_bf = k_t.astype(BF16)
    vv = uv[0]
    oo = uo[0]
    for h in range(B_HEADS):
        hs = slice(h * B_QK_DIM, (h + 1) * B_QK_DIM)
        vs = slice(h * B_V_DIM, (h + 1) * B_V_DIM)
        bc_col = bcum[:, B_HEADS + h:B_HEADS + h + 1]
        bc_row = bcum_t[B_HEADS + h:B_HEADS + h + 1, :]
        ic_row = lg_t[h:h + 1, :]
        gtot = bc_row[:, L - 1:L]
        m_prev = m_vec[:, h:h + 1]
        dmat = jnp.where(tril, bc_col - bc_row + ic_row, -jnp.inf)
        inter = bc_col + m_prev
        m_t = jnp.maximum(inter, jnp.max(dmat, axis=1, keepdims=True))
        w_intra = jnp.exp(dmat - m_t)
        w_inter = jnp.exp(inter - m_t)
        q_m = jnp.where(head_q == h, q_all, 0.0).astype(BF16)
        s = jnp.dot(q_m, k_bf, preferred_element_type=F32) * w_intra
        v_ext = jnp.concatenate([vv[:, vs], ones_col], axis=1).astype(BF16)
        intra = jnp.dot(s.astype(BF16), v_ext, preferred_element_type=F32)
        cross = jnp.dot(q_m, ct_old_bf, preferred_element_type=F32)
        num = intra[:, 0:B_V_DIM] + w_inter * cross[:, 0:B_V_DIM]
        den = jnp.sum(s, axis=1, keepdims=True) + w_inter * cross[:, B_V_DIM:B_V_DIM + 1]
        hh = num / jnp.maximum(jnp.abs(den), jnp.exp(-m_t))
        mu = jnp.mean(hh, axis=-1, keepdims=True)
        dd = hh - mu
        var = jnp.mean(dd * dd, axis=-1, keepdims=True)
        yn = dd * lax.rsqrt(var + LN_EPS) * lng[:, vs] + lnb[:, vs]
        o_y[0, :, vs] = jax.nn.sigmoid(oo[:, vs]) * yn
        lw = gtot - bc_row + ic_row
        m_new = jnp.maximum(gtot + m_prev, jnp.max(lw, axis=1, keepdims=True))
        w_s = jnp.exp(lw - m_new)
        dec = jnp.exp(gtot + m_prev - m_new)
        kw = k_t[hs, :] * w_s
        upd = jnp.dot(kw.astype(BF16), v_ext, preferred_element_type=F32)
        upd = jnp.where(lane_ct == B_V_DIM, jnp.sum(kw, axis=1, keepdims=True), upd)
        ct[hs, :] = dec * ct_old[hs, :] + upd
        m_new_vec = jnp.where(lane1 == h, m_new, m_new_vec)
    m_scr[...] = m_new_vec


MLSTM_ROWS = 1


def _mlstm(u, conv0, ct0, m0, p, L, t_valid):
    b, t, _ = u.shape
    nb = MLSTM_ROWS if b % MLSTM_ROWS == 0 else 1
    blk = lambda w, c: pl.BlockSpec((nb, L, w), lambda i, j, c=c: (i, j, c))
    full = lambda arr: pl.BlockSpec(arr.shape, lambda i, j: (0,) * arr.ndim)
    per_b = lambda arr: pl.BlockSpec((nb,) + arr.shape[1:], lambda i, j: (i, 0, 0))
    params = [p['conv_w'], p['conv_b'], p['gate_b'], p['b_ln_g'], p['b_ln_b']]
    return pl.pallas_call(
        functools.partial(_mlstm_kernel, t_valid=t_valid),
        grid=(b // nb, t // L),
        in_specs=[blk(1024, EV_QK // 1024), blk(1024, EV_VB // 1024), blk(1024, EV_O // 1024),
                  blk(LANES, EV_GATE // LANES), per_b(conv0), per_b(ct0), per_b(m0)] + [full(a) for a in params],
        out_specs=[pl.BlockSpec((nb, L, B_V_WIDTH), lambda i, j: (i, j, 0)),
                   pl.BlockSpec((nb, B_QK_WIDTH, 2 * LANES), lambda i, j: (i, 0, 0)),
                   pl.BlockSpec((nb, 1, LANES), lambda i, j: (i, 0, 0))],
        out_shape=[jax.ShapeDtypeStruct((b, t, B_V_WIDTH), F32),
                   jax.ShapeDtypeStruct((b, B_QK_WIDTH, 2 * LANES), F32),
                   jax.ShapeDtypeStruct((b, 1, LANES), F32)],
        scratch_shapes=[pltpu.VMEM((nb, B_QK_WIDTH, 2 * LANES), F32), pltpu.VMEM((nb, 1, LANES), F32),
                        pltpu.VMEM((nb, 3, 2 * B_QK_WIDTH), F32)],
        compiler_params=_cparams("parallel", "arbitrary"),
        name="mlstm",
    )(u, u, u, u, conv0, ct0, m0, *params)


def _even_out_kernel(y_ref, bonus_ref, g_ref, yb_ref, x_ref, lng, lnb, er_ref, eb_ref, woa, wob, g1, b1, o_ref):
    e = (er_ref[...], eb_ref[...])
    y = y_ref[...]
    mu = _segsum(y, e) * (1.0 / A_HEAD_DIM)
    d = y - mu
    var = _segsum(d * d, e) * (1.0 / A_HEAD_DIM)
    yn = d * lax.rsqrt(var + LN_EPS) * lng[...] + lnb[...]
    ya = (yn + bonus_ref[...]) * g_ref[...]
    mix = (jnp.dot(ya.astype(BF16), woa[...], preferred_element_type=F32)
           + jnp.dot(yb_ref[...].astype(BF16), wob[...], preferred_element_type=F32))
    o_ref[...] = _layer_norm(ALPHA * x_ref[...] + mix, g1[...], b1[...])


def _even_out(y, bonus, g, yb, x, p, b, tm):
    m = x.shape[0]
    nt = m // b // tm
    tmaj = pl.BlockSpec((tm, D_MODEL), lambda i, j: (j, i))
    row = pl.BlockSpec((tm, D_MODEL), lambda i, j: (i * nt + j, 0))
    full = lambda arr: pl.BlockSpec(arr.shape, lambda i, j: (0,) * arr.ndim)
    params = [p['a_ln_g'], p['a_ln_b'], p['seg_red'], p['seg_bc'], p['w_out_a'], p['w_out_b'], p['ln1_g'], p['ln1_b']]
    return pl.pallas_call(
        _even_out_kernel,
        grid=(b, nt),
        in_specs=[tmaj] * 3 + [row] * 2 + [full(a) for a in params],
        out_specs=row,
        out_shape=jax.ShapeDtypeStruct((m, D_MODEL), F32),
        compiler_params=_cparams("parallel", "parallel"),
        name="even_out",
    )(y, bonus, g, yb, x, *params)


def _proj_ln_kernel(a_ref, x_ref, w_ref, g1, b1, o_ref):
    mix = jnp.dot(a_ref[...].astype(BF16), w_ref[...], preferred_element_type=F32)
    o_ref[...] = _layer_norm(ALPHA * x_ref[...] + mix, g1[...], b1[...])


def _proj_ln(a, x, w, g1, b1, tm):
    m = x.shape[0]
    row = pl.BlockSpec((tm, D_MODEL), lambda i: (i, 0))
    full = lambda arr: pl.BlockSpec(arr.shape, lambda i: (0,) * arr.ndim)
    return pl.pallas_call(
        _proj_ln_kernel,
        grid=(m // tm,),
        in_specs=[row, row, full(w), full(g1), full(b1)],
        out_specs=row,
        out_shape=jax.ShapeDtypeStruct((m, D_MODEL), F32),
        compiler_params=_cparams("parallel"),
        name="proj_ln",
    )(a, x, w, g1, b1)


def _swiglu(xb, wg, wu, wd):
    hg = jnp.dot(xb, wg, preferred_element_type=F32)
    hu = jnp.dot(xb, wu, preferred_element_type=F32)
    hid = (hg * jax.nn.sigmoid(hg)) * hu
    return jnp.dot(hid.astype(BF16), wd, preferred_element_type=F32)


def _ffn_kernel(x_ref, wg, wu, wd, g2, b2, o_ref, xb):
    e = pl.program_id(1)

    @pl.when(e == 0)
    def _():
        xb[...] = x_ref[...].astype(BF16)
        o_ref[...] = jnp.zeros_like(o_ref)

    o_ref[...] += _swiglu(xb[...], wg[0], wu[0], wd[0])

    @pl.when(e == pl.num_programs(1) - 1)
    def _():
        o_ref[...] = _layer_norm(ALPHA * x_ref[...] + o_ref[...], g2[...], b2[...])


MOE_CHUNK = 256


def _moe_kernel(x_ref, wr, br, wg, wu, wd, g2, b2, o_ref, xb, comb, pos_scr, post_scr):
    e = pl.program_id(1)
    tm = x_ref.shape[0]

    @pl.when(e == 0)
    def _():
        x = x_ref[...]
        xb[...] = x.astype(BF16)
        o_ref[...] = jnp.zeros_like(o_ref)
        logits = jnp.dot(x, wr[...], preferred_element_type=F32, precision=lax.Precision.HIGHEST) + br[...]
        lane = lax.broadcasted_iota(I32, logits.shape, 1).astype(F32)
        v1 = jnp.max(logits, axis=-1, keepdims=True)
        i1 = jnp.min(jnp.where(logits == v1, lane, float(LANES)), axis=-1, keepdims=True)
        rest = jnp.where(lane == i1, -jnp.inf, logits)
        v2 = jnp.max(rest, axis=-1, keepdims=True)
        i2 = jnp.min(jnp.where(rest == v2, lane, float(LANES)), axis=-1, keepdims=True)
        e2 = jnp.exp(v2 - v1)
        den = 1.0 + e2
        comb[...] = jnp.where(lane == i1, 1.0 / den, jnp.where(lane == i2, e2 / den, 0.0))
        member = jnp.where(lane == i1, 1.0, jnp.where(lane == i2, 1.0, 0.0))
        earlier = (lax.broadcasted_iota(I32, (tm, tm), 0) > lax.broadcasted_iota(I32, (tm, tm), 1))
        pos = jnp.dot(jnp.where(earlier, 1.0, 0.0).astype(BF16), member.astype(BF16), preferred_element_type=F32)
        posm = jnp.where(member > 0.0, pos, -1.0)
        pos_scr[...] = posm
        post_scr[...] = jnp.transpose(posm)

    sel = lax.broadcasted_iota(I32, (tm, LANES), 1) == e
    gate = jnp.sum(jnp.where(sel, comb[...], 0.0), axis=-1, keepdims=True)
    slot_col = jnp.sum(jnp.where(sel, pos_scr[...], 0.0), axis=-1, keepdims=True)
    slot_row = post_scr[pl.ds(e, 1), :]
    count = jnp.max(slot_row) + 1.0

    def run_slots(base, n):
        rows = lax.broadcasted_iota(I32, (n, 1), 0).astype(F32) + float(base)
        gather = jnp.where(slot_row == rows, 1.0, 0.0).astype(BF16)
        xg = jnp.dot(gather, xb[...], preferred_element_type=F32).astype(BF16)
        y = _swiglu(xg, wg[0], wu[0], wd[0])
        cols = lax.broadcasted_iota(I32, (1, n), 1).astype(F32) + float(base)
        scatter = jnp.where(slot_col == cols, 1.0, 0.0).astype(BF16)
        o_ref[...] += gate * jnp.dot(scatter, y.astype(BF16), preferred_element_type=F32)

    chunk = min(MOE_CHUNK, tm)
    half = chunk // 2
    for c in range(tm // chunk):
        lo = c * chunk

        @pl.when(count > lo + half)
        def _(lo=lo):
            run_slots(lo, chunk)

        @pl.when(jnp.logical_and(count > lo, count <= lo + half))
        def _(lo=lo):
            run_slots(lo, half)

    @pl.when(e == pl.num_programs(1) - 1)
    def _():
        o_ref[...] = _layer_norm(ALPHA * x_ref[...] + o_ref[...], g2[...], b2[...])


def _experts(x, wr, br, wg, wu, wd, g2, b2, tm, routed):
    m = x.shape[0]
    ne, _, dff = wg.shape
    row = pl.BlockSpec((tm, D_MODEL), lambda i, e: (i, 0))
    full = lambda arr: pl.BlockSpec(arr.shape, lambda i, e: (0,) * arr.ndim)
    wspecs = [pl.BlockSpec((1, D_MODEL, dff), lambda i, e: (e, 0, 0)),
              pl.BlockSpec((1, D_MODEL, dff), lambda i, e: (e, 0, 0)),
              pl.BlockSpec((1, dff, D_MODEL), lambda i, e: (e, 0, 0))]
    common = dict(grid=(m // tm, ne), out_specs=row, out_shape=jax.ShapeDtypeStruct((m, D_MODEL), F32),
                  compiler_params=_cparams("parallel", "arbitrary"))
    if not routed:
        return pl.pallas_call(
            _ffn_kernel, in_specs=[row] + wspecs + [full(g2), full(b2)],
            scratch_shapes=[pltpu.VMEM((tm, D_MODEL), BF16)], name="experts_dense", **common,
        )(x, wg, wu, wd, g2, b2)
    return pl.pallas_call(
        _moe_kernel, in_specs=[row, full(wr), full(br)] + wspecs + [full(g2), full(b2)],
        scratch_shapes=[pltpu.VMEM((tm, D_MODEL), BF16), pltpu.VMEM((tm, LANES), F32),
                        pltpu.VMEM((tm, LANES), F32), pltpu.VMEM((LANES, tm), F32)],
        name="experts_routed", **common,
    )(x, wr, br, wg, wu, wd, g2, b2)


DSA_COARSE_BITS = 28


def _dsa_kernel(q_ref, qi_ref, wi_ref, kt_ref, v_ref, kit_ref, *rest, top_k, causal, limit_const, blk_off, has_prev):
    o_ref, key_scr, jcut_scr, thr_scr = rest[1:] if has_prev else rest
    tq = q_ref.shape[1]
    s_len = kt_ref.shape[2]
    if causal:
        row_chunk = lax.shift_right_logical(lax.broadcasted_iota(I32, (tq, 1), 0), CHUNK.bit_length() - 1)
        chunk = (pl.program_id(1) + blk_off) * (tq // CHUNK) + row_chunk
        limit = (chunk + 1) * CHUNK
        kth = jnp.minimum(top_k, limit).astype(F32)
    else:
        limit = limit_const
        kth = jnp.full((tq, 1), min(top_k, limit_const), F32)
    idx = lax.broadcasted_iota(I32, (tq, s_len), 1)

    kit = kit_ref[0]
    wi = wi_ref[0] * (IDX_HEADS ** -0.5)
    score = jnp.zeros((tq, s_len), F32)
    for h in range(IDX_HEADS):
        rel = jnp.dot(qi_ref[0, h], kit, preferred_element_type=F32)
        score = score + wi[:, IDX_DIM + h:IDX_DIM + h + 1] * jnp.maximum(rel, 0.0)
    score = jnp.where(score == 0.0, 0.0, score)
    bits = pltpu.bitcast(score, I32)
    key = jnp.where(bits < 0, bits ^ 0x7FFFFFFF, bits)
    key_scr[...] = jnp.where(idx < limit, key, INT_MIN)

    n_grp = 2 if tq % (4 * SUBLANES) == 0 else 1
    rg = tq // n_grp
    rows_of = lambda g: slice(g * rg, (g + 1) * rg)

    def count_ge(thrs):
        return [jnp.sum(jnp.where(key_scr[rows_of(g), :] >= thrs[g], 1.0, 0.0), axis=1, keepdims=True)
                for g in range(n_grp)]

    def bit_step(i, thrs):
        bit = lax.shift_left(jnp.int32(1), 31 - i)
        cands = [t + bit for t in thrs]
        cnts = count_ge(cands)
        return tuple(jnp.where(cnts[g] >= kth[rows_of(g)], cands[g], thrs[g]) for g in range(n_grp))

    thrs = lax.fori_loop(0, DSA_COARSE_BITS, bit_step,
                         tuple(jnp.full((rg, 1), INT_MIN, I32) for _ in range(n_grp)), unroll=4)
    thr_scr[...] = jnp.concatenate(thrs, axis=0)
    n_coarse = jnp.concatenate(count_ge(thrs), axis=0)

    @pl.when(jnp.max(n_coarse - kth) > 0.0)
    def _():
        fine = lax.fori_loop(DSA_COARSE_BITS, 32, bit_step, tuple(thr_scr[rows_of(g), :] for g in range(n_grp)),
                             unroll=4)
        thr_scr[...] = jnp.concatenate(fine, axis=0)

    thr = thr_scr[...]
    keyv = key_scr[...]
    n_ge = jnp.sum(jnp.where(keyv >= thr, 1.0, 0.0), axis=1, keepdims=True)
    jcut_scr[...] = jnp.full((tq, 1), s_len, I32)

    @pl.when(jnp.max(n_ge - kth) > 0.0)
    def _():
        need = kth - jnp.sum(jnp.where(key_scr[...] > thr, 1.0, 0.0), axis=1, keepdims=True)
        n_bits = max(1, (s_len - 1).bit_length())

        def idx_step(i, j):
            cand = j + lax.shift_left(jnp.int32(1), n_bits - 1 - i)
            cnt = jnp.sum(jnp.where(key_scr[...] == thr, jnp.where(idx < cand, 1.0, 0.0), 0.0),
                          axis=1, keepdims=True)
            return jnp.where(cnt < need, cand, j)

        jcut_scr[...] = lax.fori_loop(0, n_bits, idx_step, jnp.zeros((tq, 1), I32))

    jcut = jcut_scr[...]
    bias = jnp.where(keyv > thr, 0.0,
                     jnp.where(keyv == thr, jnp.where(idx <= jcut, 0.0, -jnp.inf), -jnp.inf)).astype(BF16)

    q = q_ref[0] * (C_HEAD_DIM ** -0.5)
    for grp in range(C_KV_HEADS):
        kt = kt_ref[0, grp * C_HEAD_DIM:(grp + 1) * C_HEAD_DIM, :]
        v_ext = v_ref[0, :, grp * 2 * C_HEAD_DIM:(grp + 1) * 2 * C_HEAD_DIM]
        hsl = [slice((grp * C_GROUP + hg) * C_HEAD_DIM, (grp * C_GROUP + hg + 1) * C_HEAD_DIM)
               for hg in range(C_GROUP)]
        qg = jnp.concatenate([q[:, sl] for sl in hsl], axis=0).astype(BF16)
        lg = jnp.dot(qg, kt, preferred_element_type=F32).astype(BF16)
        ps = []
        for hg in range(C_GROUP):
            logits = lg[hg * tq:(hg + 1) * tq, :] + bias
            m_tile = functools.reduce(jnp.maximum, [logits[:, i * LANES:(i + 1) * LANES]
                                                    for i in range(s_len // LANES)])
            mx = jnp.max(m_tile.astype(F32), axis=1, keepdims=True).astype(BF16)
            ps.append(jnp.exp(logits - mx))
        out = jnp.dot(jnp.concatenate(ps, axis=0), v_ext, preferred_element_type=F32)
        for hg in range(C_GROUP):
            rows = slice(hg * tq, (hg + 1) * tq)
            o_ref[0, :, hsl[hg]] = out[rows, 0:C_HEAD_DIM] / out[rows, C_HEAD_DIM:C_HEAD_DIM + 1]


DSA_KEY_STEP = 512
DSA_WIDE_Q_KEYS = 3584


def _dsa_call(u, qi_heads, kt, v, kit, prev, tq, top_k, causal, limit_const, blk_off, n_blk, s_len):
    b, t, _ = u.shape
    in_specs = [pl.BlockSpec((1, tq, C_WIDTH), lambda i, j: (i, j + blk_off, OD_Q // C_WIDTH)),
                pl.BlockSpec((1, IDX_HEADS, tq, LANES), lambda i, j: (i, 0, j + blk_off, 0)),
                pl.BlockSpec((1, tq, LANES), lambda i, j: (i, j + blk_off, OD_KI // LANES)),
                pl.BlockSpec((1, 2 * C_HEAD_DIM, s_len), lambda i, j: (i, 0, 0)),
                pl.BlockSpec((1, s_len, 2 * C_KV_HEADS * C_HEAD_DIM), lambda i, j: (i, 0, 0)),
                pl.BlockSpec((1, LANES, s_len), lambda i, j: (i, 0, 0))]
    args = [u, qi_heads, u, kt, v, kit]
    aliases = {}
    if prev is not None:
        in_specs.append(pl.BlockSpec(memory_space=pl.ANY))
        args.append(prev)
        aliases = {len(args) - 1: 0}
    return pl.pallas_call(
        functools.partial(_dsa_kernel, top_k=top_k, causal=causal, limit_const=limit_const, blk_off=blk_off,
                          has_prev=prev is not None),
        grid=(b, n_blk),
        in_specs=in_specs,
        out_specs=pl.BlockSpec((1, tq, C_WIDTH), lambda i, j: (i, j + blk_off, 0)),
        out_shape=jax.ShapeDtypeStruct((b, t, C_WIDTH), F32),
        scratch_shapes=[pltpu.VMEM((tq, s_len), I32), pltpu.VMEM((tq, 1), I32), pltpu.VMEM((tq, 1), I32)],
        input_output_aliases=aliases,
        compiler_params=_cparams("parallel", "arbitrary"),
        name="dsa",
    )(*args)


def _dsa(u, qi_heads, kt, v, kit, tq, top_k, causal, limit_const):
    b, t, _ = u.shape
    s_full = kt.shape[2]
    if not causal or s_full % DSA_KEY_STEP:
        return _dsa_call(u, qi_heads, kt, v, kit, None, tq, top_k, causal, limit_const, 0, t // tq, s_full)
    att = None
    for cls in range(s_full // DSA_KEY_STEP):
        s_len = (cls + 1) * DSA_KEY_STEP
        tq_c = 2 * tq if s_len <= DSA_WIDE_Q_KEYS and DSA_KEY_STEP % (2 * tq) == 0 else tq
        per = DSA_KEY_STEP // tq_c
        att = _dsa_call(u, qi_heads, kt, v, kit, att, tq_c, top_k, causal, limit_const, cls * per, per, s_len)
    return att


def _prep_even(w):
    win = w['w_in']
    a, bq = win[:, :A_COLS], win[:, A_COLS:]
    cols = [a[:, 0:3072], bq[:, 0:3072], a[:, 3072:A_COLS], bq[:, 3072:3088],
            jnp.zeros((D_MODEL, EV_COLS_PAD - EV_GATE - 2 * B_HEADS), F32)]
    p = {'w_in': jnp.concatenate(cols, axis=1).astype(BF16)}
    mu = w['mu']
    p['mu_main'] = mu[None, 0:3072]
    p['mu_lora'] = mu[None, 3072:A_COLS]
    row = lambda v: v.reshape(1, -1)
    p['w0'], p['a0'], p['k_k'], p['k_a'] = row(w['w0']), row(w['a0']), row(w['k_k']), row(w['k_a'])
    p['r_k'] = row(w['r_k'])
    z = lambda n: jnp.zeros((n, A_WIDTH), F32)
    p['w2p'] = jnp.concatenate([w['w2'], z(192)], axis=0).astype(BF16)
    p['a2p'] = jnp.concatenate([z(64), w['a2'], z(128)], axis=0).astype(BF16)
    p['g2p'] = jnp.concatenate([z(128), w['g2']], axis=0).astype(BF16)
    seg = jnp.arange(A_WIDTH) // A_HEAD_DIM
    p['seg_red'] = (seg[:, None] == jnp.arange(LANES)[None, :]).astype(BF16)
    p['seg_bc'] = p['seg_red'].T
    p['a_ln_g'], p['a_ln_b'] = row(w['a_ln_g']), row(w['a_ln_b'])
    p['conv_w'], p['conv_b'] = w['conv_w'], row(w['conv_b'])
    p['gate_b'] = jnp.concatenate([w['b_i'], w['b_f'], jnp.zeros((LANES - 2 * B_HEADS,), F32)])[None]
    p['b_ln_g'], p['b_ln_b'] = row(w['b_ln_g']), row(w['b_ln_b'])
    p['w_out_a'] = w['w_out'][:A_WIDTH].astype(BF16)
    p['w_out_b'] = w['w_out'][A_WIDTH:].astype(BF16)
    p['ln1_g'], p['ln1_b'], p['ln2_g'], p['ln2_b'] = row(w['ln1_g']), row(w['ln1_b']), row(w['ln2_g']), row(w['ln2_b'])
    half = w['ffn_gate'].shape[1] // 2
    p['ffn_g'] = jnp.stack([w['ffn_gate'][:, :half], w['ffn_gate'][:, half:]]).astype(BF16)
    p['ffn_u'] = jnp.stack([w['ffn_up'][:, :half], w['ffn_up'][:, half:]]).astype(BF16)
    p['ffn_d'] = jnp.stack([w['ffn_down'][:half], w['ffn_down'][half:]]).astype(BF16)
    return p


def _prep_odd(w):
    row = lambda v: v.reshape(1, -1)
    p = {'w_in': jnp.pad(w['w_in'], ((0, 0), (0, OD_COLS_PAD - w['w_in'].shape[1]))).astype(BF16)}
    p['w_out'] = w['w_out'].astype(BF16)
    p['ln1_g'], p['ln1_b'], p['ln2_g'], p['ln2_b'] = row(w['ln1_g']), row(w['ln1_b']), row(w['ln2_g']), row(w['ln2_b'])
    p['router_w'] = jnp.pad(w['router'], ((0, 0), (0, LANES - N_EXPERTS)))
    p['router_b'] = jnp.concatenate([w['router_b'], jnp.full((LANES - N_EXPERTS,), -1e30, F32)])[None]
    p['e_gate'], p['e_up'], p['e_down'] = w['e_gate'].astype(BF16), w['e_up'].astype(BF16), w['e_down'].astype(BF16)
    return p


def _pick(n, pref):
    for c in pref:
        if n % c == 0:
            return c
    return n


def _even_layer(x, shift0, wkv0, conv0, c0, n0, m0, p):
    b, t, _ = x.shape
    m = b * t
    tm = _pick(m, (512, 256, 128, 64, 32, 16, 8))
    u = _matmul(x.reshape(m, D_MODEL), p['w_in'], tm, EV_COLS_PAD // 3).reshape(b, t, EV_COLS_PAD)

    new_shift = jnp.concatenate([u[:, t - 1:, 0:3072], u[:, t - 1:, EV_LORA:EV_LORA + A_LORA]], axis=-1)
    new_conv = jnp.concatenate([conv0, u[:, :, EV_QK:EV_QK + 1024]], axis=1)[:, t:]

    s0_main, s0_lora = shift0[:, :, 0:3072], shift0[:, :, 3072:A_COLS]
    tp = _pick(t, (256, 128, 64, 32, 16, 8))
    seq, g, bonus = _rwkv_pre(u, s0_main, s0_lora, p, tp)
    rows = b * A_HEADS // 2
    assert rows % LANES == 0, "batch must be a multiple of 16"
    s0 = wkv0.reshape(b, A_HEADS // 2, 2, A_HEAD_DIM, A_HEAD_DIM).transpose(2, 4, 3, 0, 1)
    s0 = s0.reshape(2, A_HEAD_DIM, A_HEAD_DIM, rows)
    tt = _pick(t, (16, 8))
    y, s_fin = _wkv_scan(seq.reshape(WKV_N_SRC, t, rows, LANES), s0, tt)
    y = y.reshape(t, b * A_WIDTH)
    new_wkv = s_fin.reshape(2, A_HEAD_DIM, A_HEAD_DIM, b, A_HEADS // 2).transpose(3, 4, 0, 2, 1)
    new_wkv = new_wkv.reshape(b, A_HEADS, A_HEAD_DIM, A_HEAD_DIM)

    ct0 = jnp.concatenate([c0.transpose(0, 1, 3, 2).reshape(b, B_QK_WIDTH, B_V_DIM),
                           n0.reshape(b, B_QK_WIDTH, 1),
                           jnp.zeros((b, B_QK_WIDTH, LANES - 1), F32)], axis=-1)
    m0p = jnp.pad(m0, ((0, 0), (0, LANES - B_HEADS)))[:, None, :]
    lm = _pick(t, (256, 128))
    if t % LANES:
        t_pad = -t % LANES
        u_b = jnp.pad(u, ((0, 0), (0, t_pad), (0, 0)))
        lm = LANES
    else:
        u_b = u
    yb, ct, m_out = _mlstm(u_b, conv0, ct0, m0p, p, lm, t)
    yb = yb[:, :t].reshape(m, B_V_WIDTH)
    new_c = ct[:, :, 0:B_V_DIM].reshape(b, B_HEADS, B_QK_DIM, B_V_DIM).transpose(0, 1, 3, 2)
    new_n = ct[:, :, B_V_DIM].reshape(b, B_HEADS, B_QK_DIM)
    new_m = m_out[:, 0, 0:B_HEADS]

    x2 = x.reshape(m, D_MODEL)
    x2 = _even_out(y, bonus, g, yb, x2, p, b, _pick(t, (256, 128, 64, 32, 16, 8)))
    x2 = _experts(x2, None, None, p['ffn_g'], p['ffn_u'], p['ffn_d'], p['ln2_g'], p['ln2_b'], tm, routed=False)
    return x2.reshape(b, t, D_MODEL), (new_shift, new_wkv, new_conv, new_c, new_n, new_m)


def _odd_layer(x, past_k, past_v, past_ki, p):
    b, t, _ = x.shape
    m = b * t
    tm = _pick(m, (512, 256, 128, 64, 32, 16, 8))
    u = _matmul(x.reshape(m, D_MODEL), p['w_in'], tm, OD_COLS_PAD).reshape(b, t, OD_COLS_PAD)
    k_new = u[:, :, OD_K:OD_K + 256]
    v_new = u[:, :, OD_V:OD_V + 256]
    ki_new = u[:, :, OD_KI:OD_KI + IDX_DIM]
    if past_k is None:
        keys_k, keys_v, keys_i = k_new, v_new, ki_new
        causal, limit, tq = True, 0, _pick(t, (2 * CHUNK, CHUNK))
        top_k = min(TOPK_MAX, t // 4)
    else:
        keys_k = jnp.concatenate([past_k.reshape(b, -1, 256), k_new], axis=1)
        keys_v = jnp.concatenate([past_v.reshape(b, -1, 256), v_new], axis=1)
        keys_i = jnp.concatenate([past_ki, ki_new], axis=1)
        limit = keys_k.shape[1]
        causal, tq = False, t
        top_k = min(TOPK_MAX, limit // 4)
    s_pad = -keys_k.shape[1] % LANES
    pad_s = lambda z: jnp.pad(z, ((0, 0), (0, s_pad), (0, 0))) if s_pad else z
    kt = pad_s(keys_k).transpose(0, 2, 1).astype(BF16)
    vv = pad_s(keys_v).astype(BF16).reshape(b, -1, C_KV_HEADS, C_HEAD_DIM)
    vv = jnp.concatenate([vv, jnp.ones_like(vv)], axis=-1).reshape(b, -1, 2 * C_KV_HEADS * C_HEAD_DIM)
    kit = jnp.pad(pad_s(keys_i), ((0, 0), (0, 0), (0, LANES - IDX_DIM))).transpose(0, 2, 1).astype(BF16)
    qi = u[:, :, OD_QI:OD_QI + IDX_HEADS * IDX_DIM].reshape(b, t, IDX_HEADS, IDX_DIM).transpose(0, 2, 1, 3)
    qi = jnp.pad(qi, ((0, 0), (0, 0), (0, 0), (0, LANES - IDX_DIM))).astype(BF16)
    att = _dsa(u, qi, kt, vv, kit, tq, top_k, causal, limit)
    x2 = _proj_ln(att.reshape(m, C_WIDTH), x.reshape(m, D_MODEL), p['w_out'], p['ln1_g'], p['ln1_b'],
                  _pick(m, (512, 256, 128, 64, 32, 16, 8)))
    x2 = _experts(x2, p['router_w'], p['router_b'], p['e_gate'], p['e_up'], p['e_down'],
                  p['ln2_g'], p['ln2_b'], _pick(m, (1024, 512, 256, 128)), routed=True)
    st = (k_new.reshape(b, t, C_KV_HEADS, C_HEAD_DIM), v_new.reshape(b, t, C_KV_HEADS, C_HEAD_DIM), ki_new)
    return x2.reshape(b, t, D_MODEL), st


def _trunk(x, shift0, wkv0, conv0, c0, n0, m0, past_k, past_v, past_ki, ev, od):
    even_out = [[] for _ in range(6)]
    odd_out = [[] for _ in range(3)]
    for layer in range(DEPTH):
        i = layer // 2
        if layer % 2 == 0:
            x, st = _even_layer(x, shift0[i], wkv0[i], conv0[i], c0[i], n0[i], m0[i], ev[i])
            for acc, val in zip(even_out, st):
                acc.append(val)
        else:
            if past_k is None:
                x, st = _odd_layer(x, None, None, None, od[i])
            else:
                x, st = _odd_layer(x, past_k[i], past_v[i], past_ki[i], od[i])
            for acc, val in zip(odd_out, st):
                acc.append(val)
    return x, [jnp.stack(v) for v in even_out], [jnp.stack(v) for v in odd_out]


def kernel(x_prompt, x_sample, state_shift, state_wkv, state_conv, state_c, state_n, state_m,
           cache_k, cache_v, cache_idx_k,
           ev_w_in, a_mu, a_w0, a_w2, a_a0, a_a2, a_g2, a_k_k, a_k_a, a_r_k, a_ln_g, a_ln_b,
           b_conv_w, b_conv_b, b_i_bias, b_f_bias, b_ln_g, b_ln_b, ev_w_out, ev_ln1_g, ev_ln1_b,
           ffn_w_gate, ffn_w_up, ffn_w_down, ev_ln2_g, ev_ln2_b,
           od_w_in, od_w_out, od_ln1_g, od_ln1_b, moe_w_router, moe_b_router,
           moe_w_gate, moe_w_up, moe_w_down, od_ln2_g, od_ln2_b):
    ew = {'w_in': ev_w_in, 'mu': a_mu, 'w0': a_w0, 'w2': a_w2, 'a0': a_a0, 'a2': a_a2, 'g2': a_g2,
          'k_k': a_k_k, 'k_a': a_k_a, 'r_k': a_r_k, 'a_ln_g': a_ln_g, 'a_ln_b': a_ln_b,
          'conv_w': b_conv_w, 'conv_b': b_conv_b, 'b_i': b_i_bias, 'b_f': b_f_bias,
          'b_ln_g': b_ln_g, 'b_ln_b': b_ln_b, 'w_out': ev_w_out, 'ln1_g': ev_ln1_g, 'ln1_b': ev_ln1_b,
          'ffn_gate': ffn_w_gate, 'ffn_up': ffn_w_up, 'ffn_down': ffn_w_down,
          'ln2_g': ev_ln2_g, 'ln2_b': ev_ln2_b}
    ow = {'w_in': od_w_in, 'w_out': od_w_out, 'ln1_g': od_ln1_g, 'ln1_b': od_ln1_b,
          'router': moe_w_router, 'router_b': moe_b_router, 'e_gate': moe_w_gate, 'e_up': moe_w_up,
          'e_down': moe_w_down, 'ln2_g': od_ln2_g, 'ln2_b': od_ln2_b}
    n_even, n_odd = ev_w_in.shape[0], od_w_in.shape[0]
    ev = [_prep_even({k: v[i] for k, v in ew.items()}) for i in range(n_even)]
    od = [_prep_odd({k: v[i] for k, v in ow.items()}) for i in range(n_odd)]

    bp = x_prompt.shape[0]
    z = functools.partial(jnp.zeros, dtype=F32)
    y_p, ep, op = _trunk(x_prompt,
                         z((n_even, bp, 1, A_COLS)), z((n_even, bp, A_HEADS, A_HEAD_DIM, A_HEAD_DIM)),
                         z((n_even, bp, 3, 2 * B_QK_WIDTH)), z((n_even, bp, B_HEADS, B_V_DIM, B_QK_DIM)),
                         z((n_even, bp, B_HEADS, B_QK_DIM)), z((n_even, bp, B_HEADS)),
                         None, None, None, ev, od)
    y_s, es, os_ = _trunk(x_sample, state_shift, state_wkv, state_conv, state_c, state_n, state_m,
                          cache_k, cache_v, cache_idx_k, ev, od)
    return (y_p, y_s,
            ep[0], es[0], ep[1], es[1], ep[2], es[2], ep[3], es[3], ep[4], es[4], ep[5], es[5],
            op[0], os_[0], op[1], os_[1], op[2], os_[2])
```

```python
import functools
import math

import jax
import jax.numpy as jnp
from jax import lax
from jax.experimental import pallas as pl
from jax.experimental.pallas import tpu as pltpu

F32 = jnp.float32
BF16 = jnp.bfloat16
I32 = jnp.int32

D_MODEL = 1024
DEPTH = 4
CHUNK = 64
A_HEADS, A_HEAD_DIM, A_WIDTH = 16, 64, 1024
A_LORA = 256
A_COLS = 3 * A_WIDTH + A_LORA
B_HEADS, B_QK_DIM, B_V_DIM = 8, 64, 128
B_QK_WIDTH, B_V_WIDTH = 512, 1024
C_HEADS, C_KV_HEADS, C_HEAD_DIM, C_GROUP = 8, 2, 128, 4
C_WIDTH = 1024
IDX_HEADS, IDX_DIM = 8, 64
TOPK_MAX = 256
N_EXPERTS = 8
D_FF_EXPERT = 1408
ALPHA = (2 * DEPTH) ** 0.25
LN_EPS = 1e-5

LANES = 128
SUBLANES = 8
VMEM_LIMIT = 56 * 1024 * 1024

EV_R, EV_K, EV_V, EV_QK, EV_VB, EV_O, EV_LORA, EV_GATE = 0, 1024, 2048, 3072, 4096, 5120, 6144, 6400
EV_COLS_PAD = 6528
OD_Q, OD_K, OD_V, OD_QI, OD_KI = 0, 1024, 1280, 1536, 2048
OD_COLS_PAD = 2176

INT_MIN = -2 ** 31


def _cparams(*sem):
    return pltpu.CompilerParams(dimension_semantics=sem, vmem_limit_bytes=VMEM_LIMIT)


def _layer_norm(z, g, b):
    mu = jnp.mean(z, axis=-1, keepdims=True)
    d = z - mu
    var = jnp.mean(d * d, axis=-1, keepdims=True)
    return d * lax.rsqrt(var + LN_EPS) * g + b


def _hilo_dot(x, e):
    hi = x.astype(BF16)
    lo = (x - hi.astype(F32)).astype(BF16)
    return jnp.dot(hi, e, preferred_element_type=F32) + jnp.dot(lo, e, preferred_element_type=F32)


def _segsum(x, e):
    return _hilo_dot(_hilo_dot(x, e[0]), e[1])


def _mm_kernel(x_ref, w_ref, o_ref):
    o_ref[...] = jnp.dot(x_ref[...].astype(BF16), w_ref[...], preferred_element_type=F32)


def _matmul(x, w, tm, tn):
    m, k = x.shape
    n = w.shape[1]
    return pl.pallas_call(
        _mm_kernel,
        grid=(n // tn, m // tm),
        in_specs=[pl.BlockSpec((tm, k), lambda j, i: (i, 0)),
                  pl.BlockSpec((k, tn), lambda j, i: (0, j))],
        out_specs=pl.BlockSpec((tm, tn), lambda j, i: (i, j)),
        out_shape=jax.ShapeDtypeStruct((m, n), F32),
        compiler_params=_cparams("parallel", "parallel"),
        name="proj_in",
    )(x, w)


def _rwkv_pre_kernel(ur, uk, uv, ul, s0m, s0l, mum, mul_, w0, a0, kk_, ka_, rk_, w2, a2, g2, er_ref, eb_ref,
                     o_seq, o_g, o_bonus, prev_m, prev_l):
    @pl.when(pl.program_id(1) == 0)
    def _():
        prev_m[...] = s0m[0]
        prev_l[...] = s0l[0]

    tm = ur.shape[1]
    row = lax.broadcasted_iota(I32, (tm, 1), 0)

    def lerp(x, prev, mu):
        shifted = jnp.where(row == 0, prev, pltpu.roll(x, 1, 0))
        return x + (shifted - x) * mu

    r_raw, k_raw, v_raw, l_raw = ur[0], uk[0], uv[0], ul[0]
    r = lerp(r_raw, prev_m[:, 0:1024], mum[:, 0:1024])
    k = lerp(k_raw, prev_m[:, 1024:2048], mum[:, 1024:2048])
    v = lerp(v_raw, prev_m[:, 2048:3072], mum[:, 2048:3072])
    lo = lerp(l_raw, prev_l[...], mul_[...])
    prev_m[:, 0:1024] = r_raw[tm - 1:tm, :]
    prev_m[:, 1024:2048] = k_raw[tm - 1:tm, :]
    prev_m[:, 2048:3072] = v_raw[tm - 1:tm, :]
    prev_l[...] = l_raw[tm - 1:tm, :]

    zw = w0[...] + jnp.dot(jnp.tanh(lo).astype(BF16), w2[...], preferred_element_type=F32)
    decay = jnp.exp(-(math.exp(-0.5) * jax.nn.sigmoid(zw)))
    a = jax.nn.sigmoid(a0[...] + jnp.dot(lo.astype(BF16), a2[...], preferred_element_type=F32))
    g = jnp.dot(jax.nn.sigmoid(lo).astype(BF16), g2[...], preferred_element_type=F32)

    e = (er_ref[...], eb_ref[...])
    kk = k * kk_[...]
    kk = kk * lax.rsqrt(jnp.maximum(_segsum(kk * kk, e), 1e-24))
    k2 = k * (1.0 + (a - 1.0) * ka_[...])
    bonus = _segsum(r * k2 * rk_[...], e) * v

    o_seq[WKV_SRC_R] = r
    o_seq[WKV_SRC_W] = decay
    o_seq[WKV_SRC_K] = k2
    o_seq[WKV_SRC_V] = v
    o_seq[WKV_SRC_A] = -kk
    o_seq[WKV_SRC_B] = kk * a
    o_g[...] = g
    o_bonus[...] = bonus


def _rwkv_pre(u, s0_main, s0_lora, p, tm):
    b, t, _ = u.shape
    blk = lambda w, c: pl.BlockSpec((1, tm, w), lambda i, j, c=c: (i, j, c))
    full = lambda arr: pl.BlockSpec(arr.shape, lambda i, j: (0,) * arr.ndim)
    params = [p['mu_main'], p['mu_lora'], p['w0'], p['a0'], p['k_k'], p['k_a'], p['r_k'],
              p['w2p'], p['a2p'], p['g2p'], p['seg_red'], p['seg_bc']]
    out = jax.ShapeDtypeStruct((t, b * A_WIDTH), F32)
    return pl.pallas_call(
        _rwkv_pre_kernel,
        grid=(b, t // tm),
        in_specs=[blk(1024, EV_R // 1024), blk(1024, EV_K // 1024), blk(1024, EV_V // 1024),
                  blk(A_LORA, EV_LORA // A_LORA),
                  pl.BlockSpec((1, 1, 3072), lambda i, j: (i, 0, 0)),
                  pl.BlockSpec((1, 1, A_LORA), lambda i, j: (i, 0, 0))] + [full(a) for a in params],
        out_specs=[pl.BlockSpec((WKV_N_SRC, tm, A_WIDTH), lambda i, j: (0, j, i)),
                   pl.BlockSpec((tm, A_WIDTH), lambda i, j: (j, i)),
                   pl.BlockSpec((tm, A_WIDTH), lambda i, j: (j, i))],
        out_shape=[jax.ShapeDtypeStruct((WKV_N_SRC, t, b * A_WIDTH), F32), out, out],
        scratch_shapes=[pltpu.VMEM((1, 3072), F32), pltpu.VMEM((1, A_LORA), F32)],
        compiler_params=_cparams("parallel", "arbitrary"),
        name="rwkv_pre",
    )(u, u, u, u, s0_main, s0_lora, *params)


WKV_N_SRC = 6
WKV_SRC_R, WKV_SRC_W, WKV_SRC_K, WKV_SRC_V, WKV_SRC_A, WKV_SRC_B = range(WKV_N_SRC)
WKV_VGROUP = 4
WKV_KCHUNK = 32


def _wkv_kernel(seq_ref, s0_ref, y_ref, s_ref, tr, sa_buf, ybuf):
    @pl.when(pl.program_id(1) == 0)
    def _():
        s_ref[...] = s0_ref[...]

    n_t = seq_ref.shape[1]
    d = A_HEAD_DIM
    vspan = WKV_VGROUP * SUBLANES
    groups = [(p, g) for p in range(2) for g in range(d // vspan)]
    n_kc = d // WKV_KCHUNK

    def vrows(g, j):
        return slice(g * vspan + j * SUBLANES, g * vspan + (j + 1) * SUBLANES)

    def transpose_tile(i, tok):
        tr[i, tok] = jnp.transpose(seq_ref[i, tok])

    for i in range(WKV_N_SRC):
        transpose_tile(i, 0)
    transpose_tile(WKV_SRC_A, min(1, n_t - 1))

    for p, g in groups:
        acc = [jnp.zeros((SUBLANES, LANES), F32) for _ in range(WKV_VGROUP)]
        for kx in range(d):
            ab = tr[WKV_SRC_A, 0, p * d + kx:p * d + kx + 1, :]
            for j in range(WKV_VGROUP):
                acc[j] = acc[j] + s_ref[p, kx, vrows(g, j), :] * ab
        for j in range(WKV_VGROUP):
            sa_buf[p, vrows(g, j), :] = acc[j]

    def step(t, carry):
        tn = jnp.minimum(t + 1, n_t - 1)
        for gi, (p, g) in enumerate(groups):
            vv = [tr[WKV_SRC_V, t, p * d + g * vspan + j * SUBLANES:p * d + g * vspan + (j + 1) * SUBLANES, :]
                  for j in range(WKV_VGROUP)]
            sa = [sa_buf[p, vrows(g, j), :] for j in range(WKV_VGROUP)]
            zero = tuple(jnp.zeros((SUBLANES, LANES), F32) for _ in range(2 * WKV_VGROUP))

            def key_chunk(kc, accs, gi=gi, p=p, g=g, vv=vv, sa=sa):
                yacc, san = list(accs[:WKV_VGROUP]), list(accs[WKV_VGROUP:])
                for i in range(WKV_KCHUNK):
                    kx = kc * WKV_KCHUNK + i
                    row = pl.ds(p * d + kx, 1)
                    wb = tr[WKV_SRC_W, t, row, :]
                    bb = tr[WKV_SRC_B, t, row, :]
                    kb = tr[WKV_SRC_K, t, row, :]
                    rb = tr[WKV_SRC_R, t, row, :]
                    an = tr[WKV_SRC_A, tn, row, :]
                    for j in range(WKV_VGROUP):
                        n = s_ref[p, kx, vrows(g, j), :] * wb + sa[j] * bb + vv[j] * kb
                        s_ref[p, kx, vrows(g, j), :] = n
                        yacc[j] = yacc[j] + n * rb
                        san[j] = san[j] + n * an
                tile = jnp.minimum(gi * n_kc + kc, WKV_N_SRC - 1)
                ahead = jnp.where(tile == WKV_SRC_A, 2, 1)
                transpose_tile(tile, jnp.minimum(t + ahead, n_t - 1))
                return tuple(yacc) + tuple(san)

            accs = lax.fori_loop(0, n_kc, key_chunk, zero)
            yacc, san = accs[:WKV_VGROUP], accs[WKV_VGROUP:]
            for j in range(WKV_VGROUP):
                ybuf[t, p * d + g * vspan + j * SUBLANES:p * d + g * vspan + (j + 1) * SUBLANES, :] = yacc[j]
                sa_buf[p, vrows(g, j), :] = san[j]
        return carry

    lax.fori_loop(0, n_t, step, 0)

    def transpose_out(t, carry):
        y_ref[t] = jnp.transpose(ybuf[t])
        return carry

    lax.fori_loop(0, n_t, transpose_out, 0, unroll=4)


def _wkv_scan(seq, s0, tt):
    _, t, rows, _ = seq.shape
    d = A_HEAD_DIM
    st = pl.BlockSpec((2, d, d, LANES), lambda g, i: (0, 0, 0, g))
    return pl.pallas_call(
        _wkv_kernel,
        grid=(rows // LANES, t // tt),
        in_specs=[pl.BlockSpec((WKV_N_SRC, tt, LANES, LANES), lambda g, i: (0, i, g, 0)), st],
        out_specs=[pl.BlockSpec((tt, LANES, LANES), lambda g, i: (i, g, 0)), st],
        out_shape=[jax.ShapeDtypeStruct((t, rows, LANES), F32), jax.ShapeDtypeStruct((2, d, d, rows), F32)],
        scratch_shapes=[pltpu.VMEM((WKV_N_SRC, tt, LANES, LANES), F32), pltpu.VMEM((2, d, LANES), F32),
                        pltpu.VMEM((tt, LANES, LANES), F32)],
        compiler_params=_cparams("parallel", "arbitrary"),
        name="wkv_scan",
    )(seq, s0)


def _mlstm_kernel(uqk, uv, uo, ug, conv0, ct0, m0, cw, cb, gb, lng, lnb,
                  o_y, o_ct, o_m, ct, m_scr, carry, *, t_valid):
    tb = pl.program_id(1)

    @pl.when(tb == 0)
    def _():
        ct[...] = ct0[...]
        m_scr[...] = m0[...]
        carry[...] = conv0[...]

    for bi in range(uqk.shape[0]):
        one = pl.ds(bi, 1)
        _mlstm_chunk(tb, uqk.at[one], uv.at[one], uo.at[one], ug.at[one], cw, cb, gb, lng, lnb,
                     o_y.at[one], ct.at[bi], m_scr.at[bi], carry.at[bi], t_valid=t_valid)

    @pl.when(tb == pl.num_programs(1) - 1)
    def _():
        o_ct[...] = ct[...]
        o_m[...] = m_scr[...]


def _mlstm_chunk(tb, uqk, uv, uo, ug, cw, cb, gb, lng, lnb, o_y, ct, m_scr, carry, *, t_valid):
    L = uqk.shape[1]
    row = lax.broadcasted_iota(I32, (L, 1), 0)
    x = uqk[0]

    def shifted(j):
        out = pltpu.roll(x, j, 0)
        for i in range(j):
            out = jnp.where(row == i, carry[3 + i - j:4 + i - j, :], out)
        return out

    conv = x * cw[3:4, :] + shifted(1) * cw[2:3, :] + shifted(2) * cw[1:2, :] + shifted(3) * cw[0:1, :] + cb[...]
    carry[...] = x[L - 3:L, :]
    qk = conv * jax.nn.sigmoid(conv)
    q_all = qk[:, 0:B_QK_WIDTH]
    k_t = jnp.transpose(qk[:, B_QK_WIDTH:2 * B_QK_WIDTH] * (B_QK_DIM ** -0.5))

    lane = lax.broadcasted_iota(I32, (L, LANES), 1)
    gz = ug[0] + gb[...]
    lg = jnp.where(lane < B_HEADS, gz, jnp.minimum(gz, 0.0) - jnp.log(1.0 + jnp.exp(-jnp.abs(gz))))
    valid = (row + tb * L) < t_valid
    lg = jnp.where(valid, lg, jnp.where(lane < B_HEADS, -jnp.inf, 0.0))
    lg_t = jnp.transpose(lg)
    ti = lax.broadcasted_iota(I32, (L, L), 0)
    si = lax.broadcasted_iota(I32, (L, L), 1)
    tril = ti >= si
    tril_f = tril.astype(F32)
    triu_f = (ti <= si).astype(F32)
    lg_fin = jnp.where(lane < B_HEADS, 0.0, lg)
    bcum = jnp.dot(tril_f, lg_fin, preferred_element_type=F32, precision=lax.Precision.HIGHEST)
    bcum_t = jnp.dot(jnp.transpose(lg_fin), triu_f, preferred_element_type=F32,
                     precision=lax.Precision.HIGHEST)

    head_q = lax.shift_right_logical(lax.broadcasted_iota(I32, (L, B_QK_WIDTH), 1), 6)
    lane1 = lax.broadcasted_iota(I32, (1, LANES), 1)
    ones_col = (lax.broadcasted_iota(I32, (L, LANES), 1) == 0).astype(F32)
    m_vec = m_scr[...]
    m_new_vec = m_vec
    lane_ct = lax.broadcasted_iota(I32, (B_QK_DIM, 2 * LANES), 1)
    ct_old = ct[...]
    ct_old_bf = ct_old.astype(BF16)
    k_bf = k_t.astype(BF16)
    vv = uv[0]
    oo = uo[0]
    q_stack = jnp.concatenate([jnp.where(head_q == h, q_all, 0.0) for h in range(B_HEADS)], axis=0).astype(BF16)
    qk_all = jnp.dot(q_stack, k_bf, preferred_element_type=F32)
    cross_all = jnp.dot(q_stack, ct_old_bf, preferred_element_type=F32)
    for h in range(B_HEADS):
        hs = slice(h * B_QK_DIM, (h + 1) * B_QK_DIM)
        vs = slice(h * B_V_DIM, (h + 1) * B_V_DIM)
        bc_col = bcum[:, B_HEADS + h:B_HEADS + h + 1]
        bc_row = bcum_t[B_HEADS + h:B_HEADS + h + 1, :]
        ic_row = lg_t[h:h + 1, :]
        gtot = bc_row[:, L - 1:L]
        m_prev = m_vec[:, h:h + 1]
        dmat = jnp.where(tril, bc_col - bc_row + ic_row, -jnp.inf)
        inter = bc_col + m_prev
        m_t = jnp.maximum(inter, jnp.max(dmat, axis=1, keepdims=True))
        w_intra = jnp.exp(dmat - m_t)
        w_inter = jnp.exp(inter - m_t)
        s = qk_all[h * L:(h + 1) * L, :] * w_intra
        v_ext = jnp.concatenate([vv[:, vs], ones_col], axis=1).astype(BF16)
        intra = jnp.dot(s.astype(BF16), v_ext, preferred_element_type=F32)
        cross = cross_all[h * L:(h + 1) * L, :]
        num = intra[:, 0:B_V_DIM] + w_inter * cross[:, 0:B_V_DIM]
        den = jnp.sum(s, axis=1, keepdims=True) + w_inter * cross[:, B_V_DIM:B_V_DIM + 1]
        hh = num / jnp.maximum(jnp.abs(den), jnp.exp(-m_t))
        mu = jnp.mean(hh, axis=-1, keepdims=True)
        dd = hh - mu
        var = jnp.mean(dd * dd, axis=-1, keepdims=True)
        yn = dd * lax.rsqrt(var + LN_EPS) * lng[:, vs] + lnb[:, vs]
        o_y[0, :, vs] = jax.nn.sigmoid(oo[:, vs]) * yn
        lw = gtot - bc_row + ic_row
        m_new = jnp.maximum(gtot + m_prev, jnp.max(lw, axis=1, keepdims=True))
        w_s = jnp.exp(lw - m_new)
        dec = jnp.exp(gtot + m_prev - m_new)
        kw = k_t[hs, :] * w_s
        upd = jnp.dot(kw.astype(BF16), v_ext, preferred_element_type=F32)
        upd = jnp.where(lane_ct == B_V_DIM, jnp.sum(kw, axis=1, keepdims=True), upd)
        ct[hs, :] = dec * ct_old[hs, :] + upd
        m_new_vec = jnp.where(lane1 == h, m_new, m_new_vec)
    m_scr[...] = m_new_vec


MLSTM_ROWS = 1


def _mlstm(u, conv0, ct0, m0, p, L, t_valid):
    b, t, _ = u.shape
    nb = MLSTM_ROWS if b % MLSTM_ROWS == 0 else 1
    blk = lambda w, c: pl.BlockSpec((nb, L, w), lambda i, j, c=c: (i, j, c))
    full = lambda arr: pl.BlockSpec(arr.shape, lambda i, j: (0,) * arr.ndim)
    per_b = lambda arr: pl.BlockSpec((nb,) + arr.shape[1:], lambda i, j: (i, 0, 0))
    params = [p['conv_w'], p['conv_b'], p['gate_b'], p['b_ln_g'], p['b_ln_b']]
    return pl.pallas_call(
        functools.partial(_mlstm_kernel, t_valid=t_valid),
        grid=(b // nb, t // L),
        in_specs=[blk(1024, EV_QK // 1024), blk(1024, EV_VB // 1024), blk(1024, EV_O // 1024),
                  blk(LANES, EV_GATE // LANES), per_b(conv0), per_b(ct0), per_b(m0)] + [full(a) for a in params],
        out_specs=[pl.BlockSpec((nb, L, B_V_WIDTH), lambda i, j: (i, j, 0)),
                   pl.BlockSpec((nb, B_QK_WIDTH, 2 * LANES), lambda i, j: (i, 0, 0)),
                   pl.BlockSpec((nb, 1, LANES), lambda i, j: (i, 0, 0))],
        out_shape=[jax.ShapeDtypeStruct((b, t, B_V_WIDTH), F32),
                   jax.ShapeDtypeStruct((b, B_QK_WIDTH, 2 * LANES), F32),
                   jax.ShapeDtypeStruct((b, 1, LANES), F32)],
        scratch_shapes=[pltpu.VMEM((nb, B_QK_WIDTH, 2 * LANES), F32), pltpu.VMEM((nb, 1, LANES), F32),
                        pltpu.VMEM((nb, 3, 2 * B_QK_WIDTH), F32)],
        compiler_params=_cparams("parallel", "arbitrary"),
        name="mlstm",
    )(u, u, u, u, conv0, ct0, m0, *params)


def _even_out_kernel(y_ref, bonus_ref, g_ref, yb_ref, x_ref, lng, lnb, er_ref, eb_ref, woa, wob, g1, b1, o_ref):
    e = (er_ref[...], eb_ref[...])
    y = y_ref[...]
    mu = _segsum(y, e) * (1.0 / A_HEAD_DIM)
    d = y - mu
    var = _segsum(d * d, e) * (1.0 / A_HEAD_DIM)
    yn = d * lax.rsqrt(var + LN_EPS) * lng[...] + lnb[...]
    ya = (yn + bonus_ref[...]) * g_ref[...]
    mix = (jnp.dot(ya.astype(BF16), woa[...], preferred_element_type=F32)
           + jnp.dot(yb_ref[...].astype(BF16), wob[...], preferred_element_type=F32))
    o_ref[...] = _layer_norm(ALPHA * x_ref[...] + mix, g1[...], b1[...])


def _even_out(y, bonus, g, yb, x, p, b, tm):
    m = x.shape[0]
    nt = m // b // tm
    tmaj = pl.BlockSpec((tm, D_MODEL), lambda i, j: (j, i))
    row = pl.BlockSpec((tm, D_MODEL), lambda i, j: (i * nt + j, 0))
    full = lambda arr: pl.BlockSpec(arr.shape, lambda i, j: (0,) * arr.ndim)
    params = [p['a_ln_g'], p['a_ln_b'], p['seg_red'], p['seg_bc'], p['w_out_a'], p['w_out_b'], p['ln1_g'], p['ln1_b']]
    return pl.pallas_call(
        _even_out_kernel,
        grid=(b, nt),
        in_specs=[tmaj] * 3 + [row] * 2 + [full(a) for a in params],
        out_specs=row,
        out_shape=jax.ShapeDtypeStruct((m, D_MODEL), F32),
        compiler_params=_cparams("parallel", "parallel"),
        name="even_out",
    )(y, bonus, g, yb, x, *params)


def _proj_ln_kernel(a_ref, x_ref, w_ref, g1, b1, o_ref):
    mix = jnp.dot(a_ref[...].astype(BF16), w_ref[...], preferred_element_type=F32)
    o_ref[...] = _layer_norm(ALPHA * x_ref[...] + mix, g1[...], b1[...])


def _proj_ln(a, x, w, g1, b1, tm):
    m = x.shape[0]
    row = pl.BlockSpec((tm, D_MODEL), lambda i: (i, 0))
    full = lambda arr: pl.BlockSpec(arr.shape, lambda i: (0,) * arr.ndim)
    return pl.pallas_call(
        _proj_ln_kernel,
        grid=(m // tm,),
        in_specs=[row, row, full(w), full(g1), full(b1)],
        out_specs=row,
        out_shape=jax.ShapeDtypeStruct((m, D_MODEL), F32),
        compiler_params=_cparams("parallel"),
        name="proj_ln",
    )(a, x, w, g1, b1)


def _swiglu(xb, wg, wu, wd):
    hg = jnp.dot(xb, wg, preferred_element_type=F32)
    hu = jnp.dot(xb, wu, preferred_element_type=F32)
    hid = (hg * jax.nn.sigmoid(hg)) * hu
    return jnp.dot(hid.astype(BF16), wd, preferred_element_type=F32)


def _ffn_kernel(x_ref, wg, wu, wd, g2, b2, o_ref, xb):
    e = pl.program_id(1)

    @pl.when(e == 0)
    def _():
        xb[...] = x_ref[...].astype(BF16)
        o_ref[...] = jnp.zeros_like(o_ref)

    o_ref[...] += _swiglu(xb[...], wg[0], wu[0], wd[0])

    @pl.when(e == pl.num_programs(1) - 1)
    def _():
        o_ref[...] = _layer_norm(ALPHA * x_ref[...] + o_ref[...], g2[...], b2[...])


MOE_CHUNK = 256


def _moe_kernel(x_ref, wr, br, wg, wu, wd, g2, b2, o_ref, xb, comb, pos_scr, post_scr):
    e = pl.program_id(1)
    tm = x_ref.shape[0]

    @pl.when(e == 0)
    def _():
        x = x_ref[...]
        xb[...] = x.astype(BF16)
        o_ref[...] = jnp.zeros_like(o_ref)
        logits = jnp.dot(x, wr[...], preferred_element_type=F32, precision=lax.Precision.HIGHEST) + br[...]
        lane = lax.broadcasted_iota(I32, logits.shape, 1).astype(F32)
        v1 = jnp.max(logits, axis=-1, keepdims=True)
        i1 = jnp.min(jnp.where(logits == v1, lane, float(LANES)), axis=-1, keepdims=True)
        rest = jnp.where(lane == i1, -jnp.inf, logits)
        v2 = jnp.max(rest, axis=-1, keepdims=True)
        i2 = jnp.min(jnp.where(rest == v2, lane, float(LANES)), axis=-1, keepdims=True)
        e2 = jnp.exp(v2 - v1)
        den = 1.0 + e2
        comb[...] = jnp.where(lane == i1, 1.0 / den, jnp.where(lane == i2, e2 / den, 0.0))
        member = jnp.where(lane == i1, 1.0, jnp.where(lane == i2, 1.0, 0.0))
        earlier = (lax.broadcasted_iota(I32, (tm, tm), 0) > lax.broadcasted_iota(I32, (tm, tm), 1))
        pos = jnp.dot(jnp.where(earlier, 1.0, 0.0).astype(BF16), member.astype(BF16), preferred_element_type=F32)
        posm = jnp.where(member > 0.0, pos, -1.0)
        pos_scr[...] = posm
        post_scr[...] = jnp.transpose(posm)

    sel = lax.broadcasted_iota(I32, (tm, LANES), 1) == e
    gate = jnp.sum(jnp.where(sel, comb[...], 0.0), axis=-1, keepdims=True)
    slot_col = jnp.sum(jnp.where(sel, pos_scr[...], 0.0), axis=-1, keepdims=True)
    slot_row = post_scr[pl.ds(e, 1), :]
    count = jnp.max(slot_row) + 1.0

    def run_slots(base, n):
        rows = lax.broadcasted_iota(I32, (n, 1), 0).astype(F32) + float(base)
        gather = jnp.where(slot_row == rows, 1.0, 0.0).astype(BF16)
        xg = jnp.dot(gather, xb[...], preferred_element_type=F32).astype(BF16)
        y = _swiglu(xg, wg[0], wu[0], wd[0])
        cols = lax.broadcasted_iota(I32, (1, n), 1).astype(F32) + float(base)
        scatter = jnp.where(slot_col == cols, 1.0, 0.0).astype(BF16)
        o_ref[...] += gate * jnp.dot(scatter, y.astype(BF16), preferred_element_type=F32)

    chunk = min(MOE_CHUNK, tm)
    half = chunk // 2
    for c in range(tm // chunk):
        lo = c * chunk

        @pl.when(count > lo + half)
        def _(lo=lo):
            run_slots(lo, chunk)

        @pl.when(jnp.logical_and(count > lo, count <= lo + half))
        def _(lo=lo):
            run_slots(lo, half)

    @pl.when(e == pl.num_programs(1) - 1)
    def _():
        o_ref[...] = _layer_norm(ALPHA * x_ref[...] + o_ref[...], g2[...], b2[...])


def _experts(x, wr, br, wg, wu, wd, g2, b2, tm, routed):
    m = x.shape[0]
    ne, _, dff = wg.shape
    row = pl.BlockSpec((tm, D_MODEL), lambda i, e: (i, 0))
    full = lambda arr: pl.BlockSpec(arr.shape, lambda i, e: (0,) * arr.ndim)
    wspecs = [pl.BlockSpec((1, D_MODEL, dff), lambda i, e: (e, 0, 0)),
              pl.BlockSpec((1, D_MODEL, dff), lambda i, e: (e, 0, 0)),
              pl.BlockSpec((1, dff, D_MODEL), lambda i, e: (e, 0, 0))]
    common = dict(grid=(m // tm, ne), out_specs=row, out_shape=jax.ShapeDtypeStruct((m, D_MODEL), F32),
                  compiler_params=_cparams("parallel", "arbitrary"))
    if not routed:
        return pl.pallas_call(
            _ffn_kernel, in_specs=[row] + wspecs + [full(g2), full(b2)],
            scratch_shapes=[pltpu.VMEM((tm, D_MODEL), BF16)], name="experts_dense", **common,
        )(x, wg, wu, wd, g2, b2)
    return pl.pallas_call(
        _moe_kernel, in_specs=[row, full(wr), full(br)] + wspecs + [full(g2), full(b2)],
        scratch_shapes=[pltpu.VMEM((tm, D_MODEL), BF16), pltpu.VMEM((tm, LANES), F32),
                        pltpu.VMEM((tm, LANES), F32), pltpu.VMEM((LANES, tm), F32)],
        name="experts_routed", **common,
    )(x, wr, br, wg, wu, wd, g2, b2)


DSA_COARSE_BITS = 28


def _dsa_kernel(q_ref, qi_ref, wi_ref, kt_ref, v_ref, kit_ref, *rest, top_k, causal, limit_const, blk_off, has_prev):
    o_ref, key_scr, jcut_scr, thr_scr = rest[1:] if has_prev else rest
    tq = q_ref.shape[1]
    s_len = kt_ref.shape[2]
    if causal:
        row_chunk = lax.shift_right_logical(lax.broadcasted_iota(I32, (tq, 1), 0), CHUNK.bit_length() - 1)
        chunk = (pl.program_id(1) + blk_off) * (tq // CHUNK) + row_chunk
        limit = (chunk + 1) * CHUNK
        kth = jnp.minimum(top_k, limit).astype(F32)
    else:
        limit = limit_const
        kth = jnp.full((tq, 1), min(top_k, limit_const), F32)
    idx = lax.broadcasted_iota(I32, (tq, s_len), 1)

    kit = kit_ref[0]
    wi = wi_ref[0] * (IDX_HEADS ** -0.5)
    score = jnp.zeros((tq, s_len), F32)
    for h in range(IDX_HEADS):
        rel = jnp.dot(qi_ref[0, h], kit, preferred_element_type=F32)
        score = score + wi[:, IDX_DIM + h:IDX_DIM + h + 1] * jnp.maximum(rel, 0.0)
    score = jnp.where(score == 0.0, 0.0, score)
    bits = pltpu.bitcast(score, I32)
    key = jnp.where(bits < 0, bits ^ 0x7FFFFFFF, bits)
    key_scr[...] = jnp.where(idx < limit, key, INT_MIN)

    n_grp = 2 if tq % (4 * SUBLANES) == 0 else 1
    rg = tq // n_grp
    rows_of = lambda g: slice(g * rg, (g + 1) * rg)

    def count_ge(thrs):
        return [jnp.sum(jnp.where(key_scr[rows_of(g), :] >= thrs[g], 1.0, 0.0), axis=1, keepdims=True)
                for g in range(n_grp)]

    def bit_step(i, thrs):
        bit = lax.shift_left(jnp.int32(1), 31 - i)
        cands = [t + bit for t in thrs]
        cnts = count_ge(cands)
        return tuple(jnp.where(cnts[g] >= kth[rows_of(g)], cands[g], thrs[g]) for g in range(n_grp))

    thrs = lax.fori_loop(0, DSA_COARSE_BITS, bit_step,
                         tuple(jnp.full((rg, 1), INT_MIN, I32) for _ in range(n_grp)), unroll=4)
    thr_scr[...] = jnp.concatenate(thrs, axis=0)
    n_coarse = jnp.concatenate(count_ge(thrs), axis=0)

    @pl.when(jnp.max(n_coarse - kth) > 0.0)
    def _():
        fine = lax.fori_loop(DSA_COARSE_BITS, 32, bit_step, tuple(thr_scr[rows_of(g), :] for g in range(n_grp)),
                             unroll=4)
        thr_scr[...] = jnp.concatenate(fine, axis=0)

    thr = thr_scr[...]
    keyv = key_scr[...]
    n_ge = jnp.sum(jnp.where(keyv >= thr, 1.0, 0.0), axis=1, keepdims=True)
    jcut_scr[...] = jnp.full((tq, 1), s_len, I32)

    @pl.when(jnp.max(n_ge - kth) > 0.0)
    def _():
        need = kth - jnp.sum(jnp.where(key_scr[...] > thr, 1.0, 0.0), axis=1, keepdims=True)
        n_bits = max(1, (s_len - 1).bit_length())

        def idx_step(i, j):
            cand = j + lax.shift_left(jnp.int32(1), n_bits - 1 - i)
            cnt = jnp.sum(jnp.where(key_scr[...] == thr, jnp.where(idx < cand, 1.0, 0.0), 0.0),
                          axis=1, keepdims=True)
            return jnp.where(cnt < need, cand, j)

        jcut_scr[...] = lax.fori_loop(0, n_bits, idx_step, jnp.zeros((tq, 1), I32))

    jcut = jcut_scr[...]
    bias = jnp.where(keyv > thr, 0.0,
                     jnp.where(keyv == thr, jnp.where(idx <= jcut, 0.0, -jnp.inf), -jnp.inf)).astype(BF16)

    q = q_ref[0] * (C_HEAD_DIM ** -0.5)
    for grp in range(C_KV_HEADS):
        kt = kt_ref[0, grp * C_HEAD_DIM:(grp + 1) * C_HEAD_DIM, :]
        v_ext = v_ref[0, :, grp * 2 * C_HEAD_DIM:(grp + 1) * 2 * C_HEAD_DIM]
        hsl = [slice((grp * C_GROUP + hg) * C_HEAD_DIM, (grp * C_GROUP + hg + 1) * C_HEAD_DIM)
               for hg in range(C_GROUP)]
        qg = jnp.concatenate([q[:, sl] for sl in hsl], axis=0).astype(BF16)
        lg = jnp.dot(qg, kt, preferred_element_type=F32).astype(BF16)
        ps = []
        for hg in range(C_GROUP):
            logits = lg[hg * tq:(hg + 1) * tq, :] + bias
            m_tile = functools.reduce(jnp.maximum, [logits[:, i * LANES:(i + 1) * LANES]
                                                    for i in range(s_len // LANES)])
            mx = jnp.max(m_tile.astype(F32), axis=1, keepdims=True).astype(BF16)
            ps.append(jnp.exp(logits - mx))
        out = jnp.dot(jnp.concatenate(ps, axis=0), v_ext, preferred_element_type=F32)
        for hg in range(C_GROUP):
            rows = slice(hg * tq, (hg + 1) * tq)
            o_ref[0, :, hsl[hg]] = out[rows, 0:C_HEAD_DIM] / out[rows, C_HEAD_DIM:C_HEAD_DIM + 1]


DSA_KEY_STEP = 512
DSA_WIDE_Q_KEYS = 3584


def _dsa_call(u, qi_heads, kt, v, kit, prev, tq, top_k, causal, limit_const, blk_off, n_blk, s_len):
    b, t, _ = u.shape
    in_specs = [pl.BlockSpec((1, tq, C_WIDTH), lambda i, j: (i, j + blk_off, OD_Q // C_WIDTH)),
                pl.BlockSpec((1, IDX_HEADS, tq, LANES), lambda i, j: (i, 0, j + blk_off, 0)),
                pl.BlockSpec((1, tq, LANES), lambda i, j: (i, j + blk_off, OD_KI // LANES)),
                pl.BlockSpec((1, 2 * C_HEAD_DIM, s_len), lambda i, j: (i, 0, 0)),
                pl.BlockSpec((1, s_len, 2 * C_KV_HEADS * C_HEAD_DIM), lambda i, j: (i, 0, 0)),
                pl.BlockSpec((1, LANES, s_len), lambda i, j: (i, 0, 0))]
    args = [u, qi_heads, u, kt, v, kit]
    aliases = {}
    if prev is not None:
        in_specs.append(pl.BlockSpec(memory_space=pl.ANY))
        args.append(prev)
        aliases = {len(args) - 1: 0}
    return pl.pallas_call(
        functools.partial(_dsa_kernel, top_k=top_k, causal=causal, limit_const=limit_const, blk_off=blk_off,
                          has_prev=prev is not None),
        grid=(b, n_blk),
        in_specs=in_specs,
        out_specs=pl.BlockSpec((1, tq, C_WIDTH), lambda i, j: (i, j + blk_off, 0)),
        out_shape=jax.ShapeDtypeStruct((b, t, C_WIDTH), F32),
        scratch_shapes=[pltpu.VMEM((tq, s_len), I32), pltpu.VMEM((tq, 1), I32), pltpu.VMEM((tq, 1), I32)],
        input_output_aliases=aliases,
        compiler_params=_cparams("parallel", "arbitrary"),
        name="dsa",
    )(*args)


def _dsa(u, qi_heads, kt, v, kit, tq, top_k, causal, limit_const):
    b, t, _ = u.shape
    s_full = kt.shape[2]
    if not causal or s_full % DSA_KEY_STEP:
        return _dsa_call(u, qi_heads, kt, v, kit, None, tq, top_k, causal, limit_const, 0, t // tq, s_full)
    att = None
    for cls in range(s_full // DSA_KEY_STEP):
        s_len = (cls + 1) * DSA_KEY_STEP
        tq_c = 2 * tq if s_len <= DSA_WIDE_Q_KEYS and DSA_KEY_STEP % (2 * tq) == 0 else tq
        per = DSA_KEY_STEP // tq_c
        att = _dsa_call(u, qi_heads, kt, v, kit, att, tq_c, top_k, causal, limit_const, cls * per, per, s_len)
    return att


def _prep_even(w):
    win = w['w_in']
    a, bq = win[:, :A_COLS], win[:, A_COLS:]
    cols = [a[:, 0:3072], bq[:, 0:3072], a[:, 3072:A_COLS], bq[:, 3072:3088],
            jnp.zeros((D_MODEL, EV_COLS_PAD - EV_GATE - 2 * B_HEADS), F32)]
    p = {'w_in': jnp.concatenate(cols, axis=1).astype(BF16)}
    mu = w['mu']
    p['mu_main'] = mu[None, 0:3072]
    p['mu_lora'] = mu[None, 3072:A_COLS]
    row = lambda v: v.reshape(1, -1)
    p['w0'], p['a0'], p['k_k'], p['k_a'] = row(w['w0']), row(w['a0']), row(w['k_k']), row(w['k_a'])
    p['r_k'] = row(w['r_k'])
    z = lambda n: jnp.zeros((n, A_WIDTH), F32)
    p['w2p'] = jnp.concatenate([w['w2'], z(192)], axis=0).astype(BF16)
    p['a2p'] = jnp.concatenate([z(64), w['a2'], z(128)], axis=0).astype(BF16)
    p['g2p'] = jnp.concatenate([z(128), w['g2']], axis=0).astype(BF16)
    seg = jnp.arange(A_WIDTH) // A_HEAD_DIM
    p['seg_red'] = (seg[:, None] == jnp.arange(LANES)[None, :]).astype(BF16)
    p['seg_bc'] = p['seg_red'].T
    p['a_ln_g'], p['a_ln_b'] = row(w['a_ln_g']), row(w['a_ln_b'])
    p['conv_w'], p['conv_b'] = w['conv_w'], row(w['conv_b'])
    p['gate_b'] = jnp.concatenate([w['b_i'], w['b_f'], jnp.zeros((LANES - 2 * B_HEADS,), F32)])[None]
    p['b_ln_g'], p['b_ln_b'] = row(w['b_ln_g']), row(w['b_ln_b'])
    p['w_out_a'] = w['w_out'][:A_WIDTH].astype(BF16)
    p['w_out_b'] = w['w_out'][A_WIDTH:].astype(BF16)
    p['ln1_g'], p['ln1_b'], p['ln2_g'], p['ln2_b'] = row(w['ln1_g']), row(w['ln1_b']), row(w['ln2_g']), row(w['ln2_b'])
    half = w['ffn_gate'].shape[1] // 2
    p['ffn_g'] = jnp.stack([w['ffn_gate'][:, :half], w['ffn_gate'][:, half:]]).astype(BF16)
    p['ffn_u'] = jnp.stack([w['ffn_up'][:, :half], w['ffn_up'][:, half:]]).astype(BF16)
    p['ffn_d'] = jnp.stack([w['ffn_down'][:half], w['ffn_down'][half:]]).astype(BF16)
    return p


def _prep_odd(w):
    row = lambda v: v.reshape(1, -1)
    p = {'w_in': jnp.pad(w['w_in'], ((0, 0), (0, OD_COLS_PAD - w['w_in'].shape[1]))).astype(BF16)}
    p['w_out'] = w['w_out'].astype(BF16)
    p['ln1_g'], p['ln1_b'], p['ln2_g'], p['ln2_b'] = row(w['ln1_g']), row(w['ln1_b']), row(w['ln2_g']), row(w['ln2_b'])
    p['router_w'] = jnp.pad(w['router'], ((0, 0), (0, LANES - N_EXPERTS)))
    p['router_b'] = jnp.concatenate([w['router_b'], jnp.full((LANES - N_EXPERTS,), -1e30, F32)])[None]
    p['e_gate'], p['e_up'], p['e_down'] = w['e_gate'].astype(BF16), w['e_up'].astype(BF16), w['e_down'].astype(BF16)
    return p


def _pick(n, pref):
    for c in pref:
        if n % c == 0:
            return c
    return n


def _even_layer(x, shift0, wkv0, conv0, c0, n0, m0, p):
    b, t, _ = x.shape
    m = b * t
    tm = _pick(m, (512, 256, 128, 64, 32, 16, 8))
    u = _matmul(x.reshape(m, D_MODEL), p['w_in'], _pick(m, (256, 128, 64, 32, 16, 8)), EV_COLS_PAD)
    u = u.reshape(b, t, EV_COLS_PAD)

    new_shift = jnp.concatenate([u[:, t - 1:, 0:3072], u[:, t - 1:, EV_LORA:EV_LORA + A_LORA]], axis=-1)
    new_conv = jnp.concatenate([conv0, u[:, :, EV_QK:EV_QK + 1024]], axis=1)[:, t:]

    s0_main, s0_lora = shift0[:, :, 0:3072], shift0[:, :, 3072:A_COLS]
    tp = _pick(t, (256, 128, 64, 32, 16, 8))
    seq, g, bonus = _rwkv_pre(u, s0_main, s0_lora, p, tp)
    rows = b * A_HEADS // 2
    assert rows % LANES == 0, "batch must be a multiple of 16"
    s0 = wkv0.reshape(b, A_HEADS // 2, 2, A_HEAD_DIM, A_HEAD_DIM).transpose(2, 4, 3, 0, 1)
    s0 = s0.reshape(2, A_HEAD_DIM, A_HEAD_DIM, rows)
    tt = _pick(t, (16, 8))
    y, s_fin = _wkv_scan(seq.reshape(WKV_N_SRC, t, rows, LANES), s0, tt)
    y = y.reshape(t, b * A_WIDTH)
    new_wkv = s_fin.reshape(2, A_HEAD_DIM, A_HEAD_DIM, b, A_HEADS // 2).transpose(3, 4, 0, 2, 1)
    new_wkv = new_wkv.reshape(b, A_HEADS, A_HEAD_DIM, A_HEAD_DIM)

    ct0 = jnp.concatenate([c0.transpose(0, 1, 3, 2).reshape(b, B_QK_WIDTH, B_V_DIM),
                           n0.reshape(b, B_QK_WIDTH, 1),
                           jnp.zeros((b, B_QK_WIDTH, LANES - 1), F32)], axis=-1)
    m0p = jnp.pad(m0, ((0, 0), (0, LANES - B_HEADS)))[:, None, :]
    lm = _pick(t, (256, 128))
    if t % LANES:
        t_pad = -t % LANES
        u_b = jnp.pad(u, ((0, 0), (0, t_pad), (0, 0)))
        lm = LANES
    else:
        u_b = u
    yb, ct, m_out = _mlstm(u_b, conv0, ct0, m0p, p, lm, t)
    yb = yb[:, :t].reshape(m, B_V_WIDTH)
    new_c = ct[:, :, 0:B_V_DIM].reshape(b, B_HEADS, B_QK_DIM, B_V_DIM).transpose(0, 1, 3, 2)
    new_n = ct[:, :, B_V_DIM].reshape(b, B_HEADS, B_QK_DIM)
    new_m = m_out[:, 0, 0:B_HEADS]

    x2 = x.reshape(m, D_MODEL)
    x2 = _even_out(y, bonus, g, yb, x2, p, b, _pick(t, (256, 128, 64, 32, 16, 8)))
    x2 = _experts(x2, None, None, p['ffn_g'], p['ffn_u'], p['ffn_d'], p['ln2_g'], p['ln2_b'], tm, routed=False)
    return x2.reshape(b, t, D_MODEL), (new_shift, new_wkv, new_conv, new_c, new_n, new_m)


def _odd_layer(x, past_k, past_v, past_ki, p):
    b, t, _ = x.shape
    m = b * t
    tm = _pick(m, (512, 256, 128, 64, 32, 16, 8))
    u = _matmul(x.reshape(m, D_MODEL), p['w_in'], tm, OD_COLS_PAD).reshape(b, t, OD_COLS_PAD)
    k_new = u[:, :, OD_K:OD_K + 256]
    v_new = u[:, :, OD_V:OD_V + 256]
    ki_new = u[:, :, OD_KI:OD_KI + IDX_DIM]
    if past_k is None:
        keys_k, keys_v, keys_i = k_new, v_new, ki_new
        causal, limit, tq = True, 0, _pick(t, (2 * CHUNK, CHUNK))
        top_k = min(TOPK_MAX, t // 4)
    else:
        keys_k = jnp.concatenate([past_k.reshape(b, -1, 256), k_new], axis=1)
        keys_v = jnp.concatenate([past_v.reshape(b, -1, 256), v_new], axis=1)
        keys_i = jnp.concatenate([past_ki, ki_new], axis=1)
        limit = keys_k.shape[1]
        causal, tq = False, t
        top_k = min(TOPK_MAX, limit // 4)
    s_pad = -keys_k.shape[1] % LANES
    pad_s = lambda z: jnp.pad(z, ((0, 0), (0, s_pad), (0, 0))) if s_pad else z
    kt = pad_s(keys_k).transpose(0, 2, 1).astype(BF16)
    vv = pad_s(keys_v).astype(BF16).reshape(b, -1, C_KV_HEADS, C_HEAD_DIM)
    vv = jnp.concatenate([vv, jnp.ones_like(vv)], axis=-1).reshape(b, -1, 2 * C_KV_HEADS * C_HEAD_DIM)
    kit = jnp.pad(pad_s(keys_i), ((0, 0), (0, 0), (0, LANES - IDX_DIM))).transpose(0, 2, 1).astype(BF16)
    qi = u[:, :, OD_QI:OD_QI + IDX_HEADS * IDX_DIM].reshape(b, t, IDX_HEADS, IDX_DIM).transpose(0, 2, 1, 3)
    qi = jnp.pad(qi, ((0, 0), (0, 0), (0, 0), (0, LANES - IDX_DIM))).astype(BF16)
    att = _dsa(u, qi, kt, vv, kit, tq, top_k, causal, limit)
    x2 = _proj_ln(att.reshape(m, C_WIDTH), x.reshape(m, D_MODEL), p['w_out'], p['ln1_g'], p['ln1_b'],
                  _pick(m, (512, 256, 128, 64, 32, 16, 8)))
    x2 = _experts(x2, p['router_w'], p['router_b'], p['e_gate'], p['e_up'], p['e_down'],
                  p['ln2_g'], p['ln2_b'], _pick(m, (1024, 512, 256, 128)), routed=True)
    st = (k_new.reshape(b, t, C_KV_HEADS, C_HEAD_DIM), v_new.reshape(b, t, C_KV_HEADS, C_HEAD_DIM), ki_new)
    return x2.reshape(b, t, D_MODEL), st


def _trunk(x, shift0, wkv0, conv0, c0, n0, m0, past_k, past_v, past_ki, ev, od):
    even_out = [[] for _ in range(6)]
    odd_out = [[] for _ in range(3)]
    for layer in range(DEPTH):
        i = layer // 2
        if layer % 2 == 0:
            x, st = _even_layer(x, shift0[i], wkv0[i], conv0[i], c0[i], n0[i], m0[i], ev[i])
            for acc, val in zip(even_out, st):
                acc.append(val)
        else:
            if past_k is None:
                x, st = _odd_layer(x, None, None, None, od[i])
            else:
                x, st = _odd_layer(x, past_k[i], past_v[i], past_ki[i], od[i])
            for acc, val in zip(odd_out, st):
                acc.append(val)
    return x, [jnp.stack(v) for v in even_out], [jnp.stack(v) for v in odd_out]


def kernel(x_prompt, x_sample, state_shift, state_wkv, state_conv, state_c, state_n, state_m,
           cache_k, cache_v, cache_idx_k,
           ev_w_in, a_mu, a_w0, a_w2, a_a0, a_a2, a_g2, a_k_k, a_k_a, a_r_k, a_ln_g, a_ln_b,
           b_conv_w, b_conv_b, b_i_bias, b_f_bias, b_ln_g, b_ln_b, ev_w_out, ev_ln1_g, ev_ln1_b,
           ffn_w_gate, ffn_w_up, ffn_w_down, ev_ln2_g, ev_ln2_b,
           od_w_in, od_w_out, od_ln1_g, od_ln1_b, moe_w_router, moe_b_router,
           moe_w_gate, moe_w_up, moe_w_down, od_ln2_g, od_ln2_b):
    ew = {'w_in': ev_w_in, 'mu': a_mu, 'w0': a_w0, 'w2': a_w2, 'a0': a_a0, 'a2': a_a2, 'g2': a_g2,
          'k_k': a_k_k, 'k_a': a_k_a, 'r_k': a_r_k, 'a_ln_g': a_ln_g, 'a_ln_b': a_ln_b,
          'conv_w': b_conv_w, 'conv_b': b_conv_b, 'b_i': b_i_bias, 'b_f': b_f_bias,
          'b_ln_g': b_ln_g, 'b_ln_b': b_ln_b, 'w_out': ev_w_out, 'ln1_g': ev_ln1_g, 'ln1_b': ev_ln1_b,
          'ffn_gate': ffn_w_gate, 'ffn_up': ffn_w_up, 'ffn_down': ffn_w_down,
          'ln2_g': ev_ln2_g, 'ln2_b': ev_ln2_b}
    ow = {'w_in': od_w_in, 'w_out': od_w_out, 'ln1_g': od_ln1_g, 'ln1_b': od_ln1_b,
          'router': moe_w_router, 'router_b': moe_b_router, 'e_gate': moe_w_gate, 'e_up': moe_w_up,
          'e_down': moe_w_down, 'ln2_g': od_ln2_g, 'ln2_b': od_ln2_b}
    n_even, n_odd = ev_w_in.shape[0], od_w_in.shape[0]
    ev = [_prep_even({k: v[i] for k, v in ew.items()}) for i in range(n_even)]
    od = [_prep_odd({k: v[i] for k, v in ow.items()}) for i in range(n_odd)]

    bp = x_prompt.shape[0]
    z = functools.partial(jnp.zeros, dtype=F32)
    y_p, ep, op = _trunk(x_prompt,
                         z((n_even, bp, 1, A_COLS)), z((n_even, bp, A_HEADS, A_HEAD_DIM, A_HEAD_DIM)),
                         z((n_even, bp, 3, 2 * B_QK_WIDTH)), z((n_even, bp, B_HEADS, B_V_DIM, B_QK_DIM)),
                         z((n_even, bp, B_HEADS, B_QK_DIM)), z((n_even, bp, B_HEADS)),
                         None, None, None, ev, od)
    y_s, es, os_ = _trunk(x_sample, state_shift, state_wkv, state_conv, state_c, state_n, state_m,
                          cache_k, cache_v, cache_idx_k, ev, od)
    return (y_p, y_s,
            ep[0], es[0], ep[1], es[1], ep[2], es[2], ep[3], es[3], ep[4], es[4], ep[5], es[5],
            op[0], os_[0], op[1], os_[1], op[2], os_[2])
```

```python
import functools
import math

import jax
import jax.numpy as jnp
from jax import lax
from jax.experimental import pallas as pl
from jax.experimental.pallas import tpu as pltpu

F32 = jnp.float32
BF16 = jnp.bfloat16
I32 = jnp.int32

D_MODEL = 1024
DEPTH = 4
CHUNK = 64
A_HEADS, A_HEAD_DIM, A_WIDTH = 16, 64, 1024
A_LORA = 256
A_COLS = 3 * A_WIDTH + A_LORA
B_HEADS, B_QK_DIM, B_V_DIM = 8, 64, 128
B_QK_WIDTH, B_V_WIDTH = 512, 1024
C_HEADS, C_KV_HEADS, C_HEAD_DIM, C_GROUP = 8, 2, 128, 4
C_WIDTH = 1024
IDX_HEADS, IDX_DIM = 8, 64
TOPK_MAX = 256
N_EXPERTS = 8
ALPHA = (2 * DEPTH) ** 0.25
LN_EPS = 1e-5

LANES = 128
SUBLANES = 8
VMEM_LIMIT = 56 * 1024 * 1024

EV_R, EV_K, EV_V, EV_QK, EV_VB, EV_O, EV_LORA, EV_GATE = 0, 1024, 2048, 3072, 4096, 5120, 6144, 6400
EV_COLS_PAD = 6528
OD_Q, OD_K, OD_V, OD_QI, OD_KI = 0, 1024, 1280, 1536, 2048
OD_COLS_PAD = 2176

INT_MIN = -2 ** 31


def _cparams(*sem):
    return pltpu.CompilerParams(dimension_semantics=sem, vmem_limit_bytes=VMEM_LIMIT)


def _layer_norm(z, g, b):
    mu = jnp.mean(z, axis=-1, keepdims=True)
    d = z - mu
    var = jnp.mean(d * d, axis=-1, keepdims=True)
    return d * lax.rsqrt(var + LN_EPS) * g + b


def _hilo_dot(x, e):
    hi = x.astype(BF16)
    lo = (x - hi.astype(F32)).astype(BF16)
    return jnp.dot(hi, e, preferred_element_type=F32) + jnp.dot(lo, e, preferred_element_type=F32)


def _segsum(x, e):
    return _hilo_dot(_hilo_dot(x, e[0]), e[1])


def _mm_kernel(x_ref, w_ref, o_ref):
    o_ref[...] = jnp.dot(x_ref[...].astype(BF16), w_ref[...], preferred_element_type=F32)


def _matmul(x, w, tm, tn):
    m, k = x.shape
    n = w.shape[1]
    return pl.pallas_call(
        _mm_kernel,
        grid=(n // tn, m // tm),
        in_specs=[pl.BlockSpec((tm, k), lambda j, i: (i, 0)),
                  pl.BlockSpec((k, tn), lambda j, i: (0, j))],
        out_specs=pl.BlockSpec((tm, tn), lambda j, i: (i, j)),
        out_shape=jax.ShapeDtypeStruct((m, n), F32),
        compiler_params=_cparams("parallel", "parallel"),
        name="proj_in",
    )(x, w)


def _rwkv_pre_kernel(ur, uk, uv, ul, s0m, s0l, mum, mul_, w0, a0, kk_, ka_, rk_, w2, a2, g2, er_ref, eb_ref,
                     o_seq, o_g, o_bonus, prev_m, prev_l):
    @pl.when(pl.program_id(1) == 0)
    def _():
        prev_m[...] = s0m[0]
        prev_l[...] = s0l[0]

    tm = ur.shape[1]
    row = lax.broadcasted_iota(I32, (tm, 1), 0)

    def lerp(x, prev, mu):
        shifted = jnp.where(row == 0, prev, pltpu.roll(x, 1, 0))
        return x + (shifted - x) * mu

    r_raw, k_raw, v_raw, l_raw = ur[0], uk[0], uv[0], ul[0]
    r = lerp(r_raw, prev_m[:, 0:1024], mum[:, 0:1024])
    k = lerp(k_raw, prev_m[:, 1024:2048], mum[:, 1024:2048])
    v = lerp(v_raw, prev_m[:, 2048:3072], mum[:, 2048:3072])
    lo = lerp(l_raw, prev_l[...], mul_[...])
    prev_m[:, 0:1024] = r_raw[tm - 1:tm, :]
    prev_m[:, 1024:2048] = k_raw[tm - 1:tm, :]
    prev_m[:, 2048:3072] = v_raw[tm - 1:tm, :]
    prev_l[...] = l_raw[tm - 1:tm, :]

    zw = w0[...] + jnp.dot(jnp.tanh(lo).astype(BF16), w2[...], preferred_element_type=F32)
    decay = jnp.exp(-(math.exp(-0.5) * jax.nn.sigmoid(zw)))
    a = jax.nn.sigmoid(a0[...] + jnp.dot(lo.astype(BF16), a2[...], preferred_element_type=F32))
    g = jnp.dot(jax.nn.sigmoid(lo).astype(BF16), g2[...], preferred_element_type=F32)

    e = (er_ref[...], eb_ref[...])
    kk = k * kk_[...]
    kk = kk * lax.rsqrt(jnp.maximum(_segsum(kk * kk, e), 1e-24))
    k2 = k * (1.0 + (a - 1.0) * ka_[...])
    bonus = _segsum(r * k2 * rk_[...], e) * v

    o_seq[WKV_SRC_R] = r
    o_seq[WKV_SRC_W] = decay
    o_seq[WKV_SRC_K] = k2
    o_seq[WKV_SRC_V] = v
    o_seq[WKV_SRC_A] = -kk
    o_seq[WKV_SRC_B] = kk * a
    o_g[...] = g
    o_bonus[...] = bonus


def _rwkv_pre(u, s0_main, s0_lora, p, tm):
    b, t, _ = u.shape
    blk = lambda w, c: pl.BlockSpec((1, tm, w), lambda i, j, c=c: (i, j, c))
    full = lambda arr: pl.BlockSpec(arr.shape, lambda i, j: (0,) * arr.ndim)
    params = [p['mu_main'], p['mu_lora'], p['w0'], p['a0'], p['k_k'], p['k_a'], p['r_k'],
              p['w2p'], p['a2p'], p['g2p'], p['seg_red'], p['seg_bc']]
    out = jax.ShapeDtypeStruct((t, b * A_WIDTH), F32)
    return pl.pallas_call(
        _rwkv_pre_kernel,
        grid=(b, t // tm),
        in_specs=[blk(1024, EV_R // 1024), blk(1024, EV_K // 1024), blk(1024, EV_V // 1024),
                  blk(A_LORA, EV_LORA // A_LORA),
                  pl.BlockSpec((1, 1, 3072), lambda i, j: (i, 0, 0)),
                  pl.BlockSpec((1, 1, A_LORA), lambda i, j: (i, 0, 0))] + [full(a) for a in params],
        out_specs=[pl.BlockSpec((WKV_N_SRC, tm, A_WIDTH), lambda i, j: (0, j, i)),
                   pl.BlockSpec((tm, A_WIDTH), lambda i, j: (j, i)),
                   pl.BlockSpec((tm, A_WIDTH), lambda i, j: (j, i))],
        out_shape=[jax.ShapeDtypeStruct((WKV_N_SRC, t, b * A_WIDTH), F32), out, out],
        scratch_shapes=[pltpu.VMEM((1, 3072), F32), pltpu.VMEM((1, A_LORA), F32)],
        compiler_params=_cparams("parallel", "arbitrary"),
        name="rwkv_pre",
    )(u, u, u, u, s0_main, s0_lora, *params)


WKV_N_SRC = 6
WKV_SRC_R, WKV_SRC_W, WKV_SRC_K, WKV_SRC_V, WKV_SRC_A, WKV_SRC_B = range(WKV_N_SRC)
WKV_VGROUP = 4
WKV_KCHUNK = 32


def _wkv_kernel(seq_ref, s0_ref, y_ref, s_ref, tr, sa_buf, ybuf):
    @pl.when(pl.program_id(1) == 0)
    def _():
        s_ref[...] = s0_ref[...]

    n_t = seq_ref.shape[1]
    d = A_HEAD_DIM
    vspan = WKV_VGROUP * SUBLANES
    groups = [(p, g) for p in range(2) for g in range(d // vspan)]
    n_kc = d // WKV_KCHUNK

    def vrows(g, j):
        return slice(g * vspan + j * SUBLANES, g * vspan + (j + 1) * SUBLANES)

    def transpose_tile(i, tok):
        tr[i, tok] = jnp.transpose(seq_ref[i, tok])

    for i in range(WKV_N_SRC):
        transpose_tile(i, 0)
    transpose_tile(WKV_SRC_A, min(1, n_t - 1))

    for p, g in groups:
        acc = [jnp.zeros((SUBLANES, LANES), F32) for _ in range(WKV_VGROUP)]
        for kx in range(d):
            ab = tr[WKV_SRC_A, 0, p * d + kx:p * d + kx + 1, :]
            for j in range(WKV_VGROUP):
                acc[j] = acc[j] + s_ref[p, kx, vrows(g, j), :] * ab
        for j in range(WKV_VGROUP):
            sa_buf[p, vrows(g, j), :] = acc[j]

    def step(t, carry):
        tn = jnp.minimum(t + 1, n_t - 1)
        for gi, (p, g) in enumerate(groups):
            vv = [tr[WKV_SRC_V, t, p * d + g * vspan + j * SUBLANES:p * d + g * vspan + (j + 1) * SUBLANES, :]
                  for j in range(WKV_VGROUP)]
            sa = [sa_buf[p, vrows(g, j), :] for j in range(WKV_VGROUP)]
            zero = tuple(jnp.zeros((SUBLANES, LANES), F32) for _ in range(2 * WKV_VGROUP))

            def key_chunk(kc, accs, gi=gi, p=p, g=g, vv=vv, sa=sa):
                yacc, san = list(accs[:WKV_VGROUP]), list(accs[WKV_VGROUP:])
                for i in range(WKV_KCHUNK):
                    kx = kc * WKV_KCHUNK + i
                    row = pl.ds(p * d + kx, 1)
                    wb = tr[WKV_SRC_W, t, row, :]
                    bb = tr[WKV_SRC_B, t, row, :]
                    kb = tr[WKV_SRC_K, t, row, :]
                    rb = tr[WKV_SRC_R, t, row, :]
                    an = tr[WKV_SRC_A, tn, row, :]
                    for j in range(WKV_VGROUP):
                        n = s_ref[p, kx, vrows(g, j), :] * wb + sa[j] * bb + vv[j] * kb
                        s_ref[p, kx, vrows(g, j), :] = n
                        yacc[j] = yacc[j] + n * rb
                        san[j] = san[j] + n * an
                tile = jnp.minimum(gi * n_kc + kc, WKV_N_SRC - 1)
                ahead = jnp.where(tile == WKV_SRC_A, 2, 1)
                transpose_tile(tile, jnp.minimum(t + ahead, n_t - 1))
                return tuple(yacc) + tuple(san)

            accs = lax.fori_loop(0, n_kc, key_chunk, zero)
            yacc, san = accs[:WKV_VGROUP], accs[WKV_VGROUP:]
            for j in range(WKV_VGROUP):
                ybuf[t, p * d + g * vspan + j * SUBLANES:p * d + g * vspan + (j + 1) * SUBLANES, :] = yacc[j]
                sa_buf[p, vrows(g, j), :] = san[j]
        return carry

    lax.fori_loop(0, n_t, step, 0)

    def transpose_out(t, carry):
        y_ref[t] = jnp.transpose(ybuf[t])
        return carry

    lax.fori_loop(0, n_t, transpose_out, 0, unroll=4)


def _wkv_scan(seq, s0, tt):
    _, t, rows, _ = seq.shape
    d = A_HEAD_DIM
    st = pl.BlockSpec((2, d, d, LANES), lambda g, i: (0, 0, 0, g))
    return pl.pallas_call(
        _wkv_kernel,
        grid=(rows // LANES, t // tt),
        in_specs=[pl.BlockSpec((WKV_N_SRC, tt, LANES, LANES), lambda g, i: (0, i, g, 0)), st],
        out_specs=[pl.BlockSpec((tt, LANES, LANES), lambda g, i: (i, g, 0)), st],
        out_shape=[jax.ShapeDtypeStruct((t, rows, LANES), F32), jax.ShapeDtypeStruct((2, d, d, rows), F32)],
        scratch_shapes=[pltpu.VMEM((WKV_N_SRC, tt, LANES, LANES), F32), pltpu.VMEM((2, d, LANES), F32),
                        pltpu.VMEM((tt, LANES, LANES), F32)],
        compiler_params=_cparams("parallel", "arbitrary"),
        name="wkv_scan",
    )(seq, s0)


def _mlstm_kernel(uqk, uv, uo, ug, conv0, ct0, m0, cw, cb, gb, lng, lnb,
                  o_y, o_ct, o_m, ct, m_scr, carry, *, t_valid):
    tb = pl.program_id(1)

    @pl.when(tb == 0)
    def _():
        ct[...] = ct0[...]
        m_scr[...] = m0[...]
        carry[...] = conv0[...]

    for bi in range(uqk.shape[0]):
        one = pl.ds(bi, 1)
        _mlstm_chunk(tb, uqk.at[one], uv.at[one], uo.at[one], ug.at[one], cw, cb, gb, lng, lnb,
                     o_y.at[one], ct.at[bi], m_scr.at[bi], carry.at[bi], t_valid=t_valid)

    @pl.when(tb == pl.num_programs(1) - 1)
    def _():
        o_ct[...] = ct[...]
        o_m[...] = m_scr[...]


def _mlstm_chunk(tb, uqk, uv, uo, ug, cw, cb, gb, lng, lnb, o_y, ct, m_scr, carry, *, t_valid):
    L = uqk.shape[1]
    row = lax.broadcasted_iota(I32, (L, 1), 0)
    x = uqk[0]

    def shifted(j):
        out = pltpu.roll(x, j, 0)
        for i in range(j):
            out = jnp.where(row == i, carry[3 + i - j:4 + i - j, :], out)
        return out

    conv = x * cw[3:4, :] + shifted(1) * cw[2:3, :] + shifted(2) * cw[1:2, :] + shifted(3) * cw[0:1, :] + cb[...]
    carry[...] = x[L - 3:L, :]
    qk = conv * jax.nn.sigmoid(conv)
    q_all = qk[:, 0:B_QK_WIDTH]
    k_t = jnp.transpose(qk[:, B_QK_WIDTH:2 * B_QK_WIDTH] * (B_QK_DIM ** -0.5))

    lane = lax.broadcasted_iota(I32, (L, LANES), 1)
    gz = ug[0] + gb[...]
    lg = jnp.where(lane < B_HEADS, gz, jnp.minimum(gz, 0.0) - jnp.log(1.0 + jnp.exp(-jnp.abs(gz))))
    valid = (row + tb * L) < t_valid
    lg = jnp.where(valid, lg, jnp.where(lane < B_HEADS, -jnp.inf, 0.0))
    lg_t = jnp.transpose(lg)
    ti = lax.broadcasted_iota(I32, (L, L), 0)
    si = lax.broadcasted_iota(I32, (L, L), 1)
    tril = ti >= si
    tril_f = tril.astype(F32)
    triu_f = (ti <= si).astype(F32)
    lg_fin = jnp.where(lane < B_HEADS, 0.0, lg)
    bcum = jnp.dot(tril_f, lg_fin, preferred_element_type=F32, precision=lax.Precision.HIGHEST)
    bcum_t = jnp.dot(jnp.transpose(lg_fin), triu_f, preferred_element_type=F32,
                     precision=lax.Precision.HIGHEST)

    head_q = lax.shift_right_logical(lax.broadcasted_iota(I32, (L, B_QK_WIDTH), 1), 6)
    lane1 = lax.broadcasted_iota(I32, (1, LANES), 1)
    ones_col = (lax.broadcasted_iota(I32, (L, LANES), 1) == 0).astype(F32)
    m_vec = m_scr[...]
    m_new_vec = m_vec
    lane_ct = lax.broadcasted_iota(I32, (B_QK_DIM, 2 * LANES), 1)
    ct_old = ct[...]
    ct_old_bf = ct_old.astype(BF16)
    k_bf = k_t.astype(BF16)
    vv = uv[0]
    oo = uo[0]
    q_stack = jnp.concatenate([jnp.where(head_q == h, q_all, 0.0) for h in range(B_HEADS)], axis=0).astype(BF16)
    qk_all = jnp.dot(q_stack, k_bf, preferred_element_type=F32)
    cross_all = jnp.dot(q_stack, ct_old_bf, preferred_element_type=F32)
    for h in range(B_HEADS):
        hs = slice(h * B_QK_DIM, (h + 1) * B_QK_DIM)
        vs = slice(h * B_V_DIM, (h + 1) * B_V_DIM)
        bc_col = bcum[:, B_HEADS + h:B_HEADS + h + 1]
        bc_row = bcum_t[B_HEADS + h:B_HEADS + h + 1, :]
        ic_row = lg_t[h:h + 1, :]
        gtot = bc_row[:, L - 1:L]
        m_prev = m_vec[:, h:h + 1]
        dmat = jnp.where(tril, bc_col - bc_row + ic_row, -jnp.inf)
        inter = bc_col + m_prev
        m_t = jnp.maximum(inter, jnp.max(dmat, axis=1, keepdims=True))
        w_intra = jnp.exp(dmat - m_t)
        w_inter = jnp.exp(inter - m_t)
        s = qk_all[h * L:(h + 1) * L, :] * w_intra
        v_ext = jnp.concatenate([vv[:, vs], ones_col], axis=1).astype(BF16)
        intra = jnp.dot(s.astype(BF16), v_ext, preferred_element_type=F32)
        cross = cross_all[h * L:(h + 1) * L, :]
        num = intra[:, 0:B_V_DIM] + w_inter * cross[:, 0:B_V_DIM]
        den = jnp.sum(s, axis=1, keepdims=True) + w_inter * cross[:, B_V_DIM:B_V_DIM + 1]
        hh = num / jnp.maximum(jnp.abs(den), jnp.exp(-m_t))
        mu = jnp.mean(hh, axis=-1, keepdims=True)
        dd = hh - mu
        var = jnp.mean(dd * dd, axis=-1, keepdims=True)
        yn = dd * lax.rsqrt(var + LN_EPS) * lng[:, vs] + lnb[:, vs]
        o_y[0, :, vs] = jax.nn.sigmoid(oo[:, vs]) * yn
        lw = gtot - bc_row + ic_row
        m_new = jnp.maximum(gtot + m_prev, jnp.max(lw, axis=1, keepdims=True))
        w_s = jnp.exp(lw - m_new)
        dec = jnp.exp(gtot + m_prev - m_new)
        kw = k_t[hs, :] * w_s
        upd = jnp.dot(kw.astype(BF16), v_ext, preferred_element_type=F32)
        upd = jnp.where(lane_ct == B_V_DIM, jnp.sum(kw, axis=1, keepdims=True), upd)
        ct[hs, :] = dec * ct_old[hs, :] + upd
        m_new_vec = jnp.where(lane1 == h, m_new, m_new_vec)
    m_scr[...] = m_new_vec


MLSTM_ROWS = 1


def _mlstm(u, conv0, ct0, m0, p, L, t_valid):
    b, t, _ = u.shape
    nb = MLSTM_ROWS if b % MLSTM_ROWS == 0 else 1
    blk = lambda w, c: pl.BlockSpec((nb, L, w), lambda i, j, c=c: (i, j, c))
    full = lambda arr: pl.BlockSpec(arr.shape, lambda i, j: (0,) * arr.ndim)
    per_b = lambda arr: pl.BlockSpec((nb,) + arr.shape[1:], lambda i, j: (i, 0, 0))
    params = [p['conv_w'], p['conv_b'], p['gate_b'], p['b_ln_g'], p['b_ln_b']]
    return pl.pallas_call(
        functools.partial(_mlstm_kernel, t_valid=t_valid),
        grid=(b // nb, t // L),
        in_specs=[blk(1024, EV_QK // 1024), blk(1024, EV_VB // 1024), blk(1024, EV_O // 1024),
                  blk(LANES, EV_GATE // LANES), per_b(conv0), per_b(ct0), per_b(m0)] + [full(a) for a in params],
        out_specs=[pl.BlockSpec((nb, L, B_V_WIDTH), lambda i, j: (i, j, 0)),
                   pl.BlockSpec((nb, B_QK_WIDTH, 2 * LANES), lambda i, j: (i, 0, 0)),
                   pl.BlockSpec((nb, 1, LANES), lambda i, j: (i, 0, 0))],
        out_shape=[jax.ShapeDtypeStruct((b, t, B_V_WIDTH), F32),
                   jax.ShapeDtypeStruct((b, B_QK_WIDTH, 2 * LANES), F32),
                   jax.ShapeDtypeStruct((b, 1, LANES), F32)],
        scratch_shapes=[pltpu.VMEM((nb, B_QK_WIDTH, 2 * LANES), F32), pltpu.VMEM((nb, 1, LANES), F32),
                        pltpu.VMEM((nb, 3, 2 * B_QK_WIDTH), F32)],
        compiler_params=_cparams("parallel", "arbitrary"),
        name="mlstm",
    )(u, u, u, u, conv0, ct0, m0, *params)


def _even_out_kernel(y_ref, bonus_ref, g_ref, yb_ref, x_ref, lng, lnb, er_ref, eb_ref, woa, wob, g1, b1, o_ref):
    e = (er_ref[...], eb_ref[...])
    y = y_ref[...]
    mu = _segsum(y, e) * (1.0 / A_HEAD_DIM)
    d = y - mu
    var = _segsum(d * d, e) * (1.0 / A_HEAD_DIM)
    yn = d * lax.rsqrt(var + LN_EPS) * lng[...] + lnb[...]
    ya = (yn + bonus_ref[...]) * g_ref[...]
    mix = (jnp.dot(ya.astype(BF16), woa[...], preferred_element_type=F32)
           + jnp.dot(yb_ref[...].astype(BF16), wob[...], preferred_element_type=F32))
    o_ref[...] = _layer_norm(ALPHA * x_ref[...] + mix, g1[...], b1[...])


def _even_out(y, bonus, g, yb, x, p, b, tm):
    m = x.shape[0]
    nt = m // b // tm
    tmaj = pl.BlockSpec((tm, D_MODEL), lambda i, j: (j, i))
    row = pl.BlockSpec((tm, D_MODEL), lambda i, j: (i * nt + j, 0))
    full = lambda arr: pl.BlockSpec(arr.shape, lambda i, j: (0,) * arr.ndim)
    params = [p['a_ln_g'], p['a_ln_b'], p['seg_red'], p['seg_bc'], p['w_out_a'], p['w_out_b'], p['ln1_g'], p['ln1_b']]
    return pl.pallas_call(
        _even_out_kernel,
        grid=(b, nt),
        in_specs=[tmaj] * 3 + [row] * 2 + [full(a) for a in params],
        out_specs=row,
        out_shape=jax.ShapeDtypeStruct((m, D_MODEL), F32),
        compiler_params=_cparams("parallel", "parallel"),
        name="even_out",
    )(y, bonus, g, yb, x, *params)


def _proj_ln_kernel(a_ref, x_ref, w_ref, g1, b1, o_ref):
    mix = jnp.dot(a_ref[...].astype(BF16), w_ref[...], preferred_element_type=F32)
    o_ref[...] = _layer_norm(ALPHA * x_ref[...] + mix, g1[...], b1[...])


def _proj_ln(a, x, w, g1, b1, tm):
    m = x.shape[0]
    row = pl.BlockSpec((tm, D_MODEL), lambda i: (i, 0))
    full = lambda arr: pl.BlockSpec(arr.shape, lambda i: (0,) * arr.ndim)
    return pl.pallas_call(
        _proj_ln_kernel,
        grid=(m // tm,),
        in_specs=[row, row, full(w), full(g1), full(b1)],
        out_specs=row,
        out_shape=jax.ShapeDtypeStruct((m, D_MODEL), F32),
        compiler_params=_cparams("parallel"),
        name="proj_ln",
    )(a, x, w, g1, b1)


def _swiglu(xb, wg, wu, wd):
    hg = jnp.dot(xb, wg, preferred_element_type=F32)
    hu = jnp.dot(xb, wu, preferred_element_type=F32)
    hid = (hg * jax.nn.sigmoid(hg)) * hu
    return jnp.dot(hid.astype(BF16), wd, preferred_element_type=F32)


def _ffn_kernel(x_ref, wg, wu, wd, g2, b2, o_ref, xb):
    e = pl.program_id(1)

    @pl.when(e == 0)
    def _():
        xb[...] = x_ref[...].astype(BF16)
        o_ref[...] = jnp.zeros_like(o_ref)

    o_ref[...] += _swiglu(xb[...], wg[0], wu[0], wd[0])

    @pl.when(e == pl.num_programs(1) - 1)
    def _():
        o_ref[...] = _layer_norm(ALPHA * x_ref[...] + o_ref[...], g2[...], b2[...])


MOE_CHUNK = 256


def _moe_kernel(x_ref, wr, br, wg, wu, wd, g2, b2, o_ref, xb, comb, pos_scr, post_scr):
    e = pl.program_id(1)
    tm = x_ref.shape[0]

    @pl.when(e == 0)
    def _():
        x = x_ref[...]
        xb[...] = x.astype(BF16)
        o_ref[...] = jnp.zeros_like(o_ref)
        logits = jnp.dot(x, wr[...], preferred_element_type=F32, precision=lax.Precision.HIGHEST) + br[...]
        lane = lax.broadcasted_iota(I32, logits.shape, 1).astype(F32)
        v1 = jnp.max(logits, axis=-1, keepdims=True)
        i1 = jnp.min(jnp.where(logits == v1, lane, float(LANES)), axis=-1, keepdims=True)
        rest = jnp.where(lane == i1, -jnp.inf, logits)
        v2 = jnp.max(rest, axis=-1, keepdims=True)
        i2 = jnp.min(jnp.where(rest == v2, lane, float(LANES)), axis=-1, keepdims=True)
        e2 = jnp.exp(v2 - v1)
        den = 1.0 + e2
        comb[...] = jnp.where(lane == i1, 1.0 / den, jnp.where(lane == i2, e2 / den, 0.0))
        member = jnp.where(lane == i1, 1.0, jnp.where(lane == i2, 1.0, 0.0))
        earlier = (lax.broadcasted_iota(I32, (tm, tm), 0) > lax.broadcasted_iota(I32, (tm, tm), 1))
        pos = jnp.dot(jnp.where(earlier, 1.0, 0.0).astype(BF16), member.astype(BF16), preferred_element_type=F32)
        posm = jnp.where(member > 0.0, pos, -1.0)
        pos_scr[...] = posm
        post_scr[...] = jnp.transpose(posm)

    sel = lax.broadcasted_iota(I32, (tm, LANES), 1) == e
    gate = jnp.sum(jnp.where(sel, comb[...], 0.0), axis=-1, keepdims=True)
    slot_col = jnp.sum(jnp.where(sel, pos_scr[...], 0.0), axis=-1, keepdims=True)
    slot_row = post_scr[pl.ds(e, 1), :]
    count = jnp.max(slot_row) + 1.0

    def run_slots(base, n):
        rows = lax.broadcasted_iota(I32, (n, 1), 0).astype(F32) + float(base)
        gather = jnp.where(slot_row == rows, 1.0, 0.0).astype(BF16)
        xg = jnp.dot(gather, xb[...], preferred_element_type=F32).astype(BF16)
        y = _swiglu(xg, wg[0], wu[0], wd[0])
        cols = lax.broadcasted_iota(I32, (1, n), 1).astype(F32) + float(base)
        scatter = jnp.where(slot_col == cols, 1.0, 0.0).astype(BF16)
        o_ref[...] += gate * jnp.dot(scatter, y.astype(BF16), preferred_element_type=F32)

    chunk = min(MOE_CHUNK, tm)
    half = chunk // 2
    for c in range(tm // chunk):
        lo = c * chunk

        @pl.when(count > lo + half)
        def _(lo=lo):
            run_slots(lo, chunk)

        @pl.when(jnp.logical_and(count > lo, count <= lo + half))
        def _(lo=lo):
            run_slots(lo, half)

    @pl.when(e == pl.num_programs(1) - 1)
    def _():
        o_ref[...] = _layer_norm(ALPHA * x_ref[...] + o_ref[...], g2[...], b2[...])


def _experts(x, wr, br, wg, wu, wd, g2, b2, tm, routed):
    m = x.shape[0]
    ne, _, dff = wg.shape
    row = pl.BlockSpec((tm, D_MODEL), lambda i, e: (i, 0))
    full = lambda arr: pl.BlockSpec(arr.shape, lambda i, e: (0,) * arr.ndim)
    wspecs = [pl.BlockSpec((1, D_MODEL, dff), lambda i, e: (e, 0, 0)),
              pl.BlockSpec((1, D_MODEL, dff), lambda i, e: (e, 0, 0)),
              pl.BlockSpec((1, dff, D_MODEL), lambda i, e: (e, 0, 0))]
    common = dict(grid=(m // tm, ne), out_specs=row, out_shape=jax.ShapeDtypeStruct((m, D_MODEL), F32),
                  compiler_params=_cparams("parallel", "arbitrary"))
    if not routed:
        return pl.pallas_call(
            _ffn_kernel, in_specs=[row] + wspecs + [full(g2), full(b2)],
            scratch_shapes=[pltpu.VMEM((tm, D_MODEL), BF16)], name="experts_dense", **common,
        )(x, wg, wu, wd, g2, b2)
    return pl.pallas_call(
        _moe_kernel, in_specs=[row, full(wr), full(br)] + wspecs + [full(g2), full(b2)],
        scratch_shapes=[pltpu.VMEM((tm, D_MODEL), BF16), pltpu.VMEM((tm, LANES), F32),
                        pltpu.VMEM((tm, LANES), F32), pltpu.VMEM((LANES, tm), F32)],
        name="experts_routed", **common,
    )(x, wr, br, wg, wu, wd, g2, b2)


DSA_COARSE_BITS = 28


def _dsa_kernel(q_ref, qi_ref, wi_ref, kt_ref, v_ref, kit_ref, *rest, top_k, causal, limit_const, blk_off, has_prev):
    o_ref, key_scr, jcut_scr, thr_scr = rest[1:] if has_prev else rest
    tq = q_ref.shape[1]
    s_len = kt_ref.shape[2]
    if causal:
        row_chunk = lax.shift_right_logical(lax.broadcasted_iota(I32, (tq, 1), 0), CHUNK.bit_length() - 1)
        chunk = (pl.program_id(1) + blk_off) * (tq // CHUNK) + row_chunk
        limit = (chunk + 1) * CHUNK
        kth = jnp.minimum(top_k, limit).astype(F32)
    else:
        limit = limit_const
        kth = jnp.full((tq, 1), min(top_k, limit_const), F32)
    idx = lax.broadcasted_iota(I32, (tq, s_len), 1)

    kit = kit_ref[0]
    wi = wi_ref[0] * (IDX_HEADS ** -0.5)
    score = jnp.zeros((tq, s_len), F32)
    for h in range(IDX_HEADS):
        q_h = qi_ref[0, :, h * IDX_DIM:(h + 1) * IDX_DIM].astype(BF16)
        rel = jnp.dot(q_h, kit, preferred_element_type=F32)
        score = score + wi[:, IDX_DIM + h:IDX_DIM + h + 1] * jnp.maximum(rel, 0.0)
    score = jnp.where(score == 0.0, 0.0, score)
    bits = pltpu.bitcast(score, I32)
    key = jnp.where(bits < 0, bits ^ 0x7FFFFFFF, bits)
    key_scr[...] = jnp.where(idx < limit, key, INT_MIN)

    n_grp = 2 if tq % (4 * SUBLANES) == 0 else 1
    rg = tq // n_grp
    rows_of = lambda g: slice(g * rg, (g + 1) * rg)

    def count_ge(thrs):
        return [jnp.sum(jnp.where(key_scr[rows_of(g), :] >= thrs[g], 1.0, 0.0), axis=1, keepdims=True)
                for g in range(n_grp)]

    def bit_step(i, thrs):
        bit = lax.shift_left(jnp.int32(1), 31 - i)
        cands = [t + bit for t in thrs]
        cnts = count_ge(cands)
        return tuple(jnp.where(cnts[g] >= kth[rows_of(g)], cands[g], thrs[g]) for g in range(n_grp))

    thrs = lax.fori_loop(0, DSA_COARSE_BITS, bit_step,
                         tuple(jnp.full((rg, 1), INT_MIN, I32) for _ in range(n_grp)), unroll=4)
    thr_scr[...] = jnp.concatenate(thrs, axis=0)
    n_coarse = jnp.concatenate(count_ge(thrs), axis=0)

    @pl.when(jnp.max(n_coarse - kth) > 0.0)
    def _():
        fine = lax.fori_loop(DSA_COARSE_BITS, 32, bit_step, tuple(thr_scr[rows_of(g), :] for g in range(n_grp)),
                             unroll=4)
        thr_scr[...] = jnp.concatenate(fine, axis=0)

    thr = thr_scr[...]
    keyv = key_scr[...]
    n_ge = jnp.sum(jnp.where(keyv >= thr, 1.0, 0.0), axis=1, keepdims=True)
    jcut_scr[...] = jnp.full((tq, 1), s_len, I32)

    @pl.when(jnp.max(n_ge - kth) > 0.0)
    def _():
        need = kth - jnp.sum(jnp.where(key_scr[...] > thr, 1.0, 0.0), axis=1, keepdims=True)
        n_bits = max(1, (s_len - 1).bit_length())

        def idx_step(i, j):
            cand = j + lax.shift_left(jnp.int32(1), n_bits - 1 - i)
            cnt = jnp.sum(jnp.where(key_scr[...] == thr, jnp.where(idx < cand, 1.0, 0.0), 0.0),
                          axis=1, keepdims=True)
            return jnp.where(cnt < need, cand, j)

        jcut_scr[...] = lax.fori_loop(0, n_bits, idx_step, jnp.zeros((tq, 1), I32))

    jcut = jcut_scr[...]
    bias = jnp.where(keyv > thr, 0.0,
                     jnp.where(keyv == thr, jnp.where(idx <= jcut, 0.0, -jnp.inf), -jnp.inf)).astype(BF16)

    q = q_ref[0] * (C_HEAD_DIM ** -0.5)
    for grp in range(C_KV_HEADS):
        kt = kt_ref[0, grp * C_HEAD_DIM:(grp + 1) * C_HEAD_DIM, :]
        v_ext = v_ref[0, :, grp * 2 * C_HEAD_DIM:(grp + 1) * 2 * C_HEAD_DIM]
        hsl = [slice((grp * C_GROUP + hg) * C_HEAD_DIM, (grp * C_GROUP + hg + 1) * C_HEAD_DIM)
               for hg in range(C_GROUP)]
        qg = jnp.concatenate([q[:, sl] for sl in hsl], axis=0).astype(BF16)
        lg = jnp.dot(qg, kt, preferred_element_type=F32).astype(BF16)
        ps = []
        for hg in range(C_GROUP):
            logits = lg[hg * tq:(hg + 1) * tq, :] + bias
            m_tile = functools.reduce(jnp.maximum, [logits[:, i * LANES:(i + 1) * LANES]
                                                    for i in range(s_len // LANES)])
            mx = jnp.max(m_tile.astype(F32), axis=1, keepdims=True).astype(BF16)
            ps.append(jnp.exp(logits - mx))
        out = jnp.dot(jnp.concatenate(ps, axis=0), v_ext, preferred_element_type=F32)
        for hg in range(C_GROUP):
            rows = slice(hg * tq, (hg + 1) * tq)
            o_ref[0, :, hsl[hg]] = out[rows, 0:C_HEAD_DIM] / out[rows, C_HEAD_DIM:C_HEAD_DIM + 1]


DSA_KEY_STEP = 512
DSA_WIDE_Q_KEYS = 3584


def _dsa_call(u, kt, v, kit, prev, tq, top_k, causal, limit_const, blk_off, n_blk, s_len):
    b, t, _ = u.shape
    in_specs = [pl.BlockSpec((1, tq, C_WIDTH), lambda i, j: (i, j + blk_off, OD_Q // C_WIDTH)),
                pl.BlockSpec((1, tq, IDX_HEADS * IDX_DIM),
                             lambda i, j: (i, j + blk_off, OD_QI // (IDX_HEADS * IDX_DIM))),
                pl.BlockSpec((1, tq, LANES), lambda i, j: (i, j + blk_off, OD_KI // LANES)),
                pl.BlockSpec((1, 2 * C_HEAD_DIM, s_len), lambda i, j: (i, 0, 0)),
                pl.BlockSpec((1, s_len, 2 * C_KV_HEADS * C_HEAD_DIM), lambda i, j: (i, 0, 0)),
                pl.BlockSpec((1, IDX_DIM, s_len), lambda i, j: (i, 0, 0))]
    args = [u, u, u, kt, v, kit]
    aliases = {}
    if prev is not None:
        in_specs.append(pl.BlockSpec(memory_space=pl.ANY))
        args.append(prev)
        aliases = {len(args) - 1: 0}
    return pl.pallas_call(
        functools.partial(_dsa_kernel, top_k=top_k, causal=causal, limit_const=limit_const, blk_off=blk_off,
                          has_prev=prev is not None),
        grid=(b, n_blk),
        in_specs=in_specs,
        out_specs=pl.BlockSpec((1, tq, C_WIDTH), lambda i, j: (i, j + blk_off, 0)),
        out_shape=jax.ShapeDtypeStruct((b, t, C_WIDTH), F32),
        scratch_shapes=[pltpu.VMEM((tq, s_len), I32), pltpu.VMEM((tq, 1), I32), pltpu.VMEM((tq, 1), I32)],
        input_output_aliases=aliases,
        compiler_params=_cparams("parallel", "arbitrary"),
        name="dsa",
    )(*args)


def _dsa(u, kt, v, kit, tq, top_k, causal, limit_const):
    b, t, _ = u.shape
    s_full = kt.shape[2]
    if not causal or s_full % DSA_KEY_STEP:
        return _dsa_call(u, kt, v, kit, None, tq, top_k, causal, limit_const, 0, t // tq, s_full)
    att = None
    for cls in range(s_full // DSA_KEY_STEP):
        s_len = (cls + 1) * DSA_KEY_STEP
        tq_c = 2 * tq if s_len <= DSA_WIDE_Q_KEYS and DSA_KEY_STEP % (2 * tq) == 0 else tq
        per = DSA_KEY_STEP // tq_c
        att = _dsa_call(u, kt, v, kit, att, tq_c, top_k, causal, limit_const, cls * per, per, s_len)
    return att


def _prep_even(w):
    win = w['w_in']
    a, bq = win[:, :A_COLS], win[:, A_COLS:]
    cols = [a[:, 0:3072], bq[:, 0:3072], a[:, 3072:A_COLS], bq[:, 3072:3088],
            jnp.zeros((D_MODEL, EV_COLS_PAD - EV_GATE - 2 * B_HEADS), F32)]
    p = {'w_in': jnp.concatenate(cols, axis=1).astype(BF16)}
    mu = w['mu']
    p['mu_main'] = mu[None, 0:3072]
    p['mu_lora'] = mu[None, 3072:A_COLS]
    row = lambda v: v.reshape(1, -1)
    p['w0'], p['a0'], p['k_k'], p['k_a'] = row(w['w0']), row(w['a0']), row(w['k_k']), row(w['k_a'])
    p['r_k'] = row(w['r_k'])
    z = lambda n: jnp.zeros((n, A_WIDTH), F32)
    p['w2p'] = jnp.concatenate([w['w2'], z(192)], axis=0).astype(BF16)
    p['a2p'] = jnp.concatenate([z(64), w['a2'], z(128)], axis=0).astype(BF16)
    p['g2p'] = jnp.concatenate([z(128), w['g2']], axis=0).astype(BF16)
    seg = jnp.arange(A_WIDTH) // A_HEAD_DIM
    p['seg_red'] = (seg[:, None] == jnp.arange(LANES)[None, :]).astype(BF16)
    p['seg_bc'] = p['seg_red'].T
    p['a_ln_g'], p['a_ln_b'] = row(w['a_ln_g']), row(w['a_ln_b'])
    p['conv_w'], p['conv_b'] = w['conv_w'], row(w['conv_b'])
    p['gate_b'] = jnp.concatenate([w['b_i'], w['b_f'], jnp.zeros((LANES - 2 * B_HEADS,), F32)])[None]
    p['b_ln_g'], p['b_ln_b'] = row(w['b_ln_g']), row(w['b_ln_b'])
    p['w_out_a'] = w['w_out'][:A_WIDTH].astype(BF16)
    p['w_out_b'] = w['w_out'][A_WIDTH:].astype(BF16)
    p['ln1_g'], p['ln1_b'], p['ln2_g'], p['ln2_b'] = row(w['ln1_g']), row(w['ln1_b']), row(w['ln2_g']), row(w['ln2_b'])
    half = w['ffn_gate'].shape[1] // 2
    p['ffn_g'] = jnp.stack([w['ffn_gate'][:, :half], w['ffn_gate'][:, half:]]).astype(BF16)
    p['ffn_u'] = jnp.stack([w['ffn_up'][:, :half], w['ffn_up'][:, half:]]).astype(BF16)
    p['ffn_d'] = jnp.stack([w['ffn_down'][:half], w['ffn_down'][half:]]).astype(BF16)
    return p


def _prep_odd(w):
    row = lambda v: v.reshape(1, -1)
    p = {'w_in': jnp.pad(w['w_in'], ((0, 0), (0, OD_COLS_PAD - w['w_in'].shape[1]))).astype(BF16)}
    p['w_out'] = w['w_out'].astype(BF16)
    p['ln1_g'], p['ln1_b'], p['ln2_g'], p['ln2_b'] = row(w['ln1_g']), row(w['ln1_b']), row(w['ln2_g']), row(w['ln2_b'])
    p['router_w'] = jnp.pad(w['router'], ((0, 0), (0, LANES - N_EXPERTS)))
    p['router_b'] = jnp.concatenate([w['router_b'], jnp.full((LANES - N_EXPERTS,), -1e30, F32)])[None]
    p['e_gate'], p['e_up'], p['e_down'] = w['e_gate'].astype(BF16), w['e_up'].astype(BF16), w['e_down'].astype(BF16)
    return p


def _pick(n, pref):
    for c in pref:
        if n % c == 0:
            return c
    return n


def _even_layer(x, shift0, wkv0, conv0, c0, n0, m0, p):
    b, t, _ = x.shape
    m = b * t
    tm = _pick(m, (512, 256, 128, 64, 32, 16, 8))
    u = _matmul(x.reshape(m, D_MODEL), p['w_in'], _pick(m, (256, 128, 64, 32, 16, 8)), EV_COLS_PAD)
    u = u.reshape(b, t, EV_COLS_PAD)

    new_shift = jnp.concatenate([u[:, t - 1:, 0:3072], u[:, t - 1:, EV_LORA:EV_LORA + A_LORA]], axis=-1)
    new_conv = jnp.concatenate([conv0, u[:, :, EV_QK:EV_QK + 1024]], axis=1)[:, t:]

    s0_main, s0_lora = shift0[:, :, 0:3072], shift0[:, :, 3072:A_COLS]
    tp = _pick(t, (256, 128, 64, 32, 16, 8))
    seq, g, bonus = _rwkv_pre(u, s0_main, s0_lora, p, tp)
    rows = b * A_HEADS // 2
    assert rows % LANES == 0, "batch must be a multiple of 16"
    s0 = wkv0.reshape(b, A_HEADS // 2, 2, A_HEAD_DIM, A_HEAD_DIM).transpose(2, 4, 3, 0, 1)
    s0 = s0.reshape(2, A_HEAD_DIM, A_HEAD_DIM, rows)
    tt = _pick(t, (16, 8))
    y, s_fin = _wkv_scan(seq.reshape(WKV_N_SRC, t, rows, LANES), s0, tt)
    y = y.reshape(t, b * A_WIDTH)
    new_wkv = s_fin.reshape(2, A_HEAD_DIM, A_HEAD_DIM, b, A_HEADS // 2).transpose(3, 4, 0, 2, 1)
    new_wkv = new_wkv.reshape(b, A_HEADS, A_HEAD_DIM, A_HEAD_DIM)

    ct0 = jnp.concatenate([c0.transpose(0, 1, 3, 2).reshape(b, B_QK_WIDTH, B_V_DIM),
                           n0.reshape(b, B_QK_WIDTH, 1),
                           jnp.zeros((b, B_QK_WIDTH, LANES - 1), F32)], axis=-1)
    m0p = jnp.pad(m0, ((0, 0), (0, LANES - B_HEADS)))[:, None, :]
    lm = _pick(t, (256, 128))
    if t % LANES:
        t_pad = -t % LANES
        u_b = jnp.pad(u, ((0, 0), (0, t_pad), (0, 0)))
        lm = LANES
    else:
        u_b = u
    yb, ct, m_out = _mlstm(u_b, conv0, ct0, m0p, p, lm, t)
    yb = yb[:, :t].reshape(m, B_V_WIDTH)
    new_c = ct[:, :, 0:B_V_DIM].reshape(b, B_HEADS, B_QK_DIM, B_V_DIM).transpose(0, 1, 3, 2)
    new_n = ct[:, :, B_V_DIM].reshape(b, B_HEADS, B_QK_DIM)
    new_m = m_out[:, 0, 0:B_HEADS]

    x2 = x.reshape(m, D_MODEL)
    x2 = _even_out(y, bonus, g, yb, x2, p, b, _pick(t, (256, 128, 64, 32, 16, 8)))
    x2 = _experts(x2, None, None, p['ffn_g'], p['ffn_u'], p['ffn_d'], p['ln2_g'], p['ln2_b'], tm, routed=False)
    return x2.reshape(b, t, D_MODEL), (new_shift, new_wkv, new_conv, new_c, new_n, new_m)


def _odd_layer(x, past_k, past_v, past_ki, p):
    b, t, _ = x.shape
    m = b * t
    tm = _pick(m, (512, 256, 128, 64, 32, 16, 8))
    u = _matmul(x.reshape(m, D_MODEL), p['w_in'], tm, OD_COLS_PAD).reshape(b, t, OD_COLS_PAD)
    k_new = u[:, :, OD_K:OD_K + 256]
    v_new = u[:, :, OD_V:OD_V + 256]
    ki_new = u[:, :, OD_KI:OD_KI + IDX_DIM]
    if past_k is None:
        keys_k, keys_v, keys_i = k_new, v_new, ki_new
        causal, limit, tq = True, 0, _pick(t, (2 * CHUNK, CHUNK))
        top_k = min(TOPK_MAX, t // 4)
    else:
        keys_k = jnp.concatenate([past_k.reshape(b, -1, 256), k_new], axis=1)
        keys_v = jnp.concatenate([past_v.reshape(b, -1, 256), v_new], axis=1)
        keys_i = jnp.concatenate([past_ki, ki_new], axis=1)
        limit = keys_k.shape[1]
        causal, tq = False, t
        top_k = min(TOPK_MAX, limit // 4)
    s_pad = -keys_k.shape[1] % LANES
    pad_s = lambda z: jnp.pad(z, ((0, 0), (0, s_pad), (0, 0))) if s_pad else z
    kt = pad_s(keys_k).transpose(0, 2, 1).astype(BF16)
    vv = pad_s(keys_v).astype(BF16).reshape(b, -1, C_KV_HEADS, C_HEAD_DIM)
    vv = jnp.concatenate([vv, jnp.ones_like(vv)], axis=-1).reshape(b, -1, 2 * C_KV_HEADS * C_HEAD_DIM)
    kit = pad_s(keys_i).transpose(0, 2, 1).astype(BF16)
    att = _dsa(u, kt, vv, kit, tq, top_k, causal, limit)
    x2 = _proj_ln(att.reshape(m, C_WIDTH), x.reshape(m, D_MODEL), p['w_out'], p['ln1_g'], p['ln1_b'],
                  _pick(m, (512, 256, 128, 64, 32, 16, 8)))
    x2 = _experts(x2, p['router_w'], p['router_b'], p['e_gate'], p['e_up'], p['e_down'],
                  p['ln2_g'], p['ln2_b'], _pick(m, (1024, 512, 256, 128)), routed=True)
    st = (k_new.reshape(b, t, C_KV_HEADS, C_HEAD_DIM), v_new.reshape(b, t, C_KV_HEADS, C_HEAD_DIM), ki_new)
    return x2.reshape(b, t, D_MODEL), st


def _trunk(x, shift0, wkv0, conv0, c0, n0, m0, past_k, past_v, past_ki, ev, od):
    even_out = [[] for _ in range(6)]
    odd_out = [[] for _ in range(3)]
    for layer in range(DEPTH):
        i = layer // 2
        if layer % 2 == 0:
            x, st = _even_layer(x, shift0[i], wkv0[i], conv0[i], c0[i], n0[i], m0[i], ev[i])
            for acc, val in zip(even_out, st):
                acc.append(val)
        else:
            if past_k is None:
                x, st = _odd_layer(x, None, None, None, od[i])
            else:
                x, st = _odd_layer(x, past_k[i], past_v[i], past_ki[i], od[i])
            for acc, val in zip(odd_out, st):
                acc.append(val)
    return x, [jnp.stack(v) for v in even_out], [jnp.stack(v) for v in odd_out]


def kernel(x_prompt, x_sample, state_shift, state_wkv, state_conv, state_c, state_n, state_m,
           cache_k, cache_v, cache_idx_k,
           ev_w_in, a_mu, a_w0, a_w2, a_a0, a_a2, a_g2, a_k_k, a_k_a, a_r_k, a_ln_g, a_ln_b,
           b_conv_w, b_conv_b, b_i_bias, b_f_bias, b_ln_g, b_ln_b, ev_w_out, ev_ln1_g, ev_ln1_b,
           ffn_w_gate, ffn_w_up, ffn_w_down, ev_ln2_g, ev_ln2_b,
           od_w_in, od_w_out, od_ln1_g, od_ln1_b, moe_w_router, moe_b_router,
           moe_w_gate, moe_w_up, moe_w_down, od_ln2_g, od_ln2_b):
    ew = {'w_in': ev_w_in, 'mu': a_mu, 'w0': a_w0, 'w2': a_w2, 'a0': a_a0, 'a2': a_a2, 'g2': a_g2,
          'k_k': a_k_k, 'k_a': a_k_a, 'r_k': a_r_k, 'a_ln_g': a_ln_g, 'a_ln_b': a_ln_b,
          'conv_w': b_conv_w, 'conv_b': b_conv_b, 'b_i': b_i_bias, 'b_f': b_f_bias,
          'b_ln_g': b_ln_g, 'b_ln_b': b_ln_b, 'w_out': ev_w_out, 'ln1_g': ev_ln1_g, 'ln1_b': ev_ln1_b,
          'ffn_gate': ffn_w_gate, 'ffn_up': ffn_w_up, 'ffn_down': ffn_w_down,
          'ln2_g': ev_ln2_g, 'ln2_b': ev_ln2_b}
    ow = {'w_in': od_w_in, 'w_out': od_w_out, 'ln1_g': od_ln1_g, 'ln1_b': od_ln1_b,
          'router': moe_w_router, 'router_b': moe_b_router, 'e_gate': moe_w_gate, 'e_up': moe_w_up,
          'e_down': moe_w_down, 'ln2_g': od_ln2_g, 'ln2_b': od_ln2_b}
    n_even, n_odd = ev_w_in.shape[0], od_w_in.shape[0]
    ev = [_prep_even({k: v[i] for k, v in ew.items()}) for i in range(n_even)]
    od = [_prep_odd({k: v[i] for k, v in ow.items()}) for i in range(n_odd)]

    bp = x_prompt.shape[0]
    z = functools.partial(jnp.zeros, dtype=F32)
    y_p, ep, op = _trunk(x_prompt,
                         z((n_even, bp, 1, A_COLS)), z((n_even, bp, A_HEADS, A_HEAD_DIM, A_HEAD_DIM)),
                         z((n_even, bp, 3, 2 * B_QK_WIDTH)), z((n_even, bp, B_HEADS, B_V_DIM, B_QK_DIM)),
                         z((n_even, bp, B_HEADS, B_QK_DIM)), z((n_even, bp, B_HEADS)),
                         None, None, None, ev, od)
    y_s, es, os_ = _trunk(x_sample, state_shift, state_wkv, state_conv, state_c, state_n, state_m,
                          cache_k, cache_v, cache_idx_k, ev, od)
    return (y_p, y_s,
            ep[0], es[0], ep[1], es[1], ep[2], es[2], ep[3], es[3], ep[4], es[4], ep[5], es[5],
            op[0], os_[0], op[1], os_[1], op[2], os_[2])
```

```python
import functools
import math

import jax
import jax.numpy as jnp
from jax import lax
from jax.experimental import pallas as pl
from jax.experimental.pallas import tpu as pltpu

F32 = jnp.float32
BF16 = jnp.bfloat16
I32 = jnp.int32

D_MODEL = 1024
DEPTH = 4
CHUNK = 64
A_HEADS, A_HEAD_DIM, A_WIDTH = 16, 64, 1024
A_LORA = 256
A_COLS = 3 * A_WIDTH + A_LORA
B_HEADS, B_QK_DIM, B_V_DIM = 8, 64, 128
B_QK_WIDTH, B_V_WIDTH = 512, 1024
C_HEADS, C_KV_HEADS, C_HEAD_DIM, C_GROUP = 8, 2, 128, 4
C_WIDTH = 1024
IDX_HEADS, IDX_DIM = 8, 64
TOPK_MAX = 256
N_EXPERTS = 8
ALPHA = (2 * DEPTH) ** 0.25
LN_EPS = 1e-5

LANES = 128
SUBLANES = 8
VMEM_LIMIT = 56 * 1024 * 1024

EV_R, EV_K, EV_V, EV_QK, EV_VB, EV_O, EV_LORA, EV_GATE = 0, 1024, 2048, 3072, 4096, 5120, 6144, 6400
EV_COLS_PAD = 6528
OD_Q, OD_K, OD_V, OD_QI, OD_KI = 0, 1024, 1280, 1536, 2048
OD_COLS_PAD = 2176

INT_MIN = -2 ** 31


def _cparams(*sem):
    return pltpu.CompilerParams(dimension_semantics=sem, vmem_limit_bytes=VMEM_LIMIT)


def _layer_norm(z, g, b):
    mu = jnp.mean(z, axis=-1, keepdims=True)
    d = z - mu
    var = jnp.mean(d * d, axis=-1, keepdims=True)
    return d * lax.rsqrt(var + LN_EPS) * g + b


def _hilo_dot(x, e):
    hi = x.astype(BF16)
    lo = (x - hi.astype(F32)).astype(BF16)
    return jnp.dot(hi, e, preferred_element_type=F32) + jnp.dot(lo, e, preferred_element_type=F32)


def _segsum(x, e):
    return _hilo_dot(_hilo_dot(x, e[0]), e[1])


def _mm_kernel(x_ref, w_ref, o_ref):
    o_ref[...] = jnp.dot(x_ref[...].astype(BF16), w_ref[...], preferred_element_type=F32)


def _matmul(x, w, tm, tn):
    m, k = x.shape
    n = w.shape[1]
    return pl.pallas_call(
        _mm_kernel,
        grid=(n // tn, m // tm),
        in_specs=[pl.BlockSpec((tm, k), lambda j, i: (i, 0)),
                  pl.BlockSpec((k, tn), lambda j, i: (0, j))],
        out_specs=pl.BlockSpec((tm, tn), lambda j, i: (i, j)),
        out_shape=jax.ShapeDtypeStruct((m, n), F32),
        compiler_params=_cparams("parallel", "parallel"),
        name="proj_in",
    )(x, w)


def _rwkv_pre_kernel(ur, uk, uv, ul, s0m, s0l, mum, mul_, w0, a0, kk_, ka_, rk_, w2, a2, g2, er_ref, eb_ref,
                     o_seq, o_g, o_bonus, prev_m, prev_l):
    @pl.when(pl.program_id(1) == 0)
    def _():
        prev_m[...] = s0m[0]
        prev_l[...] = s0l[0]

    tm = ur.shape[1]
    row = lax.broadcasted_iota(I32, (tm, 1), 0)

    def lerp(x, prev, mu):
        shifted = jnp.where(row == 0, prev, pltpu.roll(x, 1, 0))
        return x + (shifted - x) * mu

    r_raw, k_raw, v_raw, l_raw = ur[0], uk[0], uv[0], ul[0]
    r = lerp(r_raw, prev_m[:, 0:1024], mum[:, 0:1024])
    k = lerp(k_raw, prev_m[:, 1024:2048], mum[:, 1024:2048])
    v = lerp(v_raw, prev_m[:, 2048:3072], mum[:, 2048:3072])
    lo = lerp(l_raw, prev_l[...], mul_[...])
    prev_m[:, 0:1024] = r_raw[tm - 1:tm, :]
    prev_m[:, 1024:2048] = k_raw[tm - 1:tm, :]
    prev_m[:, 2048:3072] = v_raw[tm - 1:tm, :]
    prev_l[...] = l_raw[tm - 1:tm, :]

    zw = w0[...] + jnp.dot(jnp.tanh(lo).astype(BF16), w2[...], preferred_element_type=F32)
    decay = jnp.exp(-(math.exp(-0.5) * jax.nn.sigmoid(zw)))
    a = jax.nn.sigmoid(a0[...] + jnp.dot(lo.astype(BF16), a2[...], preferred_element_type=F32))
    g = jnp.dot(jax.nn.sigmoid(lo).astype(BF16), g2[...], preferred_element_type=F32)

    e = (er_ref[...], eb_ref[...])
    kk = k * kk_[...]
    kk = kk * lax.rsqrt(jnp.maximum(_segsum(kk * kk, e), 1e-24))
    k2 = k * (1.0 + (a - 1.0) * ka_[...])
    bonus = _segsum(r * k2 * rk_[...], e) * v

    rows3 = lambda z: z.reshape(tm, A_HEADS // 2, LANES)
    o_seq[WKV_SRC_R] = rows3(r)
    o_seq[WKV_SRC_W] = rows3(decay)
    o_seq[WKV_SRC_K] = rows3(k2)
    o_seq[WKV_SRC_V] = rows3(v)
    o_seq[WKV_SRC_A] = rows3(-kk)
    o_seq[WKV_SRC_B] = rows3(kk * a)
    o_g[...] = g
    o_bonus[...] = bonus


def _rwkv_pre(u, s0_main, s0_lora, p, tm):
    b, t, _ = u.shape
    blk = lambda w, c: pl.BlockSpec((1, tm, w), lambda i, j, c=c: (i, j, c))
    full = lambda arr: pl.BlockSpec(arr.shape, lambda i, j: (0,) * arr.ndim)
    params = [p['mu_main'], p['mu_lora'], p['w0'], p['a0'], p['k_k'], p['k_a'], p['r_k'],
              p['w2p'], p['a2p'], p['g2p'], p['seg_red'], p['seg_bc']]
    out = jax.ShapeDtypeStruct((t, b * A_WIDTH), F32)
    return pl.pallas_call(
        _rwkv_pre_kernel,
        grid=(b, t // tm),
        in_specs=[blk(1024, EV_R // 1024), blk(1024, EV_K // 1024), blk(1024, EV_V // 1024),
                  blk(A_LORA, EV_LORA // A_LORA),
                  pl.BlockSpec((1, 1, 3072), lambda i, j: (i, 0, 0)),
                  pl.BlockSpec((1, 1, A_LORA), lambda i, j: (i, 0, 0))] + [full(a) for a in params],
        out_specs=[pl.BlockSpec((WKV_N_SRC, tm, A_HEADS // 2, LANES), lambda i, j: (0, j, i, 0)),
                   pl.BlockSpec((tm, A_WIDTH), lambda i, j: (j, i)),
                   pl.BlockSpec((tm, A_WIDTH), lambda i, j: (j, i))],
        out_shape=[jax.ShapeDtypeStruct((WKV_N_SRC, t, b * A_HEADS // 2, LANES), F32), out, out],
        scratch_shapes=[pltpu.VMEM((1, 3072), F32), pltpu.VMEM((1, A_LORA), F32)],
        compiler_params=_cparams("parallel", "arbitrary"),
        name="rwkv_pre",
    )(u, u, u, u, s0_main, s0_lora, *params)


WKV_N_SRC = 6
WKV_SRC_R, WKV_SRC_W, WKV_SRC_K, WKV_SRC_V, WKV_SRC_A, WKV_SRC_B = range(WKV_N_SRC)
WKV_VGROUP = 4
WKV_KCHUNK = 32


def _wkv_kernel(seq_ref, s0_ref, y_ref, s_ref, tr, sa_buf, ybuf):
    @pl.when(pl.program_id(1) == 0)
    def _():
        s_ref[...] = s0_ref[...]

    n_t = seq_ref.shape[1]
    d = A_HEAD_DIM
    vspan = WKV_VGROUP * SUBLANES
    groups = [(p, g) for p in range(2) for g in range(d // vspan)]
    n_kc = d // WKV_KCHUNK

    def vrows(g, j):
        return slice(g * vspan + j * SUBLANES, g * vspan + (j + 1) * SUBLANES)

    def transpose_tile(i, tok):
        tr[i, tok] = jnp.transpose(seq_ref[i, tok])

    for i in range(WKV_N_SRC):
        transpose_tile(i, 0)
    transpose_tile(WKV_SRC_A, min(1, n_t - 1))

    for p, g in groups:
        acc = [jnp.zeros((SUBLANES, LANES), F32) for _ in range(WKV_VGROUP)]
        for kx in range(d):
            ab = tr[WKV_SRC_A, 0, p * d + kx:p * d + kx + 1, :]
            for j in range(WKV_VGROUP):
                acc[j] = acc[j] + s_ref[p, kx, vrows(g, j), :] * ab
        for j in range(WKV_VGROUP):
            sa_buf[p, vrows(g, j), :] = acc[j]

    def step(t, carry):
        tn = jnp.minimum(t + 1, n_t - 1)
        for gi, (p, g) in enumerate(groups):
            vv = [tr[WKV_SRC_V, t, p * d + g * vspan + j * SUBLANES:p * d + g * vspan + (j + 1) * SUBLANES, :]
                  for j in range(WKV_VGROUP)]
            sa = [sa_buf[p, vrows(g, j), :] for j in range(WKV_VGROUP)]
            zero = tuple(jnp.zeros((SUBLANES, LANES), F32) for _ in range(2 * WKV_VGROUP))

            def key_chunk(kc, accs, gi=gi, p=p, g=g, vv=vv, sa=sa):
                yacc, san = list(accs[:WKV_VGROUP]), list(accs[WKV_VGROUP:])
                for i in range(WKV_KCHUNK):
                    kx = kc * WKV_KCHUNK + i
                    row = pl.ds(p * d + kx, 1)
                    wb = tr[WKV_SRC_W, t, row, :]
                    bb = tr[WKV_SRC_B, t, row, :]
                    kb = tr[WKV_SRC_K, t, row, :]
                    rb = tr[WKV_SRC_R, t, row, :]
                    an = tr[WKV_SRC_A, tn, row, :]
                    for j in range(WKV_VGROUP):
                        n = s_ref[p, kx, vrows(g, j), :] * wb + sa[j] * bb + vv[j] * kb
                        s_ref[p, kx, vrows(g, j), :] = n
                        yacc[j] = yacc[j] + n * rb
                        san[j] = san[j] + n * an
                tile = jnp.minimum(gi * n_kc + kc, WKV_N_SRC - 1)
                ahead = jnp.where(tile == WKV_SRC_A, 2, 1)
                transpose_tile(tile, jnp.minimum(t + ahead, n_t - 1))
                return tuple(yacc) + tuple(san)

            accs = lax.fori_loop(0, n_kc, key_chunk, zero)
            yacc, san = accs[:WKV_VGROUP], accs[WKV_VGROUP:]
            for j in range(WKV_VGROUP):
                ybuf[t, p * d + g * vspan + j * SUBLANES:p * d + g * vspan + (j + 1) * SUBLANES, :] = yacc[j]
                sa_buf[p, vrows(g, j), :] = san[j]
        return carry

    lax.fori_loop(0, n_t, step, 0)

    def transpose_out(t, carry):
        y_ref[t] = jnp.transpose(ybuf[t])
        return carry

    lax.fori_loop(0, n_t, transpose_out, 0, unroll=4)


def _wkv_scan(seq, s0, tt):
    _, t, rows, _ = seq.shape
    d = A_HEAD_DIM
    st = pl.BlockSpec((2, d, d, LANES), lambda g, i: (0, 0, 0, g))
    return pl.pallas_call(
        _wkv_kernel,
        grid=(rows // LANES, t // tt),
        in_specs=[pl.BlockSpec((WKV_N_SRC, tt, LANES, LANES), lambda g, i: (0, i, g, 0)), st],
        out_specs=[pl.BlockSpec((tt, LANES, LANES), lambda g, i: (i, g, 0)), st],
        out_shape=[jax.ShapeDtypeStruct((t, rows, LANES), F32), jax.ShapeDtypeStruct((2, d, d, rows), F32)],
        scratch_shapes=[pltpu.VMEM((WKV_N_SRC, tt, LANES, LANES), F32), pltpu.VMEM((2, d, LANES), F32),
                        pltpu.VMEM((tt, LANES, LANES), F32)],
        compiler_params=_cparams("parallel", "arbitrary"),
        name="wkv_scan",
    )(seq, s0)


def _mlstm_kernel(uqk, uv, uo, ug, conv0, ct0, m0, cw, cb, gb, lng, lnb,
                  o_y, o_ct, o_m, ct, m_scr, carry, *, t_valid):
    tb = pl.program_id(1)

    @pl.when(tb == 0)
    def _():
        ct[...] = ct0[...]
        m_scr[...] = m0[...]
        carry[...] = conv0[...]

    for bi in range(uqk.shape[0]):
        one = pl.ds(bi, 1)
        _mlstm_chunk(tb, uqk.at[one], uv.at[one], uo.at[one], ug.at[one], cw, cb, gb, lng, lnb,
                     o_y.at[one], ct.at[bi], m_scr.at[bi], carry.at[bi], t_valid=t_valid)

    @pl.when(tb == pl.num_programs(1) - 1)
    def _():
        o_ct[...] = ct[...]
        o_m[...] = m_scr[...]


def _mlstm_chunk(tb, uqk, uv, uo, ug, cw, cb, gb, lng, lnb, o_y, ct, m_scr, carry, *, t_valid):
    L = uqk.shape[1]
    row = lax.broadcasted_iota(I32, (L, 1), 0)
    x = uqk[0]

    def shifted(j):
        out = pltpu.roll(x, j, 0)
        for i in range(j):
            out = jnp.where(row == i, carry[3 + i - j:4 + i - j, :], out)
        return out

    conv = x * cw[3:4, :] + shifted(1) * cw[2:3, :] + shifted(2) * cw[1:2, :] + shifted(3) * cw[0:1, :] + cb[...]
    carry[...] = x[L - 3:L, :]
    qk = conv * jax.nn.sigmoid(conv)
    q_all = qk[:, 0:B_QK_WIDTH]
    k_t = jnp.transpose(qk[:, B_QK_WIDTH:2 * B_QK_WIDTH] * (B_QK_DIM ** -0.5))

    lane = lax.broadcasted_iota(I32, (L, LANES), 1)
    gz = ug[0] + gb[...]
    lg = jnp.where(lane < B_HEADS, gz, jnp.minimum(gz, 0.0) - jnp.log(1.0 + jnp.exp(-jnp.abs(gz))))
    valid = (row + tb * L) < t_valid
    lg = jnp.where(valid, lg, jnp.where(lane < B_HEADS, -jnp.inf, 0.0))
    lg_t = jnp.transpose(lg)
    ti = lax.broadcasted_iota(I32, (L, L), 0)
    si = lax.broadcasted_iota(I32, (L, L), 1)
    tril = ti >= si
    tril_f = tril.astype(F32)
    triu_f = (ti <= si).astype(F32)
    lg_fin = jnp.where(lane < B_HEADS, 0.0, lg)
    bcum = jnp.dot(tril_f, lg_fin, preferred_element_type=F32, precision=lax.Precision.HIGHEST)
    bcum_t = jnp.dot(jnp.transpose(lg_fin), triu_f, preferred_element_type=F32,
                     precision=lax.Precision.HIGHEST)

    head_q = lax.shift_right_logical(lax.broadcasted_iota(I32, (L, B_QK_WIDTH), 1), 6)
    lane1 = lax.broadcasted_iota(I32, (1, LANES), 1)
    ones_col = (lax.broadcasted_iota(I32, (L, LANES), 1) == 0).astype(F32)
    m_vec = m_scr[...]
    m_new_vec = m_vec
    lane_ct = lax.broadcasted_iota(I32, (B_QK_DIM, 2 * LANES), 1)
    ct_old = ct[...]
    ct_old_bf = ct_old.astype(BF16)
    k_bf = k_t.astype(BF16)
    vv = uv[0]
    oo = uo[0]
    q_stack = jnp.concatenate([jnp.where(head_q == h, q_all, 0.0) for h in range(B_HEADS)], axis=0).astype(BF16)
    qk_all = jnp.dot(q_stack, k_bf, preferred_element_type=F32)
    cross_all = jnp.dot(q_stack, ct_old_bf, preferred_element_type=F32)
    for h in range(B_HEADS):
        hs = slice(h * B_QK_DIM, (h + 1) * B_QK_DIM)
        vs = slice(h * B_V_DIM, (h + 1) * B_V_DIM)
        bc_col = bcum[:, B_HEADS + h:B_HEADS + h + 1]
        bc_row = bcum_t[B_HEADS + h:B_HEADS + h + 1, :]
        ic_row = lg_t[h:h + 1, :]
        gtot = bc_row[:, L - 1:L]
        m_prev = m_vec[:, h:h + 1]
        dmat = jnp.where(tril, bc_col - bc_row + ic_row, -jnp.inf)
        inter = bc_col + m_prev
        m_t = jnp.maximum(inter, jnp.max(dmat, axis=1, keepdims=True))
        w_intra = jnp.exp(dmat - m_t)
        w_inter = jnp.exp(inter - m_t)
        s = qk_all[h * L:(h + 1) * L, :] * w_intra
        v_ext = jnp.concatenate([vv[:, vs], ones_col], axis=1).astype(BF16)
        intra = jnp.dot(s.astype(BF16), v_ext, preferred_element_type=F32)
        cross = cross_all[h * L:(h + 1) * L, :]
        num = intra[:, 0:B_V_DIM] + w_inter * cross[:, 0:B_V_DIM]
        den = jnp.sum(s, axis=1, keepdims=True) + w_inter * cross[:, B_V_DIM:B_V_DIM + 1]
        hh = num / jnp.maximum(jnp.abs(den), jnp.exp(-m_t))
        mu = jnp.mean(hh, axis=-1, keepdims=True)
        dd = hh - mu
        var = jnp.mean(dd * dd, axis=-1, keepdims=True)
        yn = dd * lax.rsqrt(var + LN_EPS) * lng[:, vs] + lnb[:, vs]
        o_y[0, :, vs] = jax.nn.sigmoid(oo[:, vs]) * yn
        lw = gtot - bc_row + ic_row
        m_new = jnp.maximum(gtot + m_prev, jnp.max(lw, axis=1, keepdims=True))
        w_s = jnp.exp(lw - m_new)
        dec = jnp.exp(gtot + m_prev - m_new)
        kw = k_t[hs, :] * w_s
        upd = jnp.dot(kw.astype(BF16), v_ext, preferred_element_type=F32)
        upd = jnp.where(lane_ct == B_V_DIM, jnp.sum(kw, axis=1, keepdims=True), upd)
        ct[hs, :] = dec * ct_old[hs, :] + upd
        m_new_vec = jnp.where(lane1 == h, m_new, m_new_vec)
    m_scr[...] = m_new_vec


MLSTM_ROWS = 1


def _mlstm(u, conv0, ct0, m0, p, L, t_valid):
    b, t, _ = u.shape
    nb = MLSTM_ROWS if b % MLSTM_ROWS == 0 else 1
    blk = lambda w, c: pl.BlockSpec((nb, L, w), lambda i, j, c=c: (i, j, c))
    full = lambda arr: pl.BlockSpec(arr.shape, lambda i, j: (0,) * arr.ndim)
    per_b = lambda arr: pl.BlockSpec((nb,) + arr.shape[1:], lambda i, j: (i, 0, 0))
    params = [p['conv_w'], p['conv_b'], p['gate_b'], p['b_ln_g'], p['b_ln_b']]
    return pl.pallas_call(
        functools.partial(_mlstm_kernel, t_valid=t_valid),
        grid=(b // nb, t // L),
        in_specs=[blk(1024, EV_QK // 1024), blk(1024, EV_VB // 1024), blk(1024, EV_O // 1024),
                  blk(LANES, EV_GATE // LANES), per_b(conv0), per_b(ct0), per_b(m0)] + [full(a) for a in params],
        out_specs=[pl.BlockSpec((nb, L, B_V_WIDTH), lambda i, j: (i, j, 0)),
                   pl.BlockSpec((nb, B_QK_WIDTH, 2 * LANES), lambda i, j: (i, 0, 0)),
                   pl.BlockSpec((nb, 1, LANES), lambda i, j: (i, 0, 0))],
        out_shape=[jax.ShapeDtypeStruct((b, t, B_V_WIDTH), F32),
                   jax.ShapeDtypeStruct((b, B_QK_WIDTH, 2 * LANES), F32),
                   jax.ShapeDtypeStruct((b, 1, LANES), F32)],
        scratch_shapes=[pltpu.VMEM((nb, B_QK_WIDTH, 2 * LANES), F32), pltpu.VMEM((nb, 1, LANES), F32),
                        pltpu.VMEM((nb, 3, 2 * B_QK_WIDTH), F32)],
        compiler_params=_cparams("parallel", "arbitrary"),
        name="mlstm",
    )(u, u, u, u, conv0, ct0, m0, *params)


def _even_out_kernel(y_ref, bonus_ref, g_ref, yb_ref, x_ref, lng, lnb, er_ref, eb_ref, woa, wob, g1, b1, o_ref):
    e = (er_ref[...], eb_ref[...])
    y = y_ref[...].reshape(x_ref.shape)
    mu = _segsum(y, e) * (1.0 / A_HEAD_DIM)
    d = y - mu
    var = _segsum(d * d, e) * (1.0 / A_HEAD_DIM)
    yn = d * lax.rsqrt(var + LN_EPS) * lng[...] + lnb[...]
    ya = (yn + bonus_ref[...]) * g_ref[...]
    mix = (jnp.dot(ya.astype(BF16), woa[...], preferred_element_type=F32)
           + jnp.dot(yb_ref[...].astype(BF16), wob[...], preferred_element_type=F32))
    o_ref[...] = _layer_norm(ALPHA * x_ref[...] + mix, g1[...], b1[...])


def _even_out(y, bonus, g, yb, x, p, b, tm):
    m = x.shape[0]
    nt = m // b // tm
    tmaj = pl.BlockSpec((tm, D_MODEL), lambda i, j: (j, i))
    row = pl.BlockSpec((tm, D_MODEL), lambda i, j: (i * nt + j, 0))
    full = lambda arr: pl.BlockSpec(arr.shape, lambda i, j: (0,) * arr.ndim)
    params = [p['a_ln_g'], p['a_ln_b'], p['seg_red'], p['seg_bc'], p['w_out_a'], p['w_out_b'], p['ln1_g'], p['ln1_b']]
    return pl.pallas_call(
        _even_out_kernel,
        grid=(b, nt),
        in_specs=[pl.BlockSpec((tm, A_HEADS // 2, LANES), lambda i, j: (j, i, 0))] + [tmaj] * 2 + [row] * 2
        + [full(a) for a in params],
        out_specs=row,
        out_shape=jax.ShapeDtypeStruct((m, D_MODEL), F32),
        compiler_params=_cparams("parallel", "parallel"),
        name="even_out",
    )(y, bonus, g, yb, x, *params)


def _proj_ln_kernel(a_ref, x_ref, w_ref, g1, b1, o_ref):
    mix = jnp.dot(a_ref[...].astype(BF16), w_ref[...], preferred_element_type=F32)
    o_ref[...] = _layer_norm(ALPHA * x_ref[...] + mix, g1[...], b1[...])


def _proj_ln(a, x, w, g1, b1, tm):
    m = x.shape[0]
    row = pl.BlockSpec((tm, D_MODEL), lambda i: (i, 0))
    full = lambda arr: pl.BlockSpec(arr.shape, lambda i: (0,) * arr.ndim)
    return pl.pallas_call(
        _proj_ln_kernel,
        grid=(m // tm,),
        in_specs=[row, row, full(w), full(g1), full(b1)],
        out_specs=row,
        out_shape=jax.ShapeDtypeStruct((m, D_MODEL), F32),
        compiler_params=_cparams("parallel"),
        name="proj_ln",
    )(a, x, w, g1, b1)


def _swiglu(xb, wg, wu, wd):
    hg = jnp.dot(xb, wg, preferred_element_type=F32)
    hu = jnp.dot(xb, wu, preferred_element_type=F32)
    hid = (hg * jax.nn.sigmoid(hg)) * hu
    return jnp.dot(hid.astype(BF16), wd, preferred_element_type=F32)


def _ffn_kernel(x_ref, wg, wu, wd, g2, b2, o_ref, xb):
    e = pl.program_id(1)

    @pl.when(e == 0)
    def _():
        xb[...] = x_ref[...].astype(BF16)
        o_ref[...] = jnp.zeros_like(o_ref)

    o_ref[...] += _swiglu(xb[...], wg[0], wu[0], wd[0])

    @pl.when(e == pl.num_programs(1) - 1)
    def _():
        o_ref[...] = _layer_norm(ALPHA * x_ref[...] + o_ref[...], g2[...], b2[...])


MOE_CHUNK = 256


def _moe_kernel(x_ref, wr, br, wg, wu, wd, g2, b2, o_ref, xb, comb, pos_scr, post_scr):
    e = pl.program_id(1)
    tm = x_ref.shape[0]

    @pl.when(e == 0)
    def _():
        x = x_ref[...]
        xb[...] = x.astype(BF16)
        o_ref[...] = jnp.zeros_like(o_ref)
        logits = jnp.dot(x, wr[...], preferred_element_type=F32, precision=lax.Precision.HIGHEST) + br[...]
        lane = lax.broadcasted_iota(I32, logits.shape, 1).astype(F32)
        v1 = jnp.max(logits, axis=-1, keepdims=True)
        i1 = jnp.min(jnp.where(logits == v1, lane, float(LANES)), axis=-1, keepdims=True)
        rest = jnp.where(lane == i1, -jnp.inf, logits)
        v2 = jnp.max(rest, axis=-1, keepdims=True)
        i2 = jnp.min(jnp.where(rest == v2, lane, float(LANES)), axis=-1, keepdims=True)
        e2 = jnp.exp(v2 - v1)
        den = 1.0 + e2
        comb[...] = jnp.where(lane == i1, 1.0 / den, jnp.where(lane == i2, e2 / den, 0.0))
        member = jnp.where(lane == i1, 1.0, jnp.where(lane == i2, 1.0, 0.0))
        earlier = (lax.broadcasted_iota(I32, (tm, tm), 0) > lax.broadcasted_iota(I32, (tm, tm), 1))
        pos = jnp.dot(jnp.where(earlier, 1.0, 0.0).astype(BF16), member.astype(BF16), preferred_element_type=F32)
        posm = jnp.where(member > 0.0, pos, -1.0)
        pos_scr[...] = posm
        post_scr[...] = jnp.transpose(posm)

    sel = lax.broadcasted_iota(I32, (tm, LANES), 1) == e
    gate = jnp.sum(jnp.where(sel, comb[...], 0.0), axis=-1, keepdims=True)
    slot_col = jnp.sum(jnp.where(sel, pos_scr[...], 0.0), axis=-1, keepdims=True)
    slot_row = post_scr[pl.ds(e, 1), :]
    count = jnp.max(slot_row) + 1.0

    def run_slots(base, n):
        rows = lax.broadcasted_iota(I32, (n, 1), 0).astype(F32) + float(base)
        gather = jnp.where(slot_row == rows, 1.0, 0.0).astype(BF16)
        xg = jnp.dot(gather, xb[...], preferred_element_type=F32).astype(BF16)
        y = _swiglu(xg, wg[0], wu[0], wd[0])
        cols = lax.broadcasted_iota(I32, (1, n), 1).astype(F32) + float(base)
        scatter = jnp.where(slot_col == cols, 1.0, 0.0).astype(BF16)
        o_ref[...] += gate * jnp.dot(scatter, y.astype(BF16), preferred_element_type=F32)

    chunk = min(MOE_CHUNK, tm)
    half = chunk // 2
    for c in range(tm // chunk):
        lo = c * chunk

        @pl.when(count > lo + half)
        def _(lo=lo):
            run_slots(lo, chunk)

        @pl.when(jnp.logical_and(count > lo, count <= lo + half))
        def _(lo=lo):
            run_slots(lo, half)

    @pl.when(e == pl.num_programs(1) - 1)
    def _():
        o_ref[...] = _layer_norm(ALPHA * x_ref[...] + o_ref[...], g2[...], b2[...])


def _experts(x, wr, br, wg, wu, wd, g2, b2, tm, routed):
    m = x.shape[0]
    ne, _, dff = wg.shape
    row = pl.BlockSpec((tm, D_MODEL), lambda i, e: (i, 0))
    full = lambda arr: pl.BlockSpec(arr.shape, lambda i, e: (0,) * arr.ndim)
    wspecs = [pl.BlockSpec((1, D_MODEL, dff), lambda i, e: (e, 0, 0)),
              pl.BlockSpec((1, D_MODEL, dff), lambda i, e: (e, 0, 0)),
              pl.BlockSpec((1, dff, D_MODEL), lambda i, e: (e, 0, 0))]
    common = dict(grid=(m // tm, ne), out_specs=row, out_shape=jax.ShapeDtypeStruct((m, D_MODEL), F32),
                  compiler_params=_cparams("parallel", "arbitrary"))
    if not routed:
        return pl.pallas_call(
            _ffn_kernel, in_specs=[row] + wspecs + [full(g2), full(b2)],
            scratch_shapes=[pltpu.VMEM((tm, D_MODEL), BF16)], name="experts_dense", **common,
        )(x, wg, wu, wd, g2, b2)
    return pl.pallas_call(
        _moe_kernel, in_specs=[row, full(wr), full(br)] + wspecs + [full(g2), full(b2)],
        scratch_shapes=[pltpu.VMEM((tm, D_MODEL), BF16), pltpu.VMEM((tm, LANES), F32),
                        pltpu.VMEM((tm, LANES), F32), pltpu.VMEM((LANES, tm), F32)],
        name="experts_routed", **common,
    )(x, wr, br, wg, wu, wd, g2, b2)


DSA_COARSE_BITS = 28


def _dsa_kernel(q_ref, qi_ref, wi_ref, kt_ref, v_ref, kit_ref, *rest, top_k, causal, limit_const, blk_off, has_prev):
    o_ref, key_scr, jcut_scr, thr_scr = rest[1:] if has_prev else rest
    tq = q_ref.shape[1]
    s_len = kt_ref.shape[2]
    if causal:
        row_chunk = lax.shift_right_logical(lax.broadcasted_iota(I32, (tq, 1), 0), CHUNK.bit_length() - 1)
        chunk = (pl.program_id(1) + blk_off) * (tq // CHUNK) + row_chunk
        limit = (chunk + 1) * CHUNK
        kth = jnp.minimum(top_k, limit).astype(F32)
    else:
        limit = limit_const
        kth = jnp.full((tq, 1), min(top_k, limit_const), F32)
    idx = lax.broadcasted_iota(I32, (tq, s_len), 1)

    kit = kit_ref[0]
    wi = wi_ref[0] * (IDX_HEADS ** -0.5)
    score = jnp.zeros((tq, s_len), F32)
    for h in range(IDX_HEADS):
        q_h = qi_ref[0, :, h * IDX_DIM:(h + 1) * IDX_DIM].astype(BF16)
        rel = jnp.dot(q_h, kit, preferred_element_type=F32)
        score = score + wi[:, IDX_DIM + h:IDX_DIM + h + 1] * jnp.maximum(rel, 0.0)
    score = jnp.where(score == 0.0, 0.0, score)
    bits = pltpu.bitcast(score, I32)
    key = jnp.where(bits < 0, bits ^ 0x7FFFFFFF, bits)
    key_scr[...] = jnp.where(idx < limit, key, INT_MIN)

    n_grp = 2 if tq % (4 * SUBLANES) == 0 else 1
    rg = tq // n_grp
    rows_of = lambda g: slice(g * rg, (g + 1) * rg)

    def count_ge(thrs):
        return [jnp.sum(jnp.where(key_scr[rows_of(g), :] >= thrs[g], 1.0, 0.0), axis=1, keepdims=True)
                for g in range(n_grp)]

    def bit_step(i, thrs):
        bit = lax.shift_left(jnp.int32(1), 31 - i)
        cands = [t + bit for t in thrs]
        cnts = count_ge(cands)
        return tuple(jnp.where(cnts[g] >= kth[rows_of(g)], cands[g], thrs[g]) for g in range(n_grp))

    thrs = lax.fori_loop(0, DSA_COARSE_BITS, bit_step,
                         tuple(jnp.full((rg, 1), INT_MIN, I32) for _ in range(n_grp)), unroll=4)
    thr_scr[...] = jnp.concatenate(thrs, axis=0)
    n_coarse = jnp.concatenate(count_ge(thrs), axis=0)

    @pl.when(jnp.max(n_coarse - kth) > 0.0)
    def _():
        fine = lax.fori_loop(DSA_COARSE_BITS, 32, bit_step, tuple(thr_scr[rows_of(g), :] for g in range(n_grp)),
                             unroll=4)
        thr_scr[...] = jnp.concatenate(fine, axis=0)

    thr = thr_scr[...]
    keyv = key_scr[...]
    n_ge = jnp.sum(jnp.where(keyv >= thr, 1.0, 0.0), axis=1, keepdims=True)
    jcut_scr[...] = jnp.full((tq, 1), s_len, I32)

    @pl.when(jnp.max(n_ge - kth) > 0.0)
    def _():
        need = kth - jnp.sum(jnp.where(key_scr[...] > thr, 1.0, 0.0), axis=1, keepdims=True)
        n_bits = max(1, (s_len - 1).bit_length())

        def idx_step(i, j):
            cand = j + lax.shift_left(jnp.int32(1), n_bits - 1 - i)
            cnt = jnp.sum(jnp.where(key_scr[...] == thr, jnp.where(idx < cand, 1.0, 0.0), 0.0),
                          axis=1, keepdims=True)
            return jnp.where(cnt < need, cand, j)

        jcut_scr[...] = lax.fori_loop(0, n_bits, idx_step, jnp.zeros((tq, 1), I32))

    jcut = jcut_scr[...]
    bias = jnp.where(keyv > thr, 0.0,
                     jnp.where(keyv == thr, jnp.where(idx <= jcut, 0.0, -jnp.inf), -jnp.inf)).astype(BF16)

    q = q_ref[0] * (C_HEAD_DIM ** -0.5)
    for grp in range(C_KV_HEADS):
        kt = kt_ref[0, grp * C_HEAD_DIM:(grp + 1) * C_HEAD_DIM, :]
        v_ext = v_ref[0, :, grp * 2 * C_HEAD_DIM:(grp + 1) * 2 * C_HEAD_DIM]
        hsl = [slice((grp * C_GROUP + hg) * C_HEAD_DIM, (grp * C_GROUP + hg + 1) * C_HEAD_DIM)
               for hg in range(C_GROUP)]
        qg = jnp.concatenate([q[:, sl] for sl in hsl], axis=0).astype(BF16)
        lg = jnp.dot(qg, kt, preferred_element_type=F32).astype(BF16)
        ps = []
        for hg in range(C_GROUP):
            logits = lg[hg * tq:(hg + 1) * tq, :] + bias
            m_tile = functools.reduce(jnp.maximum, [logits[:, i * LANES:(i + 1) * LANES]
                                                    for i in range(s_len // LANES)])
            mx = jnp.max(m_tile.astype(F32), axis=1, keepdims=True).astype(BF16)
            ps.append(jnp.exp(logits - mx))
        out = jnp.dot(jnp.concatenate(ps, axis=0), v_ext, preferred_element_type=F32)
        for hg in range(C_GROUP):
            rows = slice(hg * tq, (hg + 1) * tq)
            o_ref[0, :, hsl[hg]] = out[rows, 0:C_HEAD_DIM] / out[rows, C_HEAD_DIM:C_HEAD_DIM + 1]


DSA_KEY_STEP = 512
DSA_WIDE_Q_KEYS = 3584


def _dsa_call(u, kt, v, kit, prev, tq, top_k, causal, limit_const, blk_off, n_blk, s_len):
    b, t, _ = u.shape
    in_specs = [pl.BlockSpec((1, tq, C_WIDTH), lambda i, j: (i, j + blk_off, OD_Q // C_WIDTH)),
                pl.BlockSpec((1, tq, IDX_HEADS * IDX_DIM),
                             lambda i, j: (i, j + blk_off, OD_QI // (IDX_HEADS * IDX_DIM))),
                pl.BlockSpec((1, tq, LANES), lambda i, j: (i, j + blk_off, OD_KI // LANES)),
                pl.BlockSpec((1, 2 * C_HEAD_DIM, s_len), lambda i, j: (i, 0, 0)),
                pl.BlockSpec((1, s_len, 2 * C_KV_HEADS * C_HEAD_DIM), lambda i, j: (i, 0, 0)),
                pl.BlockSpec((1, IDX_DIM, s_len), lambda i, j: (i, 0, 0))]
    args = [u, u, u, kt, v, kit]
    aliases = {}
    if prev is not None:
        in_specs.append(pl.BlockSpec(memory_space=pl.ANY))
        args.append(prev)
        aliases = {len(args) - 1: 0}
    return pl.pallas_call(
        functools.partial(_dsa_kernel, top_k=top_k, causal=causal, limit_const=limit_const, blk_off=blk_off,
                          has_prev=prev is not None),
        grid=(b, n_blk),
        in_specs=in_specs,
        out_specs=pl.BlockSpec((1, tq, C_WIDTH), lambda i, j: (i, j + blk_off, 0)),
        out_shape=jax.ShapeDtypeStruct((b, t, C_WIDTH), F32),
        scratch_shapes=[pltpu.VMEM((tq, s_len), I32), pltpu.VMEM((tq, 1), I32), pltpu.VMEM((tq, 1), I32)],
        input_output_aliases=aliases,
        compiler_params=_cparams("parallel", "arbitrary"),
        name="dsa",
    )(*args)


def _dsa(u, kt, v, kit, tq, top_k, causal, limit_const):
    b, t, _ = u.shape
    s_full = kt.shape[2]
    if not causal or s_full % DSA_KEY_STEP:
        return _dsa_call(u, kt, v, kit, None, tq, top_k, causal, limit_const, 0, t // tq, s_full)
    att = None
    for cls in range(s_full // DSA_KEY_STEP):
        s_len = (cls + 1) * DSA_KEY_STEP
        tq_c = 2 * tq if s_len <= DSA_WIDE_Q_KEYS and DSA_KEY_STEP % (2 * tq) == 0 else tq
        per = DSA_KEY_STEP // tq_c
        att = _dsa_call(u, kt, v, kit, att, tq_c, top_k, causal, limit_const, cls * per, per, s_len)
    return att


def _prep_even(w):
    win = w['w_in']
    a, bq = win[:, :A_COLS], win[:, A_COLS:]
    cols = [a[:, 0:3072], bq[:, 0:3072], a[:, 3072:A_COLS], bq[:, 3072:3088],
            jnp.zeros((D_MODEL, EV_COLS_PAD - EV_GATE - 2 * B_HEADS), F32)]
    p = {'w_in': jnp.concatenate(cols, axis=1).astype(BF16)}
    mu = w['mu']
    p['mu_main'] = mu[None, 0:3072]
    p['mu_lora'] = mu[None, 3072:A_COLS]
    row = lambda v: v.reshape(1, -1)
    p['w0'], p['a0'], p['k_k'], p['k_a'] = row(w['w0']), row(w['a0']), row(w['k_k']), row(w['k_a'])
    p['r_k'] = row(w['r_k'])
    z = lambda n: jnp.zeros((n, A_WIDTH), F32)
    p['w2p'] = jnp.concatenate([w['w2'], z(192)], axis=0).astype(BF16)
    p['a2p'] = jnp.concatenate([z(64), w['a2'], z(128)], axis=0).astype(BF16)
    p['g2p'] = jnp.concatenate([z(128), w['g2']], axis=0).astype(BF16)
    seg = jnp.arange(A_WIDTH) // A_HEAD_DIM
    p['seg_red'] = (seg[:, None] == jnp.arange(LANES)[None, :]).astype(BF16)
    p['seg_bc'] = p['seg_red'].T
    p['a_ln_g'], p['a_ln_b'] = row(w['a_ln_g']), row(w['a_ln_b'])
    p['conv_w'], p['conv_b'] = w['conv_w'], row(w['conv_b'])
    p['gate_b'] = jnp.concatenate([w['b_i'], w['b_f'], jnp.zeros((LANES - 2 * B_HEADS,), F32)])[None]
    p['b_ln_g'], p['b_ln_b'] = row(w['b_ln_g']), row(w['b_ln_b'])
    p['w_out_a'] = w['w_out'][:A_WIDTH].astype(BF16)
    p['w_out_b'] = w['w_out'][A_WIDTH:].astype(BF16)
    p['ln1_g'], p['ln1_b'], p['ln2_g'], p['ln2_b'] = row(w['ln1_g']), row(w['ln1_b']), row(w['ln2_g']), row(w['ln2_b'])
    half = w['ffn_gate'].shape[1] // 2
    p['ffn_g'] = jnp.stack([w['ffn_gate'][:, :half], w['ffn_gate'][:, half:]]).astype(BF16)
    p['ffn_u'] = jnp.stack([w['ffn_up'][:, :half], w['ffn_up'][:, half:]]).astype(BF16)
    p['ffn_d'] = jnp.stack([w['ffn_down'][:half], w['ffn_down'][half:]]).astype(BF16)
    return p


def _prep_odd(w):
    row = lambda v: v.reshape(1, -1)
    p = {'w_in': jnp.pad(w['w_in'], ((0, 0), (0, OD_COLS_PAD - w['w_in'].shape[1]))).astype(BF16)}
    p['w_out'] = w['w_out'].astype(BF16)
    p['ln1_g'], p['ln1_b'], p['ln2_g'], p['ln2_b'] = row(w['ln1_g']), row(w['ln1_b']), row(w['ln2_g']), row(w['ln2_b'])
    p['router_w'] = jnp.pad(w['router'], ((0, 0), (0, LANES - N_EXPERTS)))
    p['router_b'] = jnp.concatenate([w['router_b'], jnp.full((LANES - N_EXPERTS,), -1e30, F32)])[None]
    p['e_gate'], p['e_up'], p['e_down'] = w['e_gate'].astype(BF16), w['e_up'].astype(BF16), w['e_down'].astype(BF16)
    return p


def _pick(n, pref):
    for c in pref:
        if n % c == 0:
            return c
    return n


def _even_layer(x, shift0, wkv0, conv0, c0, n0, m0, p):
    b, t, _ = x.shape
    m = b * t
    tm = _pick(m, (512, 256, 128, 64, 32, 16, 8))
    u = _matmul(x.reshape(m, D_MODEL), p['w_in'], _pick(m, (256, 128, 64, 32, 16, 8)), EV_COLS_PAD)
    u = u.reshape(b, t, EV_COLS_PAD)

    new_shift = jnp.concatenate([u[:, t - 1:, 0:3072], u[:, t - 1:, EV_LORA:EV_LORA + A_LORA]], axis=-1)
    new_conv = jnp.concatenate([conv0, u[:, :, EV_QK:EV_QK + 1024]], axis=1)[:, t:]

    s0_main, s0_lora = shift0[:, :, 0:3072], shift0[:, :, 3072:A_COLS]
    tp = _pick(t, (256, 128, 64, 32, 16, 8))
    seq, g, bonus = _rwkv_pre(u, s0_main, s0_lora, p, tp)
    rows = b * A_HEADS // 2
    assert rows % LANES == 0, "batch must be a multiple of 16"
    s0 = wkv0.reshape(b, A_HEADS // 2, 2, A_HEAD_DIM, A_HEAD_DIM).transpose(2, 4, 3, 0, 1)
    s0 = s0.reshape(2, A_HEAD_DIM, A_HEAD_DIM, rows)
    tt = _pick(t, (16, 8))
    y, s_fin = _wkv_scan(seq, s0, tt)
    new_wkv = s_fin.reshape(2, A_HEAD_DIM, A_HEAD_DIM, b, A_HEADS // 2).transpose(3, 4, 0, 2, 1)
    new_wkv = new_wkv.reshape(b, A_HEADS, A_HEAD_DIM, A_HEAD_DIM)

    ct0 = jnp.concatenate([c0.transpose(0, 1, 3, 2).reshape(b, B_QK_WIDTH, B_V_DIM),
                           n0.reshape(b, B_QK_WIDTH, 1),
                           jnp.zeros((b, B_QK_WIDTH, LANES - 1), F32)], axis=-1)
    m0p = jnp.pad(m0, ((0, 0), (0, LANES - B_HEADS)))[:, None, :]
    lm = _pick(t, (256, 128))
    if t % LANES:
        t_pad = -t % LANES
        u_b = jnp.pad(u, ((0, 0), (0, t_pad), (0, 0)))
        lm = LANES
    else:
        u_b = u
    yb, ct, m_out = _mlstm(u_b, conv0, ct0, m0p, p, lm, t)
    yb = yb[:, :t].reshape(m, B_V_WIDTH)
    new_c = ct[:, :, 0:B_V_DIM].reshape(b, B_HEADS, B_QK_DIM, B_V_DIM).transpose(0, 1, 3, 2)
    new_n = ct[:, :, B_V_DIM].reshape(b, B_HEADS, B_QK_DIM)
    new_m = m_out[:, 0, 0:B_HEADS]

    x2 = x.reshape(m, D_MODEL)
    x2 = _even_out(y, bonus, g, yb, x2, p, b, _pick(t, (256, 128, 64, 32, 16, 8)))
    x2 = _experts(x2, None, None, p['ffn_g'], p['ffn_u'], p['ffn_d'], p['ln2_g'], p['ln2_b'], tm, routed=False)
    return x2.reshape(b, t, D_MODEL), (new_shift, new_wkv, new_conv, new_c, new_n, new_m)


def _odd_layer(x, past_k, past_v, past_ki, p):
    b, t, _ = x.shape
    m = b * t
    tm = _pick(m, (512, 256, 128, 64, 32, 16, 8))
    u = _matmul(x.reshape(m, D_MODEL), p['w_in'], tm, OD_COLS_PAD).reshape(b, t, OD_COLS_PAD)
    k_new = u[:, :, OD_K:OD_K + 256]
    v_new = u[:, :, OD_V:OD_V + 256]
    ki_new = u[:, :, OD_KI:OD_KI + IDX_DIM]
    if past_k is None:
        keys_k, keys_v, keys_i = k_new, v_new, ki_new
        causal, limit, tq = True, 0, _pick(t, (2 * CHUNK, CHUNK))
        top_k = min(TOPK_MAX, t // 4)
    else:
        keys_k = jnp.concatenate([past_k.reshape(b, -1, 256), k_new], axis=1)
        keys_v = jnp.concatenate([past_v.reshape(b, -1, 256), v_new], axis=1)
        keys_i = jnp.concatenate([past_ki, ki_new], axis=1)
        limit = keys_k.shape[1]
        causal, tq = False, t
        top_k = min(TOPK_MAX, limit // 4)
    s_pad = -keys_k.shape[1] % LANES
    pad_s = lambda z: jnp.pad(z, ((0, 0), (0, s_pad), (0, 0))) if s_pad else z
    kt = pad_s(keys_k).transpose(0, 2, 1).astype(BF16)
    vv = pad_s(keys_v).astype(BF16).reshape(b, -1, C_KV_HEADS, C_HEAD_DIM)
    vv = jnp.concatenate([vv, jnp.ones_like(vv)], axis=-1).reshape(b, -1, 2 * C_KV_HEADS * C_HEAD_DIM)
    kit = pad_s(keys_i).transpose(0, 2, 1).astype(BF16)
    att = _dsa(u, kt, vv, kit, tq, top_k, causal, limit)
    x2 = _proj_ln(att.reshape(m, C_WIDTH), x.reshape(m, D_MODEL), p['w_out'], p['ln1_g'], p['ln1_b'],
                  _pick(m, (512, 256, 128, 64, 32, 16, 8)))
    x2 = _experts(x2, p['router_w'], p['router_b'], p['e_gate'], p['e_up'], p['e_down'],
                  p['ln2_g'], p['ln2_b'], _pick(m, (1024, 512, 256, 128)), routed=True)
    st = (k_new.reshape(b, t, C_KV_HEADS, C_HEAD_DIM), v_new.reshape(b, t, C_KV_HEADS, C_HEAD_DIM), ki_new)
    return x2.reshape(b, t, D_MODEL), st


def _trunk(x, shift0, wkv0, conv0, c0, n0, m0, past_k, past_v, past_ki, ev, od):
    even_out = [[] for _ in range(6)]
    odd_out = [[] for _ in range(3)]
    for layer in range(DEPTH):
        i = layer // 2
        if layer % 2 == 0:
            x, st = _even_layer(x, shift0[i], wkv0[i], conv0[i], c0[i], n0[i], m0[i], ev[i])
            for acc, val in zip(even_out, st):
                acc.append(val)
        else:
            if past_k is None:
                x, st = _odd_layer(x, None, None, None, od[i])
            else:
                x, st = _odd_layer(x, past_k[i], past_v[i], past_ki[i], od[i])
            for acc, val in zip(odd_out, st):
                acc.append(val)
    return x, [jnp.stack(v) for v in even_out], [jnp.stack(v) for v in odd_out]


def kernel(x_prompt, x_sample, state_shift, state_wkv, state_conv, state_c, state_n, state_m,
           cache_k, cache_v, cache_idx_k,
           ev_w_in, a_mu, a_w0, a_w2, a_a0, a_a2, a_g2, a_k_k, a_k_a, a_r_k, a_ln_g, a_ln_b,
           b_conv_w, b_conv_b, b_i_bias, b_f_bias, b_ln_g, b_ln_b, ev_w_out, ev_ln1_g, ev_ln1_b,
           ffn_w_gate, ffn_w_up, ffn_w_down, ev_ln2_g, ev_ln2_b,
           od_w_in, od_w_out, od_ln1_g, od_ln1_b, moe_w_router, moe_b_router,
           moe_w_gate, moe_w_up, moe_w_down, od_ln2_g, od_ln2_b):
    ew = {'w_in': ev_w_in, 'mu': a_mu, 'w0': a_w0, 'w2': a_w2, 'a0': a_a0, 'a2': a_a2, 'g2': a_g2,
          'k_k': a_k_k, 'k_a': a_k_a, 'r_k': a_r_k, 'a_ln_g': a_ln_g, 'a_ln_b': a_ln_b,
          'conv_w': b_conv_w, 'conv_b': b_conv_b, 'b_i': b_i_bias, 'b_f': b_f_bias,
          'b_ln_g': b_ln_g, 'b_ln_b': b_ln_b, 'w_out': ev_w_out, 'ln1_g': ev_ln1_g, 'ln1_b': ev_ln1_b,
          'ffn_gate': ffn_w_gate, 'ffn_up': ffn_w_up, 'ffn_down': ffn_w_down,
          'ln2_g': ev_ln2_g, 'ln2_b': ev_ln2_b}
    ow = {'w_in': od_w_in, 'w_out': od_w_out, 'ln1_g': od_ln1_g, 'ln1_b': od_ln1_b,
          'router': moe_w_router, 'router_b': moe_b_router, 'e_gate': moe_w_gate, 'e_up': moe_w_up,
          'e_down': moe_w_down, 'ln2_g': od_ln2_g, 'ln2_b': od_ln2_b}
    n_even, n_odd = ev_w_in.shape[0], od_w_in.shape[0]
    ev = [_prep_even({k: v[i] for k, v in ew.items()}) for i in range(n_even)]
    od = [_prep_odd({k: v[i] for k, v in ow.items()}) for i in range(n_odd)]

    bp = x_prompt.shape[0]
    z = functools.partial(jnp.zeros, dtype=F32)
    y_p, ep, op = _trunk(x_prompt,
                         z((n_even, bp, 1, A_COLS)), z((n_even, bp, A_HEADS, A_HEAD_DIM, A_HEAD_DIM)),
                         z((n_even, bp, 3, 2 * B_QK_WIDTH)), z((n_even, bp, B_HEADS, B_V_DIM, B_QK_DIM)),
                         z((n_even, bp, B_HEADS, B_QK_DIM)), z((n_even, bp, B_HEADS)),
                         None, None, None, ev, od)
    y_s, es, os_ = _trunk(x_sample, state_shift, state_wkv, state_conv, state_c, state_n, state_m,
                          cache_k, cache_v, cache_idx_k, ev, od)
    return (y_p, y_s,
            ep[0], es[0], ep[1], es[1], ep[2], es[2], ep[3], es[3], ep[4], es[4], ep[5], es[5],
            op[0], os_[0], op[1], os_[1], op[2], os_[2])
```
